```python
import jax
import jax.numpy as jnp
from jax import lax
import numpy as np

D_MODEL = 4096
BATCH = 2
SEQ = 4096
DEPTH = 4

N_MIXERS = 3
N_ML = (DEPTH + 2) // 3
N_RW = (DEPTH + 1) // 3
N_HG = DEPTH // 3
NORM_EPS = 1e-6
PLE_DIM = 256

ML_HEADS = 8
ML_QK = D_MODEL // 2
ML_V = D_MODEL
ML_DK = ML_QK // ML_HEADS
ML_DV = ML_V // ML_HEADS
ML_CHUNK = 128
ML_CONV = 3
GATE_CAP = 15.0
ML_IN = 2 * ML_QK + 2 * ML_V + 4 * ML_HEADS

RW_HEAD = 64
RW_HEADS = D_MODEL // RW_HEAD
RW_DECAY_LORA = max(32, int(round(1.8 * D_MODEL ** 0.5 / 32)) * 32)
RW_AAA_LORA = max(32, int(round(1.8 * D_MODEL ** 0.5 / 32)) * 32)
RW_GATE_LORA = max(32, int(round(0.6 * D_MODEL ** 0.8 / 32)) * 32)
RW_GN_EPS = 64e-5

HG_DK = 128
HG_HEADS = D_MODEL // HG_DK
HG_DV = D_MODEL // HG_HEADS
HG_CHUNK = 64

N_EXPERTS = 16
EC_CAPACITY = 2
EXPERT_FF = D_MODEL * 3 // 16

kernel_name = 'hybrid_mlstm_rwkv7_hgrn2_ec_moe_encoder'

F32 = jnp.float32


def rms_norm(x, gain):
    x32 = x.astype(F32)
    y = x32 * lax.rsqrt(jnp.mean(x32 * x32, axis=-1, keepdims=True) + NORM_EPS)
    return (y * gain.astype(F32)).astype(x.dtype)


def head_rms(x, gain):
    x32 = x.astype(F32)
    return x32 * lax.rsqrt(jnp.mean(x32 * x32, axis=-1, keepdims=True) + NORM_EPS) * gain.astype(F32)


def centred_dwconv(x, w):
    width, ch = w.shape
    pad = (width - 1) // 2
    return lax.conv_general_dilated(x, w[:, None, :].astype(x.dtype), window_strides=(1,),
                                    padding=[(pad, pad)], dimension_numbers=('NWC', 'WIO', 'NWC'),
                                    feature_group_count=ch)


def centred_shift(x):
    prev = jnp.pad(x[:, :-1], ((0, 0), (1, 0), (0, 0)))
    nxt = jnp.pad(x[:, 1:], ((0, 0), (0, 1), (0, 0)))
    return 0.5 * (prev + nxt)


def to_chunks(t, chunk):
    b, h, s = t.shape[:3]
    return jnp.moveaxis(t.reshape(b, h, s // chunk, chunk, *t.shape[3:]), 2, 0)


def from_chunks(t):
    nc, b, h, c, d = t.shape
    return jnp.moveaxis(t, 0, 2).reshape(b, h, nc * c, d)


def mlstm_scan(q, k, v, ig, lf):
    B, H, S, DK = q.shape
    DV = v.shape[-1]
    L = ML_CHUNK
    mask = jnp.tril(jnp.ones((L, L), bool))

    def step(carry, inp):
        C, n, m = carry
        qc, kc, vc, ic, fc = inp
        b = jnp.cumsum(fc, axis=-1)
        g = b[..., -1]
        dmat = jnp.where(mask, b[..., :, None] - b[..., None, :] + ic[..., None, :], -jnp.inf)
        inter_log = b + m[..., None]
        m_t = jnp.maximum(inter_log, jnp.max(dmat, axis=-1))
        s = jnp.einsum('bhtd,bhsd->bhts', qc, kc) * jnp.exp(dmat - m_t[..., None])
        w_inter = jnp.exp(inter_log - m_t)
        num = jnp.einsum('bhts,bhse->bhte', s, vc) + w_inter[..., None] * jnp.einsum('bhtd,bhed->bhte', qc, C)
        den = jnp.sum(s, axis=-1) + w_inter * jnp.einsum('bhtd,bhd->bht', qc, n)
        h = num / jnp.maximum(jnp.abs(den), jnp.exp(-m_t))[..., None]
        src = g[..., None] - b + ic
        m_new = jnp.maximum(g + m, jnp.max(src, axis=-1))
        w_old = jnp.exp(g + m - m_new)
        w_src = jnp.exp(src - m_new[..., None])
        C = w_old[..., None, None] * C + jnp.einsum('bhs,bhse,bhsd->bhed', w_src, vc, kc)
        n = w_old[..., None] * n + jnp.einsum('bhs,bhsd->bhd', w_src, kc)
        return (C, n, m_new), h

    init = (jnp.zeros((B, H, DV, DK), F32), jnp.zeros((B, H, DK), F32), jnp.full((B, H), -1e30, F32))
    xs = (to_chunks(q, L), to_chunks(k, L), to_chunks(v, L), to_chunks(ig, L), to_chunks(lf, L))
    _, hs = lax.scan(step, init, xs)
    return from_chunks(hs)


def mlstm_mixer(xn, w_in, gate_bias, conv_w, head_gain, w_out):
    B, S, _ = xn.shape
    z = xn @ w_in
    qk = centred_dwconv(z[..., :2 * ML_QK], conv_w)
    v = z[..., 2 * ML_QK:2 * ML_QK + ML_V]
    o = z[..., 2 * ML_QK + ML_V:2 * ML_QK + 2 * ML_V]
    gates = z[..., 2 * ML_QK + 2 * ML_V:].astype(F32) + gate_bias.astype(F32)
    gates = GATE_CAP * jnp.tanh(gates / GATE_CAP)
    gates = gates.reshape(B, S, 4, ML_HEADS).transpose(2, 0, 3, 1)
    i_fwd, lf_fwd = gates[0], jax.nn.log_sigmoid(gates[1])
    i_bwd, lf_bwd = gates[2], jax.nn.log_sigmoid(gates[3])
    heads = lambda t, d: t.reshape(B, S, ML_HEADS, d).transpose(0, 2, 1, 3).astype(F32)
    q = heads(qk[..., :ML_QK], ML_DK) * (ML_DK ** -0.5)
    k = heads(qk[..., ML_QK:], ML_DK)
    vh = heads(v, ML_DV)
    fl = lambda t: jnp.flip(t, axis=2)
    h = mlstm_scan(q, k, vh, i_fwd, lf_fwd) + fl(mlstm_scan(fl(q), fl(k), fl(vh), fl(i_bwd), fl(lf_bwd)))
    h = head_rms(h.transpose(0, 2, 1, 3), head_gain.reshape(ML_HEADS, ML_DV))
    h = h.reshape(B, S, ML_V).astype(xn.dtype)
    return (jax.nn.sigmoid(o) * h) @ w_out


def rwkv7_scan(r, w, kk, a, k, v):
    _, B, H, N = r.shape

    def step(st, inp):
        r_t, w_t, kk_t, a_t, k_t, v_t = inp
        sa = jnp.einsum('bhvk,bhk->bhv', st, -kk_t)
        st = st * w_t[:, :, None, :] + sa[..., None] * (kk_t * a_t)[:, :, None, :] + v_t[..., None] * k_t[:, :, None, :]
        return st, jnp.einsum('bhvk,bhk->bhv', st, r_t)

    _, y = lax.scan(step, jnp.zeros((B, H, N, N), F32), (r, w, kk, a, k, v))
    return y


def rwkv7_mixer(xn, mu, w_rkv, w0, w1, w2, a0, a1, a2, g1, g2, k_k, k_a, r_k, ln_gain, ln_bias, w_o):
    B, S, D = xn.shape
    H, N = RW_HEADS, RW_HEAD
    xmix = xn[:, :, None, :] + (centred_shift(xn) - xn)[:, :, None, :] * mu
    rkv = jnp.einsum('bsjd,jde->bsje', xmix[:, :, :3], w_rkv)
    r32, k32, v32 = rkv[:, :, 0].astype(F32), rkv[:, :, 1].astype(F32), rkv[:, :, 2].astype(F32)
    xw, xa, xg = xmix[:, :, 3], xmix[:, :, 4], xmix[:, :, 5]
    wl = w0 + jnp.einsum('bsjr,jrd->bsjd', jnp.tanh(jnp.einsum('bsd,jdr->bsjr', xw, w1)), w2)
    decay = jnp.exp(-jnp.exp(-jax.nn.softplus(-wl.astype(F32)) - 0.5))
    a = jax.nn.sigmoid((a0 + jnp.einsum('bsjr,jrd->bsjd', jnp.einsum('bsd,jdr->bsjr', xa, a1), a2)).astype(F32))
    gate = jax.nn.sigmoid(xg @ g1) @ g2
    kk = (k32 * k_k.astype(F32)).reshape(B, S, H, N)
    kk = kk / jnp.maximum(jnp.sqrt(jnp.sum(kk * kk, axis=-1, keepdims=True)), 1e-12)
    k_dir = k32[:, :, None, :] * (1.0 + (a - 1.0) * k_a.astype(F32))
    tm = lambda t: t.reshape(B, S, H, N).transpose(1, 0, 2, 3)
    fl = lambda t: jnp.flip(t, axis=0)
    r_t, v_t, kk_t = tm(r32), tm(v32), kk.transpose(1, 0, 2, 3)
    y = rwkv7_scan(r_t, tm(decay[:, :, 0]), kk_t, tm(a[:, :, 0]), tm(k_dir[:, :, 0]), v_t)
    y = y + fl(rwkv7_scan(fl(r_t), fl(tm(decay[:, :, 1])), fl(kk_t), fl(tm(a[:, :, 1])), fl(tm(k_dir[:, :, 1])), fl(v_t)))
    y = y.transpose(1, 0, 2, 3)
    mean = jnp.mean(y, axis=-1, keepdims=True)
    var = jnp.mean(jnp.square(y - mean), axis=-1, keepdims=True)
    yn = ((y - mean) * lax.rsqrt(var + RW_GN_EPS)).reshape(B, S, D) * ln_gain.astype(F32) + ln_bias.astype(F32)
    k_bonus = k32 * (1.0 + (jnp.mean(a, axis=2) - 1.0) * k_a.astype(F32))
    bonus = jnp.sum((r32 * k_bonus * r_k.astype(F32)).reshape(B, S, H, N), axis=-1, keepdims=True) * v32.reshape(B, S, H, N)
    out = (yn + bonus.reshape(B, S, D)).astype(xn.dtype) * gate
    return out @ w_o


def gla_scan(q, k, v, lf):
    B, H, S, DK = q.shape
    DV = v.shape[-1]
    L = HG_CHUNK
    mask = jnp.tril(jnp.ones((L, L), bool))[:, :, None]

    def step(st, inp):
        qc, kc, vc, fc = inp
        b = jnp.cumsum(fc, axis=2)
        diff = b[:, :, :, None, :] - b[:, :, None, :, :]
        dec = jnp.exp(jnp.where(mask, diff, -jnp.inf))
        att = jnp.einsum('bhtd,bhsd,bhtsd->bhts', qc, kc, dec)
        o = jnp.einsum('bhts,bhse->bhte', att, vc) + jnp.einsum('bhtd,bhde->bhte', qc * jnp.exp(b), st)
        b_end = b[:, :, -1]
        st = jnp.exp(b_end)[..., None] * st + jnp.einsum('bhsd,bhse->bhde', kc * jnp.exp(b_end[:, :, None] - b), vc)
        return st, o

    xs = (to_chunks(q, L), to_chunks(k, L), to_chunks(v, L), to_chunks(lf, L))
    _, os_ = lax.scan(step, jnp.zeros((B, H, DK, DV), F32), xs)
    return from_chunks(os_)


def hgrn2_mixer(xn, w_in, f_bias, lb_logits, layer_idx, head_gain, w_out):
    B, S, D = xn.shape
    z = xn @ w_in
    q, zf_fwd, zf_bwd, i, g = jnp.split(z, 5, axis=-1)
    probs = jax.nn.softmax(lb_logits.astype(F32), axis=0)
    lb = (jnp.cumsum(probs, axis=0) - probs[0])[layer_idx]
    hd = lambda t, d: t.reshape(B, S, HG_HEADS, d).transpose(0, 2, 1, 3)

    def forget(zf, bias):
        zf = zf.astype(F32) + bias.astype(F32)
        log_f = jnp.logaddexp(jnp.log(lb), jnp.log1p(-lb) + jax.nn.log_sigmoid(zf))
        k = (1.0 - lb) * jax.nn.sigmoid(-zf)
        return hd(log_f, HG_DK), hd(k, HG_DK)

    lf_fwd, k_fwd = forget(zf_fwd, f_bias[0])
    lf_bwd, k_bwd = forget(zf_bwd, f_bias[1])
    qh = hd(jax.nn.silu(q.astype(F32)), HG_DK) * (HG_DK ** -0.5)
    vh = hd(i.astype(F32), HG_DV)
    fl = lambda t: jnp.flip(t, axis=2)
    o = gla_scan(qh, k_fwd, vh, lf_fwd) + fl(gla_scan(fl(qh), fl(k_bwd), fl(vh), fl(lf_bwd)))
    o = head_rms(o.transpose(0, 2, 1, 3), head_gain.reshape(HG_HEADS, HG_DV)).reshape(B, S, D).astype(xn.dtype)
    return (o * jax.nn.silu(g)) @ w_out


def expert_choice_ffn(xn, router, w_gate, w_up, w_down):
    B, T, D = xn.shape
    cap = EC_CAPACITY * T // N_EXPERTS
    aff = jax.nn.softmax((xn @ router).astype(F32), axis=-1)
    gsel, idx = lax.top_k(jnp.swapaxes(aff, 1, 2), cap)
    xg = jax.vmap(lambda xb, ib: xb[ib])(xn, idx)
    hid = jax.nn.silu(jnp.einsum('becd,edf->becf', xg, w_gate)) * jnp.einsum('becd,edf->becf', xg, w_up)
    out = jnp.einsum('becf,efd->becd', hid, w_down) * gsel[..., None].astype(xn.dtype)
    return jax.vmap(lambda ib, ob: jnp.zeros((T, D), xn.dtype).at[ib.reshape(-1)].add(ob.reshape(-1, D)))(idx, out)


def per_layer_embedding(h, p_i, w_ple, post_gain, gate_gain, g_down, g_up):
    e = rms_norm(p_i.astype(h.dtype) @ w_ple, post_gain)
    gate = jax.nn.sigmoid((rms_norm(h, gate_gain) @ g_down) @ g_up)
    return e * gate


def setup_inputs(seed: int = 0) -> dict:
    key = jax.random.key(seed)
    ks = iter(jax.random.split(key, 64))

    def nrm(shape, scale):
        return scale * jax.random.normal(next(ks), shape, F32)

    def unif(shape, lo, hi):
        return jax.random.uniform(next(ks), shape, F32, lo, hi)

    def gain(shape):
        return 1.0 + nrm(shape, 0.02)

    D = D_MODEL
    centre = jnp.zeros((ML_CONV, 1), F32).at[ML_CONV // 2].set(1.0)
    ml_gate_bias = jnp.stack([nrm((N_ML, ML_HEADS), 0.5) - 1.0, unif((N_ML, ML_HEADS), 3.0, 6.0),
                              nrm((N_ML, ML_HEADS), 0.5) - 1.0, unif((N_ML, ML_HEADS), 3.0, 6.0)],
                             axis=1).reshape(N_ML, 4 * ML_HEADS)
    return {
        'x': nrm((BATCH, SEQ, D), 1.0),
        'p': nrm((DEPTH, BATCH, SEQ, PLE_DIM), 1.0),
        'norm_mix': gain((DEPTH, D)),
        'norm_ffn': gain((DEPTH, D)),
        'norm_ple_gate': gain((DEPTH, D)),
        'norm_ple_post': gain((DEPTH, D)),
        'norm_final': gain((D,)),
        'ml_w_in': nrm((N_ML, D, ML_IN), D ** -0.5),
        'ml_gate_bias': ml_gate_bias,
        'ml_conv': nrm((N_ML, ML_CONV, 2 * ML_QK), 0.3) + centre,
        'ml_head_gain': gain((N_ML, ML_V)),
        'ml_w_out': nrm((N_ML, ML_V, D), ML_V ** -0.5),
        'rw_mu': unif((N_RW, 6, D), 0.0, 1.0),
        'rw_w_rkv': nrm((N_RW, 3, D, D), D ** -0.5),
        'rw_w0': unif((N_RW, 2, D), -6.0, -1.0),
        'rw_w1': nrm((N_RW, 2, D, RW_DECAY_LORA), D ** -0.5),
        'rw_w2': nrm((N_RW, 2, RW_DECAY_LORA, D), 0.5 * RW_DECAY_LORA ** -0.5),
        'rw_a0': nrm((N_RW, 2, D), 0.5),
        'rw_a1': nrm((N_RW, 2, D, RW_AAA_LORA), D ** -0.5),
        'rw_a2': nrm((N_RW, 2, RW_AAA_LORA, D), 0.5 * RW_AAA_LORA ** -0.5),
        'rw_g1': nrm((N_RW, D, RW_GATE_LORA), D ** -0.5),
        'rw_g2': nrm((N_RW, RW_GATE_LORA, D), RW_GATE_LORA ** -0.5),
        'rw_k_k': 0.85 + nrm((N_RW, D), 0.05),
        'rw_k_a': 1.0 + nrm((N_RW, D), 0.05),
        'rw_r_k': nrm((N_RW, D), 0.1),
        'rw_ln_gain': gain((N_RW, D)),
        'rw_ln_bias': nrm((N_RW, D), 0.02),
        'rw_w_o': nrm((N_RW, D, D), D ** -0.5),
        'hg_w_in': nrm((N_HG, D, 5 * D), D ** -0.5),
        'hg_f_bias': 2.0 + nrm((N_HG, 2, D), 0.5),
        'hg_lb': nrm((DEPTH, D), 0.1),
        'hg_head_gain': gain((N_HG, D)),
        'hg_w_out': nrm((N_HG, D, D), D ** -0.5),
        'moe_router': nrm((DEPTH, D, N_EXPERTS), D ** -0.5),
        'moe_w_gate': nrm((DEPTH, N_EXPERTS, D, EXPERT_FF), D ** -0.5),
        'moe_w_up': nrm((DEPTH, N_EXPERTS, D, EXPERT_FF), D ** -0.5),
        'moe_w_down': nrm((DEPTH, N_EXPERTS, EXPERT_FF, D), EXPERT_FF ** -0.5),
        'ple_w': nrm((DEPTH, PLE_DIM, D), PLE_DIM ** -0.5),
        'ple_gate_down': nrm((DEPTH, D, PLE_DIM), D ** -0.5),
        'ple_gate_up': nrm((DEPTH, PLE_DIM, D), PLE_DIM ** -0.5),
    }


def reference(x, p, norm_mix, norm_ffn, norm_ple_gate, norm_ple_post, norm_final,
              ml_w_in, ml_gate_bias, ml_conv, ml_head_gain, ml_w_out,
              rw_mu, rw_w_rkv, rw_w0, rw_w1, rw_w2, rw_a0, rw_a1, rw_a2, rw_g1, rw_g2,
              rw_k_k, rw_k_a, rw_r_k, rw_ln_gain, rw_ln_bias, rw_w_o,
              hg_w_in, hg_f_bias, hg_lb, hg_head_gain, hg_w_out,
              moe_router, moe_w_gate, moe_w_up, moe_w_down,
              ple_w, ple_gate_down, ple_gate_up):
    h = x
    for i in range(DEPTH):
        kind, j = i % N_MIXERS, i // N_MIXERS
        hn = rms_norm(h, norm_mix[i])
        if kind == 0:
            h = h + mlstm_mixer(hn, ml_w_in[j], ml_gate_bias[j], ml_conv[j], ml_head_gain[j], ml_w_out[j])
        elif kind == 1:
            h = h + rwkv7_mixer(hn, rw_mu[j], rw_w_rkv[j], rw_w0[j], rw_w1[j], rw_w2[j], rw_a0[j], rw_a1[j],
                                rw_a2[j], rw_g1[j], rw_g2[j], rw_k_k[j], rw_k_a[j], rw_r_k[j],
                                rw_ln_gain[j], rw_ln_bias[j], rw_w_o[j])
        else:
            h = h + hgrn2_mixer(hn, hg_w_in[j], hg_f_bias[j], hg_lb, i, hg_head_gain[j], hg_w_out[j])
        h = h + expert_choice_ffn(rms_norm(h, norm_ffn[i]), moe_router[i], moe_w_gate[i], moe_w_up[i], moe_w_down[i])
        h = h + per_layer_embedding(h, p[i], ple_w[i], norm_ple_post[i], norm_ple_gate[i], ple_gate_down[i], ple_gate_up[i])
    return rms_norm(h, norm_final)
```

```python
import functools

import jax
import jax.numpy as jnp
from jax import lax
from jax.experimental import pallas as pl
from jax.experimental.pallas import tpu as pltpu

F32 = jnp.float32
BF16 = jnp.bfloat16

NORM_EPS = 1e-6
GATE_CAP = 15.0
ML_CHUNK = 128
RW_HEAD = 64
RW_GN_EPS = 64e-5
HG_DK = 128
HG_CHUNK = 64
EC_CAPACITY = 2

LANES = 128
VMEM_LIMIT = 52 * 1024 * 1024


def _cparams(*sem):
    return pltpu.CompilerParams(dimension_semantics=sem, vmem_limit_bytes=VMEM_LIMIT)


def _tile(n, prefs):
    for t in prefs:
        if n % t == 0:
            return t
    return n


def _rms(x, gain):
    return x * lax.rsqrt(jnp.mean(x * x, axis=-1, keepdims=True) + NORM_EPS) * gain


def _norm_kernel(h_ref, g_ref, o_ref):
    o_ref[...] = _rms(h_ref[...], g_ref[...]).astype(o_ref.dtype)


def rms_norm_bf16(h, gain):
    T, D = h.shape
    tm = _tile(T, (256, 128, 64, 32, 16, 8))
    return pl.pallas_call(
        _norm_kernel,
        grid=(T // tm,),
        in_specs=[pl.BlockSpec((tm, D), lambda i: (i, 0)), pl.BlockSpec((1, D), lambda i: (0, 0))],
        out_specs=pl.BlockSpec((tm, D), lambda i: (i, 0)),
        out_shape=jax.ShapeDtypeStruct((T, D), BF16),
        compiler_params=_cparams("parallel"),
        name="rms_norm",
    )(h, gain.reshape(1, D))


def _norm_router_kernel(h_ref, g_ref, r_ref, o_ref, l_ref):
    y = _rms(h_ref[...], g_ref[...])
    o_ref[...] = y.astype(o_ref.dtype)
    l_ref[...] = jnp.dot(y, r_ref[...], preferred_element_type=F32, precision=lax.Precision.HIGHEST)


def rms_norm_router(h, gain, router):
    T, D = h.shape
    E = router.shape[1]
    rp = jnp.pad(router, ((0, 0), (0, LANES - E)))
    tm = _tile(T, (256, 128, 64, 32, 16, 8))
    hn, logits = pl.pallas_call(
        _norm_router_kernel,
        grid=(T // tm,),
        in_specs=[pl.BlockSpec((tm, D), lambda i: (i, 0)), pl.BlockSpec((1, D), lambda i: (0, 0)),
                  pl.BlockSpec((D, LANES), lambda i: (0, 0))],
        out_specs=[pl.BlockSpec((tm, D), lambda i: (i, 0)), pl.BlockSpec((tm, LANES), lambda i: (i, 0))],
        out_shape=[jax.ShapeDtypeStruct((T, D), BF16), jax.ShapeDtypeStruct((T, LANES), F32)],
        compiler_params=_cparams("parallel"),
        name="rms_norm_router",
    )(h, gain.reshape(1, D), rp)
    return hn, logits[:, :E]


_CAST_ROWS = 256


def _cast_weight(w_ref, wb_ref):
    k = w_ref.shape[0]
    rows = _tile(k, (_CAST_ROWS, 128, 64, 32, 16))

    def body(c, _):
        r = pl.multiple_of(c * rows, rows)
        wb_ref[pl.ds(r, rows), :] = w_ref[pl.ds(r, rows), :].astype(BF16)
        return 0

    lax.fori_loop(0, k // rows, body, 0)


def _mm_kernel(x_ref, w_ref, *rest, has_res, has_bias, act):
    rest = list(rest)
    b_ref = rest.pop(0) if has_bias else None
    r_ref = rest.pop(0) if has_res else None
    o_ref, wb_ref = rest

    @pl.when(pl.program_id(1) == 0)
    def _():
        _cast_weight(w_ref, wb_ref)

    acc = jnp.dot(x_ref[...], wb_ref[...], preferred_element_type=F32)
    if has_bias:
        acc = acc + b_ref[...]
    if act is not None:
        acc = act(acc)
    if has_res:
        acc = acc + r_ref[...]
    o_ref[...] = acc.astype(o_ref.dtype)


def matmul(x, w, *, n=None, col_off=0, bias=None, residual=None, act=None, out_dtype=F32, tm=None, tn=None):
    M, K = x.shape
    if n is None:
        n = w.shape[1] - col_off
    tm = tm or _tile(M, (1024, 512, 256, 128, 64, 32, 16))
    tn = tn or _tile(n, (512, 256, 128))
    assert n % tn == 0 and col_off % tn == 0 and M % tm == 0
    cb = col_off // tn
    in_specs = [pl.BlockSpec((tm, K), lambda j, i: (i, 0)), pl.BlockSpec((K, tn), lambda j, i: (0, j + cb))]
    args = [x, w]
    if bias is not None:
        in_specs.append(pl.BlockSpec((1, tn), lambda j, i: (0, j)))
        args.append(bias.reshape(1, n))
    if residual is not None:
        in_specs.append(pl.BlockSpec((tm, tn), lambda j, i: (i, j)))
        args.append(residual)
    return pl.pallas_call(
        functools.partial(_mm_kernel, has_res=residual is not None, has_bias=bias is not None, act=act),
        grid=(n // tn, M // tm),
        in_specs=in_specs,
        out_specs=pl.BlockSpec((tm, tn), lambda j, i: (i, j)),
        out_shape=jax.ShapeDtypeStruct((M, n), out_dtype),
        scratch_shapes=[pltpu.VMEM((K, tn), BF16)],
        compiler_params=_cparams("arbitrary", "arbitrary"),
        name="matmul",
    )(*args)


def _lora_kernel(x_ref, a_ref, b_ref, bias_ref, o_ref, ab_ref, bb_ref, *, act1, act2):
    @pl.when(pl.program_id(0) == 0)
    def _():
        _cast_weight(a_ref, ab_ref)
        _cast_weight(b_ref, bb_ref)

    t = jnp.dot(x_ref[...], ab_ref[...], preferred_element_type=F32)
    if act1 is not None:
        t = act1(t)
    y = jnp.dot(t.astype(BF16), bb_ref[...], preferred_element_type=F32) + bias_ref[...]
    if act2 is not None:
        y = act2(y)
    o_ref[...] = y.astype(o_ref.dtype)


def lora(x, a, b, bias=None, act1=None, act2=None, out_dtype=F32):
    M, K = x.shape
    R, N = b.shape
    rp = -R % LANES
    if rp:
        a = jnp.pad(a, ((0, 0), (0, rp)))
        b = jnp.pad(b, ((0, rp), (0, 0)))
        R += rp
    if bias is None:
        bias = jnp.zeros((N,), F32)
    tm = _tile(M, (256, 128, 64, 32, 16))
    return pl.pallas_call(
        functools.partial(_lora_kernel, act1=act1, act2=act2),
        grid=(M // tm,),
        in_specs=[pl.BlockSpec((tm, K), lambda i: (i, 0)), pl.BlockSpec((K, R), lambda i: (0, 0)),
                  pl.BlockSpec((R, N), lambda i: (0, 0)), pl.BlockSpec((1, N), lambda i: (0, 0))],
        out_specs=pl.BlockSpec((tm, N), lambda i: (i, 0)),
        out_shape=jax.ShapeDtypeStruct((M, N), out_dtype),
        scratch_shapes=[pltpu.VMEM((K, R), BF16), pltpu.VMEM((R, N), BF16)],
        compiler_params=_cparams("arbitrary"),
        name="lora",
    )(x, a, b, bias.reshape(1, N))


def _ple_kernel(h_ref, p_ref, wp_ref, gd_ref, gu_ref, gg_ref, pg_ref, ng_ref, ho_ref, no_ref,
                wpb_ref, gdb_ref, gub_ref):
    @pl.when(pl.program_id(0) == 0)
    def _():
        _cast_weight(wp_ref, wpb_ref)
        _cast_weight(gd_ref, gdb_ref)
        _cast_weight(gu_ref, gub_ref)

    h = h_ref[...]
    e = jnp.dot(p_ref[...].astype(BF16), wpb_ref[...], preferred_element_type=F32)
    e = _rms(e, pg_ref[...])
    t = jnp.dot(_rms(h, gg_ref[...]).astype(BF16), gdb_ref[...], preferred_element_type=F32)
    g = jnp.dot(t.astype(BF16), gub_ref[...], preferred_element_type=F32)
    hn = h + e * jax.nn.sigmoid(g)
    ho_ref[...] = hn
    no_ref[...] = _rms(hn, ng_ref[...]).astype(no_ref.dtype)


def ple_layer(h, p, w_ple, g_down, g_up, gate_gain, post_gain, next_gain, next_dtype):
    T, D = h.shape
    P = p.shape[1]
    tm = _tile(T, (256, 128, 64, 32, 16, 8))
    row = lambda i: (i, 0)
    fix = lambda i: (0, 0)
    return pl.pallas_call(
        _ple_kernel,
        grid=(T // tm,),
        in_specs=[pl.BlockSpec((tm, D), row), pl.BlockSpec((tm, P), row), pl.BlockSpec((P, D), fix),
                  pl.BlockSpec((D, P), fix), pl.BlockSpec((P, D), fix), pl.BlockSpec((1, D), fix),
                  pl.BlockSpec((1, D), fix), pl.BlockSpec((1, D), fix)],
        out_specs=[pl.BlockSpec((tm, D), row), pl.BlockSpec((tm, D), row)],
        out_shape=[jax.ShapeDtypeStruct((T, D), F32), jax.ShapeDtypeStruct((T, D), next_dtype)],
        scratch_shapes=[pltpu.VMEM((P, D), BF16), pltpu.VMEM((D, P), BF16), pltpu.VMEM((P, D), BF16)],
        compiler_params=_cparams("arbitrary"),
        name="ple",
    )(h, p, w_ple, g_down, g_up, gate_gain.reshape(1, D), post_gain.reshape(1, D), next_gain.reshape(1, D))


def _moe_up_kernel(x_ref, wg_ref, wu_ref, o_ref, wgb_ref, wub_ref):
    _cast_weight(wg_ref.at[0], wgb_ref)
    _cast_weight(wu_ref.at[0], wub_ref)
    x = x_ref[0]
    g = jnp.dot(x, wgb_ref[...], preferred_element_type=F32)
    u = jnp.dot(x, wub_ref[...], preferred_element_type=F32)
    o_ref[0] = (g * jax.nn.sigmoid(g) * u).astype(o_ref.dtype)


def _moe_down_kernel(h_ref, wd_ref, s_ref, o_ref, wdb_ref):
    _cast_weight(wd_ref.at[0], wdb_ref)
    o_ref[0] = jnp.dot(h_ref[0], wdb_ref[...], preferred_element_type=F32) * s_ref[0]


def moe_ffn(xg, w_gate, w_up, w_down, gsel):
    E, R, D = xg.shape
    FF = w_gate.shape[2]
    tf = _tile(FF, (256, 128))
    hid = pl.pallas_call(
        _moe_up_kernel,
        grid=(E, FF // tf),
        in_specs=[pl.BlockSpec((1, R, D), lambda e, f: (e, 0, 0)),
                  pl.BlockSpec((1, D, tf), lambda e, f: (e, 0, f)),
                  pl.BlockSpec((1, D, tf), lambda e, f: (e, 0, f))],
        out_specs=pl.BlockSpec((1, R, tf), lambda e, f: (e, 0, f)),
        out_shape=jax.ShapeDtypeStruct((E, R, FF), BF16),
        scratch_shapes=[pltpu.VMEM((D, tf), BF16), pltpu.VMEM((D, tf), BF16)],
        compiler_params=_cparams("parallel", "parallel"),
        name="moe_up",
    )(xg, w_gate, w_up)
    tn = _tile(D, (1024, 512, 256, 128))
    return pl.pallas_call(
        _moe_down_kernel,
        grid=(E, D // tn),
        in_specs=[pl.BlockSpec((1, R, FF), lambda e, j: (e, 0, 0)),
                  pl.BlockSpec((1, FF, tn), lambda e, j: (e, 0, j)),
                  pl.BlockSpec((1, R, 1), lambda e, j: (e, 0, 0))],
        out_specs=pl.BlockSpec((1, R, tn), lambda e, j: (e, 0, j)),
        out_shape=jax.ShapeDtypeStruct((E, R, D), F32),
        scratch_shapes=[pltpu.VMEM((FF, tn), BF16)],
        compiler_params=_cparams("parallel", "parallel"),
        name="moe_down",
    )(hid, w_down, gsel.reshape(E, R, 1))


def moe_layer(h, hn, logits, w_gate, w_up, w_down, B, S):
    T, D = h.shape
    E = logits.shape[1]
    cap = EC_CAPACITY * S // E
    aff = jax.nn.softmax(logits, axis=-1).reshape(B, S, E)
    gsel, idx = lax.top_k(jnp.swapaxes(aff, 1, 2), cap)
    tok = idx + (jnp.arange(B, dtype=idx.dtype) * S)[:, None, None]
    tok = jnp.swapaxes(tok, 0, 1).reshape(E, B * cap)
    gsel = jnp.swapaxes(gsel, 0, 1).reshape(E, B * cap)
    xg = jnp.take(hn, tok, axis=0)
    out = moe_ffn(xg, w_gate, w_up, w_down, gsel)
    return h.at[tok.reshape(-1)].add(out.reshape(-1, D))


def _to_chunks(t, chunk):
    b, hh, s = t.shape[:3]
    return jnp.moveaxis(t.reshape(b, hh, s // chunk, chunk, *t.shape[3:]), 2, 0)


def _from_chunks(t):
    nc, b, hh, c, d = t.shape
    return jnp.moveaxis(t, 0, 2).reshape(b, hh, nc * c, d)


def _mlstm_scan_xla(q, k, v, ig, lf):
    B, H, S, DK = q.shape
    DV = v.shape[-1]
    L = ML_CHUNK
    mask = jnp.tril(jnp.ones((L, L), bool))

    def step(carry, inp):
        C, n, m = carry
        qc, kc, vc, ic, fc = inp
        b = jnp.cumsum(fc, axis=-1)
        g = b[..., -1]
        dmat = jnp.where(mask, b[..., :, None] - b[..., None, :] + ic[..., None, :], -jnp.inf)
        inter_log = b + m[..., None]
        m_t = jnp.maximum(inter_log, jnp.max(dmat, axis=-1))
        s = jnp.einsum('bhtd,bhsd->bhts', qc, kc) * jnp.exp(dmat - m_t[..., None])
        w_inter = jnp.exp(inter_log - m_t)
        num = jnp.einsum('bhts,bhse->bhte', s, vc) + w_inter[..., None] * jnp.einsum('bhtd,bhed->bhte', qc, C)
        den = jnp.sum(s, axis=-1) + w_inter * jnp.einsum('bhtd,bhd->bht', qc, n)
        hh = num / jnp.maximum(jnp.abs(den), jnp.exp(-m_t))[..., None]
        src = g[..., None] - b + ic
        m_new = jnp.maximum(g + m, jnp.max(src, axis=-1))
        w_old = jnp.exp(g + m - m_new)
        w_src = jnp.exp(src - m_new[..., None])
        C = w_old[..., None, None] * C + jnp.einsum('bhs,bhse,bhsd->bhed', w_src, vc, kc)
        n = w_old[..., None] * n + jnp.einsum('bhs,bhsd->bhd', w_src, kc)
        return (C, n, m_new), hh

    init = (jnp.zeros((B, H, DV, DK), F32), jnp.zeros((B, H, DK), F32), jnp.full((B, H), -1e30, F32))
    xs = (_to_chunks(q, L), _to_chunks(k, L), _to_chunks(v, L), _to_chunks(ig, L), _to_chunks(lf, L))
    _, hs = lax.scan(step, init, xs)
    return _from_chunks(hs)


def mlstm_layer(h, hn, w_in, gate_bias, conv_w, head_gain, w_out, B, S):
    T, D = h.shape
    H = gate_bias.shape[0] // 4
    QK = conv_w.shape[1] // 2
    V = w_out.shape[0]
    DK, DV = QK // H, V // H
    nmain = 2 * QK + 2 * V
    z = matmul(hn, w_in, n=nmain, out_dtype=BF16)
    wg = jnp.pad(w_in[:, nmain:], ((0, 0), (0, LANES - 4 * H)))
    gates = matmul(hn, wg)[:, :4 * H] + gate_bias
    gates = GATE_CAP * jnp.tanh(gates / GATE_CAP)
    gates = gates.reshape(B, S, 4, H).transpose(2, 0, 3, 1)
    i_fwd, lf_fwd = gates[0], jax.nn.log_sigmoid(gates[1])
    i_bwd, lf_bwd = gates[2], jax.nn.log_sigmoid(gates[3])
    z3 = z.reshape(B, S, nmain).astype(F32)
    x = z3[..., :2 * QK]
    prev = jnp.pad(x[:, :-1], ((0, 0), (1, 0), (0, 0)))
    nxt = jnp.pad(x[:, 1:], ((0, 0), (0, 1), (0, 0)))
    qk = conv_w[0] * prev + conv_w[1] * x + conv_w[2] * nxt
    v = z3[..., 2 * QK:2 * QK + V]
    o = z3[..., 2 * QK + V:]
    heads = lambda t, d: t.reshape(B, S, H, d).transpose(0, 2, 1, 3)
    q = heads(qk[..., :QK], DK) * (DK ** -0.5)
    k = heads(qk[..., QK:], DK)
    vh = heads(v, DV)
    fl = lambda t: jnp.flip(t, axis=2)
    hh = _mlstm_scan_xla(q, k, vh, i_fwd, lf_fwd) + fl(_mlstm_scan_xla(fl(q), fl(k), fl(vh), fl(i_bwd), fl(lf_bwd)))
    hh = hh.transpose(0, 2, 1, 3)
    hh = hh * lax.rsqrt(jnp.mean(hh * hh, axis=-1, keepdims=True) + NORM_EPS) * head_gain.reshape(H, DV)
    gated = (jax.nn.sigmoid(o) * hh.reshape(B, S, V)).reshape(T, V).astype(BF16)
    return matmul(gated, w_out, residual=h)


def _rwkv7_scan_xla(r, w, kk, a, k, v):
    _, B, H, N = r.shape

    def step(st, inp):
        r_t, w_t, kk_t, a_t, k_t, v_t = inp
        sa = jnp.einsum('bhvk,bhk->bhv', st, -kk_t)
        st = st * w_t[:, :, None, :] + sa[..., None] * (kk_t * a_t)[:, :, None, :] + v_t[..., None] * k_t[:, :, None, :]
        return st, jnp.einsum('bhvk,bhk->bhv', st, r_t)

    _, y = lax.scan(step, jnp.zeros((B, H, N, N), F32), (r, w, kk, a, k, v))
    return y


def rwkv_layer(h, hn, mu, w_rkv, w0, w1, w2, a0, a1, a2, g1, g2, k_k, k_a, r_k, ln_gain, ln_bias, w_o, B, S):
    T, D = h.shape
    N = RW_HEAD
    H = D // N
    xn = hn.astype(F32).reshape(B, S, D)
    prev = jnp.pad(xn[:, :-1], ((0, 0), (1, 0), (0, 0)))
    nxt = jnp.pad(xn[:, 1:], ((0, 0), (0, 1), (0, 0)))
    dx = 0.5 * (prev + nxt) - xn
    xm = [(xn + dx * mu[j]).reshape(T, D).astype(BF16) for j in range(6)]
    r32 = matmul(xm[0], w_rkv[0]).reshape(B, S, D)
    k32 = matmul(xm[1], w_rkv[1]).reshape(B, S, D)
    v32 = matmul(xm[2], w_rkv[2]).reshape(B, S, D)
    wl = jnp.stack([lora(xm[3], w1[j], w2[j], bias=w0[j], act1=jnp.tanh) for j in range(2)], 1).reshape(B, S, 2, D)
    a = jnp.stack([lora(xm[4], a1[j], a2[j], bias=a0[j], act2=jax.nn.sigmoid) for j in range(2)], 1).reshape(B, S, 2, D)
    gate = lora(xm[5], g1, g2, act1=jax.nn.sigmoid).reshape(B, S, D)
    decay = jnp.exp(-jnp.exp(-jax.nn.softplus(-wl) - 0.5))
    kk = (k32 * k_k).reshape(B, S, H, N)
    kk = kk / jnp.maximum(jnp.sqrt(jnp.sum(kk * kk, axis=-1, keepdims=True)), 1e-12)
    k_dir = k32[:, :, None, :] * (1.0 + (a - 1.0) * k_a)
    tm = lambda t: t.reshape(B, S, H, N).transpose(1, 0, 2, 3)
    fl = lambda t: jnp.flip(t, axis=0)
    r_t, v_t, kk_t = tm(r32), tm(v32), kk.transpose(1, 0, 2, 3)
    y = _rwkv7_scan_xla(r_t, tm(decay[:, :, 0]), kk_t, tm(a[:, :, 0]), tm(k_dir[:, :, 0]), v_t)
    y = y + fl(_rwkv7_scan_xla(fl(r_t), fl(tm(decay[:, :, 1])), fl(kk_t), fl(tm(a[:, :, 1])), fl(tm(k_dir[:, :, 1])), fl(v_t)))
    y = y.transpose(1, 0, 2, 3)
    mean = jnp.mean(y, axis=-1, keepdims=True)
    var = jnp.mean(jnp.square(y - mean), axis=-1, keepdims=True)
    yn = ((y - mean) * lax.rsqrt(var + RW_GN_EPS)).reshape(B, S, D) * ln_gain + ln_bias
    k_bonus = k32 * (1.0 + (jnp.mean(a, axis=2) - 1.0) * k_a)
    bonus = jnp.sum((r32 * k_bonus * r_k).reshape(B, S, H, N), axis=-1, keepdims=True) * v32.reshape(B, S, H, N)
    out = ((yn + bonus.reshape(B, S, D)) * gate).reshape(T, D).astype(BF16)
    return matmul(out, w_o, residual=h)


def _gla_scan_xla(q, k, v, lf):
    B, H, S, DK = q.shape
    DV = v.shape[-1]
    L = HG_CHUNK
    mask = jnp.tril(jnp.ones((L, L), bool))[:, :, None]

    def step(st, inp):
        qc, kc, vc, fc = inp
        b = jnp.cumsum(fc, axis=2)
        diff = b[:, :, :, None, :] - b[:, :, None, :, :]
        dec = jnp.exp(jnp.where(mask, diff, -jnp.inf))
        att = jnp.einsum('bhtd,bhsd,bhtsd->bhts', qc, kc, dec)
        o = jnp.einsum('bhts,bhse->bhte', att, vc) + jnp.einsum('bhtd,bhde->bhte', qc * jnp.exp(b), st)
        b_end = b[:, :, -1]
        st = jnp.exp(b_end)[..., None] * st + jnp.einsum('bhsd,bhse->bhde', kc * jnp.exp(b_end[:, :, None] - b), vc)
        return st, o

    xs = (_to_chunks(q, L), _to_chunks(k, L), _to_chunks(v, L), _to_chunks(lf, L))
    _, os_ = lax.scan(step, jnp.zeros((B, H, DK, DV), F32), xs)
    return _from_chunks(os_)


def hgrn2_layer(h, hn, w_in, f_bias, lb_logits, layer_idx, head_gain, w_out, B, S):
    T, D = h.shape
    H = D // HG_DK
    DV = D // H
    z = matmul(hn, w_in, out_dtype=BF16).reshape(B, S, 5 * D).astype(F32)
    q, zf_fwd, zf_bwd, i, g = jnp.split(z, 5, axis=-1)
    probs = jax.nn.softmax(lb_logits, axis=0)
    lb = (jnp.cumsum(probs, axis=0) - probs[0])[layer_idx]
    hd = lambda t, d: t.reshape(B, S, H, d).transpose(0, 2, 1, 3)

    def forget(zf, bias):
        zf = zf + bias
        log_f = jnp.logaddexp(jnp.log(lb), jnp.log1p(-lb) + jax.nn.log_sigmoid(zf))
        k = (1.0 - lb) * jax.nn.sigmoid(-zf)
        return hd(log_f, HG_DK), hd(k, HG_DK)

    lf_fwd, k_fwd = forget(zf_fwd, f_bias[0])
    lf_bwd, k_bwd = forget(zf_bwd, f_bias[1])
    qh = hd(jax.nn.silu(q), HG_DK) * (HG_DK ** -0.5)
    vh = hd(i, DV)
    fl = lambda t: jnp.flip(t, axis=2)
    o = _gla_scan_xla(qh, k_fwd, vh, lf_fwd) + fl(_gla_scan_xla(fl(qh), fl(k_bwd), fl(vh), fl(lf_bwd)))
    o = o.transpose(0, 2, 1, 3)
    o = o * lax.rsqrt(jnp.mean(o * o, axis=-1, keepdims=True) + NORM_EPS) * head_gain.reshape(H, DV)
    gated = (o.reshape(B, S, D) * jax.nn.silu(g)).reshape(T, D).astype(BF16)
    return matmul(gated, w_out, residual=h)


def kernel(x, p, norm_mix, norm_ffn, norm_ple_gate, norm_ple_post, norm_final, ml_w_in, ml_gate_bias, ml_conv, ml_head_gain, ml_w_out, rw_mu, rw_w_rkv, rw_w0, rw_w1, rw_w2, rw_a0, rw_a1, rw_a2, rw_g1, rw_g2, rw_k_k, rw_k_a, rw_r_k, rw_ln_gain, rw_ln_bias, rw_w_o, hg_w_in, hg_f_bias, hg_lb, hg_head_gain, hg_w_out, moe_router, moe_w_gate, moe_w_up, moe_w_down, ple_w, ple_gate_down, ple_gate_up):
    B, S, D = x.shape
    depth = p.shape[0]
    T = B * S
    h = x.reshape(T, D)
    hn = rms_norm_bf16(h, norm_mix[0])
    for i in range(depth):
        kind, j = i % 3, i // 3
        if kind == 0:
            h = mlstm_layer(h, hn, ml_w_in[j], ml_gate_bias[j], ml_conv[j], ml_head_gain[j], ml_w_out[j], B, S)
        elif kind == 1:
            h = rwkv_layer(h, hn, rw_mu[j], rw_w_rkv[j], rw_w0[j], rw_w1[j], rw_w2[j], rw_a0[j], rw_a1[j],
                           rw_a2[j], rw_g1[j], rw_g2[j], rw_k_k[j], rw_k_a[j], rw_r_k[j], rw_ln_gain[j],
                           rw_ln_bias[j], rw_w_o[j], B, S)
        else:
            h = hgrn2_layer(h, hn, hg_w_in[j], hg_f_bias[j], hg_lb, i, hg_head_gain[j], hg_w_out[j], B, S)
        hn2, logits = rms_norm_router(h, norm_ffn[i], moe_router[i])
        h = moe_layer(h, hn2, logits, moe_w_gate[i], moe_w_up[i], moe_w_down[i], B, S)
        last = i + 1 == depth
        h, hn = ple_layer(h, p[i].reshape(T, -1), ple_w[i], ple_gate_down[i], ple_gate_up[i], norm_ple_gate[i],
                          norm_ple_post[i], norm_final if last else norm_mix[i + 1], F32 if last else BF16)
    return hn.reshape(B, S, D)
```

```python
import functools

import jax
import jax.numpy as jnp
from jax import lax
from jax.experimental import pallas as pl
from jax.experimental.pallas import tpu as pltpu

F32 = jnp.float32
BF16 = jnp.bfloat16

NORM_EPS = 1e-6
GATE_CAP = 15.0
ML_CHUNK = 128
RW_HEAD = 64
RW_GN_EPS = 64e-5
HG_DK = 128
HG_CHUNK = 64
EC_CAPACITY = 2

LANES = 128
VMEM_LIMIT = 52 * 1024 * 1024


def _cparams(*sem):
    return pltpu.CompilerParams(dimension_semantics=sem, vmem_limit_bytes=VMEM_LIMIT)


def _tile(n, prefs):
    for t in prefs:
        if n % t == 0:
            return t
    return n


def _rms(x, gain):
    return x * lax.rsqrt(jnp.mean(x * x, axis=-1, keepdims=True) + NORM_EPS) * gain


def _norm_kernel(h_ref, g_ref, o_ref):
    o_ref[...] = _rms(h_ref[...], g_ref[...]).astype(o_ref.dtype)


def rms_norm_bf16(h, gain):
    T, D = h.shape
    tm = _tile(T, (256, 128, 64, 32, 16, 8))
    return pl.pallas_call(
        _norm_kernel,
        grid=(T // tm,),
        in_specs=[pl.BlockSpec((tm, D), lambda i: (i, 0)), pl.BlockSpec((1, D), lambda i: (0, 0))],
        out_specs=pl.BlockSpec((tm, D), lambda i: (i, 0)),
        out_shape=jax.ShapeDtypeStruct((T, D), BF16),
        compiler_params=_cparams("parallel"),
        name="rms_norm",
    )(h, gain.reshape(1, D))


def _norm_router_kernel(h_ref, g_ref, r_ref, o_ref, l_ref):
    y = _rms(h_ref[...], g_ref[...])
    o_ref[...] = y.astype(o_ref.dtype)
    l_ref[...] = jnp.dot(y, r_ref[...], preferred_element_type=F32, precision=lax.Precision.HIGHEST)


def rms_norm_router(h, gain, router):
    T, D = h.shape
    E = router.shape[1]
    rp = jnp.pad(router, ((0, 0), (0, LANES - E)))
    tm = _tile(T, (256, 128, 64, 32, 16, 8))
    hn, logits = pl.pallas_call(
        _norm_router_kernel,
        grid=(T // tm,),
        in_specs=[pl.BlockSpec((tm, D), lambda i: (i, 0)), pl.BlockSpec((1, D), lambda i: (0, 0)),
                  pl.BlockSpec((D, LANES), lambda i: (0, 0))],
        out_specs=[pl.BlockSpec((tm, D), lambda i: (i, 0)), pl.BlockSpec((tm, LANES), lambda i: (i, 0))],
        out_shape=[jax.ShapeDtypeStruct((T, D), BF16), jax.ShapeDtypeStruct((T, LANES), F32)],
        compiler_params=_cparams("parallel"),
        name="rms_norm_router",
    )(h, gain.reshape(1, D), rp)
    return hn, logits[:, :E]


_CAST_ROWS = 256


def _cast_weight(w_ref, wb_ref):
    k = w_ref.shape[0]
    rows = _tile(k, (_CAST_ROWS, 128, 64, 32, 16))

    def body(c, _):
        r = pl.multiple_of(c * rows, rows)
        wb_ref[pl.ds(r, rows), :] = w_ref[pl.ds(r, rows), :].astype(BF16)
        return 0

    lax.fori_loop(0, k // rows, body, 0)


def _mm_kernel(x_ref, w_ref, *rest, has_res, has_bias, act):
    rest = list(rest)
    b_ref = rest.pop(0) if has_bias else None
    r_ref = rest.pop(0) if has_res else None
    o_ref, wb_ref = rest

    @pl.when(pl.program_id(1) == 0)
    def _():
        _cast_weight(w_ref, wb_ref)

    acc = jnp.dot(x_ref[...], wb_ref[...], preferred_element_type=F32)
    if has_bias:
        acc = acc + b_ref[...]
    if act is not None:
        acc = act(acc)
    if has_res:
        acc = acc + r_ref[...]
    o_ref[...] = acc.astype(o_ref.dtype)


def matmul(x, w, *, n=None, col_off=0, bias=None, residual=None, act=None, out_dtype=F32, tm=None, tn=None):
    M, K = x.shape
    if n is None:
        n = w.shape[1] - col_off
    tm = tm or _tile(M, (1024, 512, 256, 128, 64, 32, 16))
    tn = tn or _tile(n, (512, 256, 128))
    assert n % tn == 0 and col_off % tn == 0 and M % tm == 0
    cb = col_off // tn
    in_specs = [pl.BlockSpec((tm, K), lambda j, i: (i, 0)), pl.BlockSpec((K, tn), lambda j, i: (0, j + cb))]
    args = [x, w]
    if bias is not None:
        in_specs.append(pl.BlockSpec((1, tn), lambda j, i: (0, j)))
        args.append(bias.reshape(1, n))
    if residual is not None:
        in_specs.append(pl.BlockSpec((tm, tn), lambda j, i: (i, j)))
        args.append(residual)
    return pl.pallas_call(
        functools.partial(_mm_kernel, has_res=residual is not None, has_bias=bias is not None, act=act),
        grid=(n // tn, M // tm),
        in_specs=in_specs,
        out_specs=pl.BlockSpec((tm, tn), lambda j, i: (i, j)),
        out_shape=jax.ShapeDtypeStruct((M, n), out_dtype),
        scratch_shapes=[pltpu.VMEM((K, tn), BF16)],
        compiler_params=_cparams("arbitrary", "arbitrary"),
        name="matmul",
    )(*args)


def _lora_kernel(x_ref, a_ref, b_ref, bias_ref, o_ref, ab_ref, bb_ref, *, act1, act2):
    @pl.when(pl.program_id(0) == 0)
    def _():
        _cast_weight(a_ref, ab_ref)
        _cast_weight(b_ref, bb_ref)

    t = jnp.dot(x_ref[...], ab_ref[...], preferred_element_type=F32)
    if act1 is not None:
        t = act1(t)
    y = jnp.dot(t.astype(BF16), bb_ref[...], preferred_element_type=F32) + bias_ref[...]
    if act2 is not None:
        y = act2(y)
    o_ref[...] = y.astype(o_ref.dtype)


def lora(x, a, b, bias=None, act1=None, act2=None, out_dtype=F32, tm=256):
    M, K = x.shape
    R, N = b.shape
    rp = -R % LANES
    if rp:
        a = jnp.pad(a, ((0, 0), (0, rp)))
        b = jnp.pad(b, ((0, rp), (0, 0)))
        R += rp
    if bias is None:
        bias = jnp.zeros((N,), F32)
    tm = _tile(M, (tm, 128, 64, 32, 16))
    return pl.pallas_call(
        functools.partial(_lora_kernel, act1=act1, act2=act2),
        grid=(M // tm,),
        in_specs=[pl.BlockSpec((tm, K), lambda i: (i, 0)), pl.BlockSpec((K, R), lambda i: (0, 0)),
                  pl.BlockSpec((R, N), lambda i: (0, 0)), pl.BlockSpec((1, N), lambda i: (0, 0))],
        out_specs=pl.BlockSpec((tm, N), lambda i: (i, 0)),
        out_shape=jax.ShapeDtypeStruct((M, N), out_dtype),
        scratch_shapes=[pltpu.VMEM((K, R), BF16), pltpu.VMEM((R, N), BF16)],
        compiler_params=_cparams("arbitrary"),
        name="lora",
    )(x, a, b, bias.reshape(1, N))


def _ple_kernel(h_ref, p_ref, wp_ref, gd_ref, gu_ref, gg_ref, pg_ref, ng_ref, ho_ref, no_ref,
                wpb_ref, gdb_ref, gub_ref):
    @pl.when(pl.program_id(0) == 0)
    def _():
        _cast_weight(wp_ref, wpb_ref)
        _cast_weight(gd_ref, gdb_ref)
        _cast_weight(gu_ref, gub_ref)

    h = h_ref[...]
    e = jnp.dot(p_ref[...].astype(BF16), wpb_ref[...], preferred_element_type=F32)
    e = _rms(e, pg_ref[...])
    t = jnp.dot(_rms(h, gg_ref[...]).astype(BF16), gdb_ref[...], preferred_element_type=F32)
    g = jnp.dot(t.astype(BF16), gub_ref[...], preferred_element_type=F32)
    hn = h + e * jax.nn.sigmoid(g)
    ho_ref[...] = hn
    no_ref[...] = _rms(hn, ng_ref[...]).astype(no_ref.dtype)


def ple_layer(h, p, w_ple, g_down, g_up, gate_gain, post_gain, next_gain, next_dtype):
    T, D = h.shape
    P = p.shape[1]
    tm = _tile(T, (256, 128, 64, 32, 16, 8))
    row = lambda i: (i, 0)
    fix = lambda i: (0, 0)
    return pl.pallas_call(
        _ple_kernel,
        grid=(T // tm,),
        in_specs=[pl.BlockSpec((tm, D), row), pl.BlockSpec((tm, P), row), pl.BlockSpec((P, D), fix),
                  pl.BlockSpec((D, P), fix), pl.BlockSpec((P, D), fix), pl.BlockSpec((1, D), fix),
                  pl.BlockSpec((1, D), fix), pl.BlockSpec((1, D), fix)],
        out_specs=[pl.BlockSpec((tm, D), row), pl.BlockSpec((tm, D), row)],
        out_shape=[jax.ShapeDtypeStruct((T, D), F32), jax.ShapeDtypeStruct((T, D), next_dtype)],
        scratch_shapes=[pltpu.VMEM((P, D), BF16), pltpu.VMEM((D, P), BF16), pltpu.VMEM((P, D), BF16)],
        compiler_params=_cparams("arbitrary"),
        name="ple",
    )(h, p, w_ple, g_down, g_up, gate_gain.reshape(1, D), post_gain.reshape(1, D), next_gain.reshape(1, D))


def _moe_up_kernel(x_ref, wg_ref, wu_ref, o_ref, wgb_ref, wub_ref):
    _cast_weight(wg_ref.at[0], wgb_ref)
    _cast_weight(wu_ref.at[0], wub_ref)
    x = x_ref[0]
    g = jnp.dot(x, wgb_ref[...], preferred_element_type=F32)
    u = jnp.dot(x, wub_ref[...], preferred_element_type=F32)
    o_ref[0] = (g * jax.nn.sigmoid(g) * u).astype(o_ref.dtype)


def _moe_down_kernel(h_ref, wd_ref, s_ref, o_ref, wdb_ref):
    _cast_weight(wd_ref.at[0], wdb_ref)
    o_ref[0] = jnp.dot(h_ref[0], wdb_ref[...], preferred_element_type=F32) * s_ref[0]


def moe_ffn(xg, w_gate, w_up, w_down, gsel):
    E, R, D = xg.shape
    FF = w_gate.shape[2]
    tf = _tile(FF, (256, 128))
    hid = pl.pallas_call(
        _moe_up_kernel,
        grid=(E, FF // tf),
        in_specs=[pl.BlockSpec((1, R, D), lambda e, f: (e, 0, 0)),
                  pl.BlockSpec((1, D, tf), lambda e, f: (e, 0, f)),
                  pl.BlockSpec((1, D, tf), lambda e, f: (e, 0, f))],
        out_specs=pl.BlockSpec((1, R, tf), lambda e, f: (e, 0, f)),
        out_shape=jax.ShapeDtypeStruct((E, R, FF), BF16),
        scratch_shapes=[pltpu.VMEM((D, tf), BF16), pltpu.VMEM((D, tf), BF16)],
        compiler_params=_cparams("parallel", "parallel"),
        name="moe_up",
    )(xg, w_gate, w_up)
    tn = _tile(D, (1024, 512, 256, 128))
    return pl.pallas_call(
        _moe_down_kernel,
        grid=(E, D // tn),
        in_specs=[pl.BlockSpec((1, R, FF), lambda e, j: (e, 0, 0)),
                  pl.BlockSpec((1, FF, tn), lambda e, j: (e, 0, j)),
                  pl.BlockSpec((1, R, 1), lambda e, j: (e, 0, 0))],
        out_specs=pl.BlockSpec((1, R, tn), lambda e, j: (e, 0, j)),
        out_shape=jax.ShapeDtypeStruct((E, R, D), F32),
        scratch_shapes=[pltpu.VMEM((FF, tn), BF16)],
        compiler_params=_cparams("parallel", "parallel"),
        name="moe_down",
    )(hid, w_down, gsel.reshape(E, R, 1))


def moe_layer(h, hn, logits, w_gate, w_up, w_down, B, S):
    T, D = h.shape
    E = logits.shape[1]
    cap = EC_CAPACITY * S // E
    aff = jax.nn.softmax(logits, axis=-1).reshape(B, S, E)
    gsel, idx = lax.top_k(jnp.swapaxes(aff, 1, 2), cap)
    tok = idx + (jnp.arange(B, dtype=idx.dtype) * S)[:, None, None]
    tok = jnp.swapaxes(tok, 0, 1).reshape(E, B * cap)
    gsel = jnp.swapaxes(gsel, 0, 1).reshape(E, B * cap)
    xg = jnp.take(hn, tok, axis=0)
    out = moe_ffn(xg, w_gate, w_up, w_down, gsel)
    return h.at[tok.reshape(-1)].add(out.reshape(-1, D))


def _softplus(x):
    return jnp.maximum(x, 0.0) + jnp.log(1.0 + jnp.exp(-jnp.abs(x)))


def _dwconv3_kernel(x_ref, w_ref, o_ref):
    x = x_ref[0].astype(F32)
    s = x.shape[0]
    t = lax.broadcasted_iota(jnp.int32, x.shape, 0)
    prev = jnp.where(t == 0, 0.0, pltpu.roll(x, 1, 0))
    nxt = jnp.where(t == s - 1, 0.0, pltpu.roll(x, s - 1, 0))
    o_ref[0] = (w_ref[0:1, :] * prev + w_ref[1:2, :] * x + w_ref[2:3, :] * nxt).astype(o_ref.dtype)


def dwconv3(x, w):
    B, S, _ = x.shape
    assert w.shape[0] == 3
    C = w.shape[1]
    blk = pl.BlockSpec((1, S, LANES), lambda b, j: (b, 0, j))
    return pl.pallas_call(
        _dwconv3_kernel,
        grid=(B, C // LANES),
        in_specs=[blk, pl.BlockSpec((3, LANES), lambda b, j: (0, j))],
        out_specs=blk,
        out_shape=jax.ShapeDtypeStruct((B, S, C), BF16),
        compiler_params=_cparams("parallel", "parallel"),
        name="dwconv3",
    )(x, w)


ML_EXT = LANES


def _mlstm_scan_kernel(q_ref, k_ref, v_ref, g_ref, gb_ref, o_ref, c_ref, m_ref, *, heads):
    hd = pl.program_id(1)
    d = pl.program_id(2)
    L = q_ref.shape[0]
    dv = v_ref.shape[1]

    @pl.when(pl.program_id(3) == 0)
    def _():
        c_ref[...] = jnp.zeros_like(c_ref)
        m_ref[...] = jnp.full_like(m_ref, -1e30)

    sgn = jnp.where(d == 0, 1, -1)
    rl = lax.broadcasted_iota(jnp.int32, (L, L), 0)
    cl = lax.broadcasted_iota(jnp.int32, (L, L), 1)
    incl = (cl - rl) * sgn <= 0
    incl_t = (rl - cl) * sgn <= 0
    eye = rl == cl

    g = g_ref[...] + gb_ref[...]
    g = GATE_CAP * jnp.tanh(g / GATE_CAP)
    lane = lax.broadcasted_iota(jnp.int32, g.shape, 1)
    i_idx = d * 2 * heads + hd
    i_col = jnp.sum(jnp.where(lane == i_idx, g, 0.0), axis=1, keepdims=True)
    f_col = -_softplus(-jnp.sum(jnp.where(lane == i_idx + heads, g, 0.0), axis=1, keepdims=True))
    i_row = jnp.sum(jnp.where(eye, i_col, 0.0), axis=0, keepdims=True)
    f_row = jnp.sum(jnp.where(eye, f_col, 0.0), axis=0, keepdims=True)
    b_col = jnp.sum(jnp.where(incl, f_row, 0.0), axis=1, keepdims=True)
    b_row = jnp.sum(jnp.where(incl_t, f_col, 0.0), axis=0, keepdims=True)
    gtot = jnp.sum(f_col, axis=0, keepdims=True)
    m = m_ref[0:1, 0:1]

    dm = b_col - b_row + i_row
    inter_log = b_col + m
    m_t = jnp.maximum(inter_log, jnp.max(jnp.where(incl, dm, -jnp.inf), axis=1, keepdims=True))
    pmat = jnp.where(incl, jnp.exp(jnp.where(incl, dm - m_t, 0.0)), 0.0)
    w_inter = jnp.exp(inter_log - m_t)

    q = q_ref[...]
    k = k_ref[...]
    s = lax.dot_general(q, k, (((1,), (1,)), ((), ())), preferred_element_type=F32) * pmat
    ones_col = jnp.where(lax.broadcasted_iota(jnp.int32, (L, ML_EXT), 1) == 0, 1.0, 0.0).astype(BF16)
    vext = jnp.concatenate([v_ref[...], ones_col], axis=1)
    c = c_ref[...]
    tot = (jnp.dot(s.astype(BF16), vext, preferred_element_type=F32)
           + w_inter * jnp.dot(q, c.astype(BF16), preferred_element_type=F32))
    den = tot[:, dv:dv + 1]
    o_ref[0] = tot[:, :dv] / jnp.maximum(jnp.abs(den), jnp.exp(-m_t))

    src = gtot - b_col + i_col
    m_new = jnp.maximum(gtot + m, jnp.max(src, axis=0, keepdims=True))
    kw = (k.astype(F32) * jnp.exp(src - m_new)).astype(BF16)
    c_ref[...] = jnp.exp(gtot + m - m_new) * c + lax.dot_general(
        kw, vext, (((0,), (0,)), ((), ())), preferred_element_type=F32)
    m_ref[...] = jnp.broadcast_to(m_new, m_ref.shape)


def mlstm_scan(qk, z, gates, gate_bias, heads, B, S):
    T, QK2 = qk.shape
    L = ML_CHUNK
    dk = QK2 // 2 // heads
    V = (z.shape[1] - QK2) // 2
    dv = V // heads
    nc = S // L

    def trow(b, d, c):
        return b * nc + jnp.where(d == 0, c, nc - 1 - c)

    return pl.pallas_call(
        functools.partial(_mlstm_scan_kernel, heads=heads),
        grid=(B, heads, 2, nc),
        in_specs=[pl.BlockSpec((L, dk), lambda b, h, d, c: (trow(b, d, c), h)),
                  pl.BlockSpec((L, dk), lambda b, h, d, c: (trow(b, d, c), heads + h)),
                  pl.BlockSpec((L, dv), lambda b, h, d, c: (trow(b, d, c), QK2 // dv + h)),
                  pl.BlockSpec((L, LANES), lambda b, h, d, c: (trow(b, d, c), 0)),
                  pl.BlockSpec((1, LANES), lambda b, h, d, c: (0, 0))],
        out_specs=pl.BlockSpec((1, L, dv), lambda b, h, d, c: (d, trow(b, d, c), h)),
        out_shape=jax.ShapeDtypeStruct((2, T, V), F32),
        scratch_shapes=[pltpu.VMEM((dk, dv + ML_EXT), F32), pltpu.VMEM((1, LANES), F32)],
        compiler_params=_cparams("parallel", "parallel", "arbitrary", "arbitrary"),
        name="mlstm_scan",
    )(qk, qk, z, gates, gate_bias.reshape(1, LANES))


def mlstm_layer(h, hn, w_in, gate_bias, conv_w, head_gain, w_out, B, S):
    T, D = h.shape
    H = gate_bias.shape[0] // 4
    QK = conv_w.shape[1] // 2
    V = w_out.shape[0]
    DK, DV = QK // H, V // H
    nmain = 2 * QK + 2 * V
    z = matmul(hn, w_in, n=nmain, out_dtype=BF16)
    pad = LANES - 4 * H
    gates = matmul(hn, jnp.pad(w_in[:, nmain:], ((0, 0), (0, pad))))
    taps = conv_w * jnp.concatenate([jnp.full((QK,), DK ** -0.5, F32), jnp.ones((QK,), F32)])
    qk = dwconv3(z.reshape(B, S, nmain), taps).reshape(T, 2 * QK)
    hs = mlstm_scan(qk, z, gates, jnp.pad(gate_bias, (0, pad)), H, B, S)
    gated = head_out(hs, z, 2 * QK + V, head_gain, DV, jax.nn.sigmoid)
    return matmul(gated, w_out, residual=h)


RW_CHUNK = 64
RW_SUB = 16
RW_BLOCK_T = 128
RW_BLOCK_H = 16
assert RW_CHUNK == RW_HEAD


def _bdot(a, b):
    return jnp.dot(a, b, preferred_element_type=F32)


def _split(x):
    hi = x.astype(BF16)
    return hi, (x - hi.astype(F32)).astype(BF16)


def _dot_exact_lhs(m, x):
    xh, xl = _split(x)
    return _bdot(m, xh) + _bdot(m, xl)


def _dot_exact_rhs(x, m):
    xh, xl = _split(x)
    return _bdot(xh, m) + _bdot(xl, m)


def _tri_inverse(a, blk, eye, nblk, mm, bd):
    d = jnp.where(blk, a, 0.0)
    off = a - d
    d2 = mm(d, bd(d))
    d4 = mm(d2, bd(d2))
    d8 = mm(d4, bd(d4))
    p = eye + d
    p = p + mm(p, bd(d2))
    p = p + mm(p, bd(d4))
    p = p + mm(p, bd(d8))
    n = mm(p, bd(off))
    q = eye + n
    pw, reach = n, 1
    while 2 * reach < nblk:
        pw = mm(pw, bd(pw))
        q = q + mm(q, bd(pw))
        reach *= 2
    return mm(q, bd(p))


def _rwkv_scan_kernel(r_ref, k_ref, v_ref, wl_ref, a_ref, kk_ref, ka_ref, y_ref, s_ref):
    d = pl.program_id(2)
    L, N = RW_CHUNK, RW_HEAD
    W = 2 * N
    tb, hw = r_ref.shape
    nch, P = tb // L, hw // W

    @pl.when(pl.program_id(3) == 0)
    def _():
        s_ref[...] = jnp.zeros_like(s_ref)

    fwd = d == 0
    sgn = jnp.where(fwd, 1, -1)
    row = lax.broadcasted_iota(jnp.int32, (L, W), 0)
    lane = lax.broadcasted_iota(jnp.int32, (L, W), 1)
    cs = lane % L
    ahead = (cs - row) * sgn
    strict = ahead < 0
    incl = ahead <= 0
    eye = jnp.where(cs == row, 1.0, 0.0)
    blk = (row // RW_SUB) == (cs // RW_SUB)
    low = lane < N
    rl = lax.broadcasted_iota(jnp.int32, (L, L), 0)
    cl = lax.broadcasted_iota(jnp.int32, (L, L), 1)
    incl_bf = jnp.where((cl - rl) * sgn <= 0, 1.0, 0.0).astype(BF16)
    br = lax.broadcasted_iota(jnp.int32, (W, W), 0) // N
    bc = lax.broadcasted_iota(jnp.int32, (W, W), 1) // N
    same = br == bc
    seg = jnp.where(same, 1.0, 0.0).astype(BF16)

    def stack(x):
        return jnp.stack([x[:, p * W:(p + 1) * W] for p in range(P)], axis=0)

    def bd(x):
        return jnp.where(same, jnp.concatenate([x, x], axis=1), 0.0).astype(BF16)

    def mm(a, b):
        return lax.dot_general(a.astype(BF16), b, (((2,), (1,)), ((0,), (0,))), preferred_element_type=F32)

    def mm_nt(a, b):
        return lax.dot_general(a.astype(BF16), b, (((2,), (2,)), ((0,), (0,))), preferred_element_type=F32)

    for ci in range(nch):
        t0 = pl.multiple_of(jnp.where(fwd, ci * L, (nch - 1 - ci) * L), L)
        rows = pl.ds(t0, L)
        r = r_ref[rows, :]
        k = k_ref[rows, :]
        v = v_ref[rows, :]
        rate = a_ref[rows, :]
        logw = -jnp.exp(-_softplus(-wl_ref[rows, :]) - 0.5)
        kk = k * kk_ref[...]
        ssq = kk * kk
        nrm2 = jnp.concatenate([_dot_exact_rhs(ssq[:, p * W:(p + 1) * W], seg) for p in range(P)], axis=1)
        kk = kk / jnp.maximum(jnp.sqrt(nrm2), 1e-12)
        kd = k * (1.0 + (rate - 1.0) * ka_ref[...])
        bv = kk * rate
        c = _dot_exact_lhs(incl_bf, logw)
        cend = jnp.sum(logw, axis=0, keepdims=True)
        einv = jnp.exp(-c)
        eend = jnp.exp(cend - c)
        at = stack(-kk * jnp.exp(c - logw))
        rt = stack(r * jnp.exp(c))
        vp = stack(v)
        ar = jnp.concatenate([at, rt], axis=1)
        g_b = mm_nt(ar, bd(stack(bv * einv)))
        g_k = mm_nt(ar, bd(stack(kd * einv)))
        aab = jnp.where(strict, g_b[:, :L], 0.0)
        rb = jnp.where(incl, g_b[:, L:], 0.0)
        aak = jnp.where(strict, g_k[:, :L], 0.0)
        rk = jnp.where(incl, g_k[:, L:], 0.0)
        tinv = _tri_inverse(aab, blk, eye, L // RW_SUB, mm, bd)
        s0 = s_ref[...]
        bd_s, bd_v = bd(s0), bd(vp)
        u = mm(tinv, bd(mm_nt(at, bd_s) + mm(aak, bd_v)))
        y = mm_nt(rt, bd_s) + mm(rb, bd(u)) + mm(rk, bd_v)
        for p in range(P):
            y_ref[0, rows, p * W:(p + 1) * W] = y[p]
        uv = jnp.concatenate([u, vp], axis=1).astype(BF16)
        bkh = jnp.concatenate([stack(bv * eend), stack(kd * eend)], axis=1).astype(BF16)
        g = lax.dot_general(uv, bkh, (((1,), (1,)), ((0,), (0,))), preferred_element_type=F32)
        s_ref[...] = s0 * stack(jnp.exp(cend)) + jnp.where(low, g[:, :N], g[:, N:])


def rwkv_scan(r, k, v, wl2, a2, k_k, k_a, B, S):
    T, D = r.shape
    tb = _tile(S, (RW_BLOCK_T, RW_CHUNK))
    hw = _tile(D, (RW_BLOCK_H * RW_HEAD, 4 * RW_HEAD, 2 * RW_HEAD))
    nt, nh = S // tb, D // hw

    def trow(b, d, c):
        return b * nt + jnp.where(d == 0, c, nt - 1 - c)

    shared = pl.BlockSpec((tb, hw), lambda b, g, d, c: (trow(b, d, c), g))
    perdir = pl.BlockSpec((tb, hw), lambda b, g, d, c: (trow(b, d, c), d * nh + g))
    par = pl.BlockSpec((1, hw), lambda b, g, d, c: (0, g))
    return pl.pallas_call(
        _rwkv_scan_kernel,
        grid=(B, nh, 2, nt),
        in_specs=[shared, shared, shared, perdir, perdir, par, par],
        out_specs=pl.BlockSpec((1, tb, hw), lambda b, g, d, c: (d, trow(b, d, c), g)),
        out_shape=jax.ShapeDtypeStruct((2, T, D), F32),
        scratch_shapes=[pltpu.VMEM((hw // (2 * RW_HEAD), RW_HEAD, 2 * RW_HEAD), F32)],
        compiler_params=_cparams("parallel", "parallel", "arbitrary", "arbitrary"),
        name="rwkv_scan",
    )(r, k, v, wl2, a2, k_k.reshape(1, D), k_a.reshape(1, D))


def _seg_sum(x, seg_bf):
    return _dot_exact_rhs(x, seg_bf)


def _rwkv_out_kernel(y_ref, r_ref, k_ref, v_ref, af_ref, ab_ref, g_ref, rk_ref, ka_ref, lg_ref, lb_ref, o_ref):
    w = y_ref.shape[2]
    li = lax.broadcasted_iota(jnp.int32, (w, w), 0) // RW_HEAD
    lj = lax.broadcasted_iota(jnp.int32, (w, w), 1) // RW_HEAD
    seg = jnp.where(li == lj, 1.0, 0.0).astype(BF16)
    inv_n = 1.0 / RW_HEAD
    y = y_ref[0] + y_ref[1]
    mean = _seg_sum(y, seg) * inv_n
    yc = y - mean
    var = _seg_sum(yc * yc, seg) * inv_n
    yn = yc * lax.rsqrt(var + RW_GN_EPS) * lg_ref[...] + lb_ref[...]
    a_mean = 0.5 * (af_ref[...] + ab_ref[...])
    k_bonus = k_ref[...] * (1.0 + (a_mean - 1.0) * ka_ref[...])
    bonus = _seg_sum(r_ref[...] * k_bonus * rk_ref[...], seg) * v_ref[...]
    o_ref[...] = ((yn + bonus) * g_ref[...]).astype(o_ref.dtype)


def rwkv_out(y, r, k, v, a2, gate, r_k, k_a, ln_gain, ln_bias):
    _, T, D = y.shape
    tm = _tile(T, (256, 128, 64, 32, 16, 8))
    w = _tile(D, (256, 128))
    nw = D // w
    tile = pl.BlockSpec((tm, w), lambda i, j: (i, j))
    par = pl.BlockSpec((1, w), lambda i, j: (0, j))
    return pl.pallas_call(
        _rwkv_out_kernel,
        grid=(T // tm, nw),
        in_specs=[pl.BlockSpec((2, tm, w), lambda i, j: (0, i, j)), tile, tile, tile,
                  tile, pl.BlockSpec((tm, w), lambda i, j: (i, nw + j)), tile, par, par, par, par],
        out_specs=tile,
        out_shape=jax.ShapeDtypeStruct((T, D), BF16),
        compiler_params=_cparams("parallel", "parallel"),
        name="rwkv_out",
    )(y, r, k, v, a2, a2, gate, r_k.reshape(1, D), k_a.reshape(1, D), ln_gain.reshape(1, D), ln_bias.reshape(1, D))


def _shift_mix_kernel(x_ref, mu_ref, *o_refs):
    x = x_ref[0].astype(F32)
    s = x.shape[0]
    t = lax.broadcasted_iota(jnp.int32, x.shape, 0)
    prev = jnp.where(t == 0, 0.0, pltpu.roll(x, 1, 0))
    nxt = jnp.where(t == s - 1, 0.0, pltpu.roll(x, s - 1, 0))
    dx = 0.5 * (prev + nxt) - x
    for j, o_ref in enumerate(o_refs):
        o_ref[0] = (x + dx * mu_ref[j:j + 1, :]).astype(o_ref.dtype)


def shift_mix(xn, mu):
    B, S, D = xn.shape
    J = mu.shape[0]
    w = LANES
    blk = pl.BlockSpec((1, S, w), lambda b, j: (b, 0, j))
    return pl.pallas_call(
        _shift_mix_kernel,
        grid=(B, D // w),
        in_specs=[blk, pl.BlockSpec((J, w), lambda b, j: (0, j))],
        out_specs=[blk] * J,
        out_shape=[jax.ShapeDtypeStruct((B, S, D), BF16)] * J,
        compiler_params=_cparams("parallel", "parallel"),
        name="shift_mix",
    )(xn, mu)


def rwkv_layer(h, hn, mu, w_rkv, w0, w1, w2, a0, a1, a2, g1, g2, k_k, k_a, r_k, ln_gain, ln_bias, w_o, B, S):
    T, D = h.shape
    N = RW_HEAD
    H = D // N
    xm = [t.reshape(T, D) for t in shift_mix(hn.reshape(B, S, D), mu)]
    r = matmul(xm[0], w_rkv[0])
    k = matmul(xm[1], w_rkv[1])
    v = matmul(xm[2], w_rkv[2])

    def both_dirs(lo, hi):
        z = jnp.zeros_like(hi[0])
        return (jnp.concatenate([lo[0], lo[1]], axis=1),
                jnp.concatenate([jnp.concatenate([hi[0], z], axis=1), jnp.concatenate([z, hi[1]], axis=1)], axis=0))

    wa, wb = both_dirs(w1, w2)
    wl2 = lora(xm[3], wa, wb, bias=w0.reshape(-1), act1=jnp.tanh, tm=128)
    aa, ab = both_dirs(a1, a2)
    rate2 = lora(xm[4], aa, ab, bias=a0.reshape(-1), act2=jax.nn.sigmoid, tm=128)
    gate = lora(xm[5], g1, g2, act1=jax.nn.sigmoid)
    y = rwkv_scan(r, k, v, wl2, rate2, k_k, k_a, B, S)
    out = rwkv_out(y, r, k, v, rate2, gate, r_k, k_a, ln_gain, ln_bias)
    return matmul(out, w_o, residual=h)


HG_SUB = 16
HG_BLOCK_T = 128
HG_BLOCK_H = 8


def _hgrn_scan_kernel(q_ref, zf_ref, v_ref, fb_ref, lb_ref, o_ref, s_ref):
    d = pl.program_id(2)
    L, N, SUB = HG_CHUNK, HG_DK, HG_SUB
    tb, hw = q_ref.shape
    nch, P = tb // L, hw // N

    @pl.when(pl.program_id(3) == 0)
    def _():
        s_ref[...] = jnp.zeros_like(s_ref)

    fwd = d == 0
    sgn = jnp.where(fwd, 1, -1)
    rl = lax.broadcasted_iota(jnp.int32, (L, L), 0)
    cl = lax.broadcasted_iota(jnp.int32, (L, L), 1)
    incl_bf = jnp.where((cl - rl) * sgn <= 0, 1.0, 0.0).astype(BF16)
    blk_r, blk_c = rl // SUB, cl // SUB
    first_r = blk_r * SUB + jnp.where(fwd, 0, SUB - 1)
    sel_first = jnp.where(cl == first_r, 1.0, 0.0).astype(BF16)
    blk_before = (blk_c - blk_r) * sgn < 0
    t_in = lax.broadcasted_iota(jnp.int32, (L, hw), 0) % SUB
    ones_nl = jnp.ones((N, L), BF16)

    def stack(x):
        return jnp.stack([x[:, p * N:(p + 1) * N] for p in range(P)], axis=0)

    def within_block(x, j):
        return jnp.concatenate(
            [jnp.broadcast_to(x[i * SUB + j:i * SUB + j + 1, :], (SUB, x.shape[1])) for i in range(L // SUB)], axis=0)

    lb = lb_ref[...]
    log_lb = jnp.log(lb)
    log_1m = jnp.log(1.0 - lb)

    for ci in range(nch):
        t0 = pl.multiple_of(jnp.where(fwd, ci * L, (nch - 1 - ci) * L), L)
        rows = pl.ds(t0, L)
        qv = q_ref[rows, :].astype(F32)
        q = qv * jax.nn.sigmoid(qv) * (N ** -0.5)
        zf = zf_ref[rows, :].astype(F32) + fb_ref[0]
        hi = log_1m - _softplus(-zf)
        lf = jnp.maximum(log_lb, hi) + jnp.log(1.0 + jnp.exp(-jnp.abs(log_lb - hi)))
        kx = (1.0 - lb) * jax.nn.sigmoid(-zf)
        vb = stack(v_ref[rows, :])
        b = _dot_exact_lhs(incl_bf, lf)
        ref = _dot_exact_lhs(sel_first, b - lf)
        bend = jnp.sum(lf, axis=0, keepdims=True)
        qh = stack(q * jnp.exp(b - ref)).astype(BF16)
        rows_att = []
        for i in range(L // SUB):
            ki = kx * jnp.exp(jnp.minimum(ref[i * SUB:i * SUB + 1, :] - b, 0.0))
            rows_att.append(lax.dot_general(qh[:, i * SUB:(i + 1) * SUB], stack(ki).astype(BF16),
                                            (((2,), (2,)), ((0,), (0,))), preferred_element_type=F32))
        att = jnp.where(blk_before, jnp.concatenate(rows_att, axis=1), 0.0)
        for j in range(SUB):
            valid = (j - t_in) * sgn <= 0
            e = jnp.where(valid, jnp.exp(jnp.where(valid, b - within_block(b, j), 0.0)), 0.0)
            pj = stack(q * within_block(kx, j) * e).astype(BF16)
            tot = jnp.dot(pj.reshape(P * L, N), ones_nl, preferred_element_type=F32).reshape(P, L, L)
            att = jnp.where(cl == blk_r * SUB + j, tot, att)
        s0 = s_ref[...]
        o = lax.dot_general(att.astype(BF16), vb, (((2,), (1,)), ((0,), (0,))), preferred_element_type=F32)
        o = o + lax.dot_general(stack(q * jnp.exp(b)).astype(BF16), s0.astype(BF16),
                                (((2,), (2,)), ((0,), (0,))), preferred_element_type=F32)
        for p in range(P):
            o_ref[0, rows, p * N:(p + 1) * N] = o[p]
        kend = stack(kx * jnp.exp(bend - b)).astype(BF16)
        s_ref[...] = s0 * stack(jnp.exp(bend)) + lax.dot_general(
            vb, kend, (((1,), (1,)), ((0,), (0,))), preferred_element_type=F32)


def hgrn_scan(z, f_bias, lb, B, S):
    T, D5 = z.shape
    D = D5 // 5
    tb = _tile(S, (HG_BLOCK_T, HG_CHUNK))
    hw = _tile(D, (HG_BLOCK_H * HG_DK, 4 * HG_DK, 2 * HG_DK, HG_DK))
    nt, nh = S // tb, D // hw

    def trow(b, d, c):
        return b * nt + jnp.where(d == 0, c, nt - 1 - c)

    return pl.pallas_call(
        _hgrn_scan_kernel,
        grid=(B, nh, 2, nt),
        in_specs=[pl.BlockSpec((tb, hw), lambda b, g, d, c: (trow(b, d, c), g)),
                  pl.BlockSpec((tb, hw), lambda b, g, d, c: (trow(b, d, c), (1 + d) * nh + g)),
                  pl.BlockSpec((tb, hw), lambda b, g, d, c: (trow(b, d, c), 3 * nh + g)),
                  pl.BlockSpec((1, 1, hw), lambda b, g, d, c: (d, 0, g)),
                  pl.BlockSpec((1, hw), lambda b, g, d, c: (0, g))],
        out_specs=pl.BlockSpec((1, tb, hw), lambda b, g, d, c: (d, trow(b, d, c), g)),
        out_shape=jax.ShapeDtypeStruct((2, T, D), F32),
        scratch_shapes=[pltpu.VMEM((hw // HG_DK, HG_DK, HG_DK), F32)],
        compiler_params=_cparams("parallel", "parallel", "arbitrary", "arbitrary"),
        name="hgrn_scan",
    )(z, z, z, f_bias.reshape(2, 1, D), lb.reshape(1, D))


def _head_out_kernel(y_ref, g_ref, gain_ref, o_ref, *, head, act):
    y = y_ref[0] + y_ref[1]
    g = g_ref[...].astype(F32)
    gain = gain_ref[...]
    for p in range(y.shape[1] // head):
        cols = slice(p * head, (p + 1) * head)
        yp = y[:, cols]
        yn = yp * lax.rsqrt(jnp.mean(yp * yp, axis=-1, keepdims=True) + NORM_EPS) * gain[:, cols]
        o_ref[:, cols] = (yn * act(g[:, cols])).astype(o_ref.dtype)


def head_out(y, z, gate_col, gain, head, act):
    _, T, D = y.shape
    tm = _tile(T, (256, 128, 64, 32, 16, 8))
    w = _tile(D, (512, 256, 128))
    w = max(w, head)
    gb = gate_col // w
    return pl.pallas_call(
        functools.partial(_head_out_kernel, head=head, act=act),
        grid=(T // tm, D // w),
        in_specs=[pl.BlockSpec((2, tm, w), lambda i, j: (0, i, j)),
                  pl.BlockSpec((tm, w), lambda i, j: (i, gb + j)),
                  pl.BlockSpec((1, w), lambda i, j: (0, j))],
        out_specs=pl.BlockSpec((tm, w), lambda i, j: (i, j)),
        out_shape=jax.ShapeDtypeStruct((T, D), BF16),
        compiler_params=_cparams("parallel", "parallel"),
        name="head_out",
    )(y, z, gain.reshape(1, D))


def hgrn2_layer(h, hn, w_in, f_bias, lb_logits, layer_idx, head_gain, w_out, B, S):
    T, D = h.shape
    z = matmul(hn, w_in, out_dtype=BF16)
    probs = jax.nn.softmax(lb_logits, axis=0)
    lb = (jnp.cumsum(probs, axis=0) - probs[0])[layer_idx]
    o = hgrn_scan(z, f_bias, lb, B, S)
    gated = head_out(o, z, 4 * D, head_gain, HG_DK, jax.nn.silu)
    return matmul(gated, w_out, residual=h)


def kernel(x, p, norm_mix, norm_ffn, norm_ple_gate, norm_ple_post, norm_final, ml_w_in, ml_gate_bias, ml_conv, ml_head_gain, ml_w_out, rw_mu, rw_w_rkv, rw_w0, rw_w1, rw_w2, rw_a0, rw_a1, rw_a2, rw_g1, rw_g2, rw_k_k, rw_k_a, rw_r_k, rw_ln_gain, rw_ln_bias, rw_w_o, hg_w_in, hg_f_bias, hg_lb, hg_head_gain, hg_w_out, moe_router, moe_w_gate, moe_w_up, moe_w_down, ple_w, ple_gate_down, ple_gate_up):
    B, S, D = x.shape
    depth = p.shape[0]
    T = B * S
    h = x.reshape(T, D)
    hn = rms_norm_bf16(h, norm_mix[0])
    for i in range(depth):
        kind, j = i % 3, i // 3
        if kind == 0:
            h = mlstm_layer(h, hn, ml_w_in[j], ml_gate_bias[j], ml_conv[j], ml_head_gain[j], ml_w_out[j], B, S)
        elif kind == 1:
            h = rwkv_layer(h, hn, rw_mu[j], rw_w_rkv[j], rw_w0[j], rw_w1[j], rw_w2[j], rw_a0[j], rw_a1[j],
                           rw_a2[j], rw_g1[j], rw_g2[j], rw_k_k[j], rw_k_a[j], rw_r_k[j], rw_ln_gain[j],
                           rw_ln_bias[j], rw_w_o[j], B, S)
        else:
            h = hgrn2_layer(h, hn, hg_w_in[j], hg_f_bias[j], hg_lb, i, hg_head_gain[j], hg_w_out[j], B, S)
        hn2, logits = rms_norm_router(h, norm_ffn[i], moe_router[i])
        h = moe_layer(h, hn2, logits, moe_w_gate[i], moe_w_up[i], moe_w_down[i], B, S)
        last = i + 1 == depth
        h, hn = ple_layer(h, p[i].reshape(T, -1), ple_w[i], ple_gate_down[i], ple_gate_up[i], norm_ple_gate[i],
                          norm_ple_post[i], norm_final if last else norm_mix[i + 1], F32 if last else BF16)
    return hn.reshape(B, S, D)
```

```python
import functools

import jax
import jax.numpy as jnp
from jax import lax
from jax.experimental import pallas as pl
from jax.experimental.pallas import tpu as pltpu

F32 = jnp.float32
BF16 = jnp.bfloat16

NORM_EPS = 1e-6
GATE_CAP = 15.0
ML_CHUNK = 128
RW_HEAD = 64
RW_GN_EPS = 64e-5
HG_DK = 128
HG_CHUNK = 64
EC_CAPACITY = 2

LANES = 128
VMEM_LIMIT = 52 * 1024 * 1024


def _cparams(*sem):
    return pltpu.CompilerParams(dimension_semantics=sem, vmem_limit_bytes=VMEM_LIMIT)


def _tile(n, prefs):
    for t in prefs:
        if n % t == 0:
            return t
    return n


def _rms(x, gain):
    return x * lax.rsqrt(jnp.mean(x * x, axis=-1, keepdims=True) + NORM_EPS) * gain


def _norm_kernel(h_ref, g_ref, o_ref):
    o_ref[...] = _rms(h_ref[...], g_ref[...]).astype(o_ref.dtype)


def rms_norm_bf16(h, gain):
    T, D = h.shape
    tm = _tile(T, (256, 128, 64, 32, 16, 8))
    return pl.pallas_call(
        _norm_kernel,
        grid=(T // tm,),
        in_specs=[pl.BlockSpec((tm, D), lambda i: (i, 0)), pl.BlockSpec((1, D), lambda i: (0, 0))],
        out_specs=pl.BlockSpec((tm, D), lambda i: (i, 0)),
        out_shape=jax.ShapeDtypeStruct((T, D), BF16),
        compiler_params=_cparams("parallel"),
        name="rms_norm",
    )(h, gain.reshape(1, D))


def _router_kernel(h_ref, g_ref, r_ref, l_ref):
    y = _rms(h_ref[...], g_ref[...])
    l_ref[...] = jnp.dot(y, r_ref[0], preferred_element_type=F32, precision=lax.Precision.HIGHEST)


def router_logits(h, gain, router, layer):
    T, D = h.shape
    E = router.shape[2]
    rp = jnp.pad(router, ((0, 0), (0, 0), (0, LANES - E)))
    tm = _tile(T, (256, 128, 64, 32, 16, 8))
    logits = pl.pallas_call(
        _router_kernel,
        grid=(T // tm,),
        in_specs=[pl.BlockSpec((tm, D), lambda i: (i, 0)), pl.BlockSpec((1, D), lambda i: (0, 0)),
                  pl.BlockSpec((1, D, LANES), lambda i: (layer, 0, 0))],
        out_specs=pl.BlockSpec((tm, LANES), lambda i: (i, 0)),
        out_shape=jax.ShapeDtypeStruct((T, LANES), F32),
        compiler_params=_cparams("parallel"),
        name="router_logits",
    )(h, gain.reshape(1, D), rp)
    return logits[:, :E]


_CAST_ROWS = 256


def _cast_weight(w_ref, wb_ref):
    k = w_ref.shape[0]
    rows = _tile(k, (_CAST_ROWS, 128, 64, 32, 16))

    def body(c, _):
        r = pl.multiple_of(c * rows, rows)
        wb_ref[pl.ds(r, rows), :] = w_ref[pl.ds(r, rows), :].astype(BF16)
        return 0

    lax.fori_loop(0, k // rows, body, 0)


def _mm_kernel(x_ref, w_ref, *rest, has_res, has_bias, act):
    rest = list(rest)
    b_ref = rest.pop(0) if has_bias else None
    r_ref = rest.pop(0) if has_res else None
    o_ref, wb_ref = rest

    @pl.when(pl.program_id(1) == 0)
    def _():
        _cast_weight(w_ref.at[0], wb_ref)

    acc = jnp.dot(x_ref[...], wb_ref[...], preferred_element_type=F32)
    if has_bias:
        acc = acc + b_ref[...]
    if act is not None:
        acc = act(acc)
    if has_res:
        acc = acc + r_ref[...]
    o_ref[...] = acc.astype(o_ref.dtype)


def matmul(x, w, layer=0, *, n=None, col_off=0, bias=None, residual=None, act=None, out_dtype=F32, tm=None, tn=None):
    M, K = x.shape
    if w.ndim == 2:
        w = w[None]
    if n is None:
        n = w.shape[2] - col_off
    tm = tm or _tile(M, (1024, 512, 256, 128, 64, 32, 16))
    tn = tn or _tile(n, (512, 256, 128))
    assert n % tn == 0 and col_off % tn == 0 and M % tm == 0
    cb = col_off // tn
    in_specs = [pl.BlockSpec((tm, K), lambda j, i: (i, 0)),
                pl.BlockSpec((1, K, tn), lambda j, i: (layer, 0, j + cb))]
    args = [x, w]
    if bias is not None:
        in_specs.append(pl.BlockSpec((1, tn), lambda j, i: (0, j)))
        args.append(bias.reshape(1, n))
    if residual is not None:
        in_specs.append(pl.BlockSpec((tm, tn), lambda j, i: (i, j)))
        args.append(residual)
    return pl.pallas_call(
        functools.partial(_mm_kernel, has_res=residual is not None, has_bias=bias is not None, act=act),
        grid=(n // tn, M // tm),
        in_specs=in_specs,
        out_specs=pl.BlockSpec((tm, tn), lambda j, i: (i, j)),
        out_shape=jax.ShapeDtypeStruct((M, n), out_dtype),
        scratch_shapes=[pltpu.VMEM((K, tn), BF16)],
        compiler_params=_cparams("arbitrary", "arbitrary"),
        name="matmul",
    )(*args)


def _lora_kernel(x_ref, a_ref, b_ref, bias_ref, o_ref, ab_ref, bb_ref, *, act1, act2):
    @pl.when(pl.program_id(0) == 0)
    def _():
        _cast_weight(a_ref, ab_ref)
        _cast_weight(b_ref, bb_ref)

    t = jnp.dot(x_ref[...], ab_ref[...], preferred_element_type=F32)
    if act1 is not None:
        t = act1(t)
    y = jnp.dot(t.astype(BF16), bb_ref[...], preferred_element_type=F32) + bias_ref[...]
    if act2 is not None:
        y = act2(y)
    o_ref[...] = y.astype(o_ref.dtype)


def lora(x, a, b, bias=None, act1=None, act2=None, out_dtype=F32, tm=256):
    M, K = x.shape
    R, N = b.shape
    rp = -R % LANES
    if rp:
        a = jnp.pad(a, ((0, 0), (0, rp)))
        b = jnp.pad(b, ((0, rp), (0, 0)))
        R += rp
    if bias is None:
        bias = jnp.zeros((N,), F32)
    tm = _tile(M, (tm, 128, 64, 32, 16))
    return pl.pallas_call(
        functools.partial(_lora_kernel, act1=act1, act2=act2),
        grid=(M // tm,),
        in_specs=[pl.BlockSpec((tm, K), lambda i: (i, 0)), pl.BlockSpec((K, R), lambda i: (0, 0)),
                  pl.BlockSpec((R, N), lambda i: (0, 0)), pl.BlockSpec((1, N), lambda i: (0, 0))],
        out_specs=pl.BlockSpec((tm, N), lambda i: (i, 0)),
        out_shape=jax.ShapeDtypeStruct((M, N), out_dtype),
        scratch_shapes=[pltpu.VMEM((K, R), BF16), pltpu.VMEM((R, N), BF16)],
        compiler_params=_cparams("arbitrary"),
        name="lora",
    )(x, a, b, bias.reshape(1, N))


def _ple_kernel(h_ref, p_ref, wp_ref, gd_ref, gu_ref, gg_ref, pg_ref, ng_ref, ho_ref, no_ref,
                wpb_ref, gdb_ref, gub_ref):
    @pl.when(pl.program_id(0) == 0)
    def _():
        _cast_weight(wp_ref, wpb_ref)
        _cast_weight(gd_ref, gdb_ref)
        _cast_weight(gu_ref, gub_ref)

    h = h_ref[...]
    e = jnp.dot(p_ref[...].astype(BF16), wpb_ref[...], preferred_element_type=F32)
    e = _rms(e, pg_ref[...])
    t = jnp.dot(_rms(h, gg_ref[...]).astype(BF16), gdb_ref[...], preferred_element_type=F32)
    g = jnp.dot(t.astype(BF16), gub_ref[...], preferred_element_type=F32)
    hn = h + e * jax.nn.sigmoid(g)
    ho_ref[...] = hn
    no_ref[...] = _rms(hn, ng_ref[...]).astype(no_ref.dtype)


def ple_layer(h, p, w_ple, g_down, g_up, gate_gain, post_gain, next_gain, next_dtype):
    T, D = h.shape
    P = p.shape[1]
    tm = _tile(T, (256, 128, 64, 32, 16, 8))
    row = lambda i: (i, 0)
    fix = lambda i: (0, 0)
    return pl.pallas_call(
        _ple_kernel,
        grid=(T // tm,),
        in_specs=[pl.BlockSpec((tm, D), row), pl.BlockSpec((tm, P), row), pl.BlockSpec((P, D), fix),
                  pl.BlockSpec((D, P), fix), pl.BlockSpec((P, D), fix), pl.BlockSpec((1, D), fix),
                  pl.BlockSpec((1, D), fix), pl.BlockSpec((1, D), fix)],
        out_specs=[pl.BlockSpec((tm, D), row), pl.BlockSpec((tm, D), row)],
        out_shape=[jax.ShapeDtypeStruct((T, D), F32), jax.ShapeDtypeStruct((T, D), next_dtype)],
        scratch_shapes=[pltpu.VMEM((P, D), BF16), pltpu.VMEM((D, P), BF16), pltpu.VMEM((P, D), BF16)],
        compiler_params=_cparams("arbitrary"),
        name="ple",
    )(h, p, w_ple, g_down, g_up, gate_gain.reshape(1, D), post_gain.reshape(1, D), next_gain.reshape(1, D))


MOE_NORM_ROWS = 256


def _moe_up_kernel(tok_ref, h_ref, gain_ref, wg_ref, wu_ref, o_ref, xg_ref, xb_ref, wgb_ref, wub_ref, sem):
    rows = xg_ref.shape[0]

    @pl.when(pl.program_id(1) == 0)
    def _():
        def issue(i, _):
            pltpu.make_async_copy(h_ref.at[pl.ds(tok_ref[0, 0, i], 1), :], xg_ref.at[pl.ds(i, 1), :], sem).start()
            return 0

        lax.fori_loop(0, rows, issue, 0)
        pltpu.make_async_copy(h_ref.at[pl.ds(0, rows), :], xg_ref, sem).wait()

        step = _tile(rows, (MOE_NORM_ROWS, 128, 64, 32, 16))

        def norm(c, _):
            r = pl.multiple_of(c * step, step)
            xb_ref[pl.ds(r, step), :] = _rms(xg_ref[pl.ds(r, step), :], gain_ref[...]).astype(BF16)
            return 0

        lax.fori_loop(0, rows // step, norm, 0)

    _cast_weight(wg_ref.at[0, 0], wgb_ref)
    _cast_weight(wu_ref.at[0, 0], wub_ref)
    x = xb_ref[...]
    g = jnp.dot(x, wgb_ref[...], preferred_element_type=F32)
    u = jnp.dot(x, wub_ref[...], preferred_element_type=F32)
    o_ref[0] = (g * jax.nn.sigmoid(g) * u).astype(o_ref.dtype)


def _moe_down_kernel(h_ref, wd_ref, s_ref, o_ref, wdb_ref):
    _cast_weight(wd_ref.at[0, 0], wdb_ref)
    o_ref[0] = jnp.dot(h_ref[0], wdb_ref[...], preferred_element_type=F32) * s_ref[0]


def moe_ffn(h, gain, tok, gsel, w_gate, w_up, w_down, layer):
    T, D = h.shape
    E, R = tok.shape
    FF = w_gate.shape[3]
    tf = _tile(FF, (256, 128))
    hid = pl.pallas_call(
        _moe_up_kernel,
        grid=(E, FF // tf),
        in_specs=[pl.BlockSpec((1, 1, R), lambda e, f: (e, 0, 0), memory_space=pltpu.SMEM),
                  pl.BlockSpec(memory_space=pl.ANY),
                  pl.BlockSpec((1, D), lambda e, f: (0, 0)),
                  pl.BlockSpec((1, 1, D, tf), lambda e, f: (layer, e, 0, f)),
                  pl.BlockSpec((1, 1, D, tf), lambda e, f: (layer, e, 0, f))],
        out_specs=pl.BlockSpec((1, R, tf), lambda e, f: (e, 0, f)),
        out_shape=jax.ShapeDtypeStruct((E, R, FF), BF16),
        scratch_shapes=[pltpu.VMEM((R, D), F32), pltpu.VMEM((R, D), BF16),
                        pltpu.VMEM((D, tf), BF16), pltpu.VMEM((D, tf), BF16), pltpu.SemaphoreType.DMA(())],
        compiler_params=_cparams("arbitrary", "arbitrary"),
        name="moe_up",
    )(tok.reshape(E, 1, R), h, gain.reshape(1, D), w_gate, w_up)
    tn = _tile(D, (1024, 512, 256, 128))
    return pl.pallas_call(
        _moe_down_kernel,
        grid=(E, D // tn),
        in_specs=[pl.BlockSpec((1, R, FF), lambda e, j: (e, 0, 0)),
                  pl.BlockSpec((1, 1, FF, tn), lambda e, j: (layer, e, 0, j)),
                  pl.BlockSpec((1, R, 1), lambda e, j: (e, 0, 0))],
        out_specs=pl.BlockSpec((1, R, tn), lambda e, j: (e, 0, j)),
        out_shape=jax.ShapeDtypeStruct((E, R, D), F32),
        scratch_shapes=[pltpu.VMEM((FF, tn), BF16)],
        compiler_params=_cparams("parallel", "parallel"),
        name="moe_down",
    )(hid, w_down, gsel.reshape(E, R, 1))


def moe_layer(h, gain, router, w_gate, w_up, w_down, layer, B, S):
    T, D = h.shape
    E = router.shape[2]
    cap = EC_CAPACITY * S // E
    logits = router_logits(h, gain, router, layer)
    aff = jax.nn.softmax(logits, axis=-1).reshape(B, S, E)
    gsel, idx = lax.top_k(jnp.swapaxes(aff, 1, 2), cap)
    tok = idx + (jnp.arange(B, dtype=idx.dtype) * S)[:, None, None]
    tok = jnp.swapaxes(tok, 0, 1).reshape(E, B * cap)
    gsel = jnp.swapaxes(gsel, 0, 1).reshape(E, B * cap)
    out = moe_ffn(h, gain, tok, gsel, w_gate, w_up, w_down, layer)
    return h.at[tok.reshape(-1)].add(out.reshape(-1, D))


def _softplus(x):
    return jnp.maximum(x, 0.0) + jnp.log(1.0 + jnp.exp(-jnp.abs(x)))


def _dwconv3_kernel(x_ref, w_ref, o_ref):
    x = x_ref[0].astype(F32)
    s = x.shape[0]
    t = lax.broadcasted_iota(jnp.int32, x.shape, 0)
    prev = jnp.where(t == 0, 0.0, pltpu.roll(x, 1, 0))
    nxt = jnp.where(t == s - 1, 0.0, pltpu.roll(x, s - 1, 0))
    o_ref[0] = (w_ref[0:1, :] * prev + w_ref[1:2, :] * x + w_ref[2:3, :] * nxt).astype(o_ref.dtype)


def dwconv3(x, w):
    B, S, _ = x.shape
    assert w.shape[0] == 3
    C = w.shape[1]
    blk = pl.BlockSpec((1, S, LANES), lambda b, j: (b, 0, j))
    return pl.pallas_call(
        _dwconv3_kernel,
        grid=(B, C // LANES),
        in_specs=[blk, pl.BlockSpec((3, LANES), lambda b, j: (0, j))],
        out_specs=blk,
        out_shape=jax.ShapeDtypeStruct((B, S, C), BF16),
        compiler_params=_cparams("parallel", "parallel"),
        name="dwconv3",
    )(x, w)


ML_EXT = LANES
ML_BLOCK_H = 4


def _mlstm_scan_kernel(q_ref, k_ref, v_ref, g_ref, gb_ref, o_ref, c_ref, m_ref, *, heads):
    hg = pl.program_id(1)
    d = pl.program_id(2)
    L = q_ref.shape[0]
    hb = c_ref.shape[0]
    dk = q_ref.shape[1] // hb
    dv = v_ref.shape[1] // hb

    @pl.when(pl.program_id(3) == 0)
    def _():
        c_ref[...] = jnp.zeros_like(c_ref)
        m_ref[...] = jnp.full_like(m_ref, -1e30)

    sgn = jnp.where(d == 0, 1, -1)
    rl = lax.broadcasted_iota(jnp.int32, (L, L), 0)
    cl = lax.broadcasted_iota(jnp.int32, (L, L), 1)
    incl = (cl - rl) * sgn <= 0
    incl_t = (rl - cl) * sgn <= 0
    eye = rl == cl

    g = g_ref[...] + gb_ref[...]
    g = GATE_CAP * jnp.tanh(g / GATE_CAP)
    lane = lax.broadcasted_iota(jnp.int32, g.shape, 1)
    ones_col = jnp.where(lax.broadcasted_iota(jnp.int32, (L, ML_EXT), 1) == 0, 1.0, 0.0).astype(BF16)

    for hh in range(hb):
        i_idx = d * 2 * heads + hg * hb + hh
        i_col = jnp.sum(jnp.where(lane == i_idx, g, 0.0), axis=1, keepdims=True)
        f_col = -_softplus(-jnp.sum(jnp.where(lane == i_idx + heads, g, 0.0), axis=1, keepdims=True))
        i_row = jnp.sum(jnp.where(eye, i_col, 0.0), axis=0, keepdims=True)
        f_row = jnp.sum(jnp.where(eye, f_col, 0.0), axis=0, keepdims=True)
        b_col = jnp.sum(jnp.where(incl, f_row, 0.0), axis=1, keepdims=True)
        b_row = jnp.sum(jnp.where(incl_t, f_col, 0.0), axis=0, keepdims=True)
        gtot = jnp.sum(f_col, axis=0, keepdims=True)
        m = m_ref[hh, 0:1, 0:1]

        dm = b_col - b_row + i_row
        inter_log = b_col + m
        m_t = jnp.maximum(inter_log, jnp.max(jnp.where(incl, dm, -jnp.inf), axis=1, keepdims=True))
        pmat = jnp.where(incl, jnp.exp(jnp.where(incl, dm - m_t, 0.0)), 0.0)
        w_inter = jnp.exp(inter_log - m_t)

        q = q_ref[:, hh * dk:(hh + 1) * dk]
        k = k_ref[:, hh * dk:(hh + 1) * dk]
        s = lax.dot_general(q, k, (((1,), (1,)), ((), ())), preferred_element_type=F32) * pmat
        vext = jnp.concatenate([v_ref[:, hh * dv:(hh + 1) * dv], ones_col], axis=1)
        c = c_ref[hh]
        tot = (jnp.dot(s.astype(BF16), vext, preferred_element_type=F32)
               + w_inter * jnp.dot(q, c.astype(BF16), preferred_element_type=F32))
        den = tot[:, dv:dv + 1]
        o_ref[0, :, hh * dv:(hh + 1) * dv] = tot[:, :dv] / jnp.maximum(jnp.abs(den), jnp.exp(-m_t))

        src = gtot - b_col + i_col
        m_new = jnp.maximum(gtot + m, jnp.max(src, axis=0, keepdims=True))
        kw = (k.astype(F32) * jnp.exp(src - m_new)).astype(BF16)
        c_ref[hh] = jnp.exp(gtot + m - m_new) * c + lax.dot_general(
            kw, vext, (((0,), (0,)), ((), ())), preferred_element_type=F32)
        m_ref[hh] = jnp.broadcast_to(m_new, (1, LANES))


def mlstm_scan(qk, z, gates, gate_bias, heads, B, S):
    T, QK2 = qk.shape
    L = ML_CHUNK
    dk = QK2 // 2 // heads
    V = (z.shape[1] - QK2) // 2
    dv = V // heads
    nc = S // L
    hb = _tile(heads, (ML_BLOCK_H, 1))
    ng = heads // hb

    def trow(b, d, c):
        return b * nc + jnp.where(d == 0, c, nc - 1 - c)

    return pl.pallas_call(
        functools.partial(_mlstm_scan_kernel, heads=heads),
        grid=(B, ng, 2, nc),
        in_specs=[pl.BlockSpec((L, hb * dk), lambda b, h, d, c: (trow(b, d, c), h)),
                  pl.BlockSpec((L, hb * dk), lambda b, h, d, c: (trow(b, d, c), ng + h)),
                  pl.BlockSpec((L, hb * dv), lambda b, h, d, c: (trow(b, d, c), QK2 // (hb * dv) + h)),
                  pl.BlockSpec((L, LANES), lambda b, h, d, c: (trow(b, d, c), 0)),
                  pl.BlockSpec((1, LANES), lambda b, h, d, c: (0, 0))],
        out_specs=pl.BlockSpec((1, L, hb * dv), lambda b, h, d, c: (d, trow(b, d, c), h)),
        out_shape=jax.ShapeDtypeStruct((2, T, V), F32),
        scratch_shapes=[pltpu.VMEM((hb, dk, dv + ML_EXT), F32), pltpu.VMEM((hb, 1, LANES), F32)],
        compiler_params=_cparams("parallel", "parallel", "arbitrary", "arbitrary"),
        name="mlstm_scan",
    )(qk, qk, z, gates, gate_bias.reshape(1, LANES))


def mlstm_layer(h, hn, w_in, gate_bias, conv_w, head_gain, w_out, B, S, lj=0):
    T, D = h.shape
    H = gate_bias.shape[0] // 4
    QK = conv_w.shape[1] // 2
    V = w_out.shape[-2]
    DK, DV = QK // H, V // H
    nmain = 2 * QK + 2 * V
    z = matmul(hn, w_in, lj, n=nmain, out_dtype=BF16)
    pad = LANES - 4 * H
    w_gates = (w_in[lj] if w_in.ndim == 3 else w_in)[:, nmain:]
    gates = matmul(hn, jnp.pad(w_gates, ((0, 0), (0, pad))))
    taps = conv_w * jnp.concatenate([jnp.full((QK,), DK ** -0.5, F32), jnp.ones((QK,), F32)])
    qk = dwconv3(z.reshape(B, S, nmain), taps).reshape(T, 2 * QK)
    hs = mlstm_scan(qk, z, gates, jnp.pad(gate_bias, (0, pad)), H, B, S)
    gated = head_out(hs, z, 2 * QK + V, head_gain, DV, jax.nn.sigmoid)
    return matmul(gated, w_out, lj, residual=h)


RW_CHUNK = 64
RW_SUB = 16
RW_BLOCK_T = 128
RW_BLOCK_H = 32
assert RW_CHUNK == RW_HEAD


def _bdot(a, b):
    return jnp.dot(a, b, preferred_element_type=F32)


def _split(x):
    hi = x.astype(BF16)
    return hi, (x - hi.astype(F32)).astype(BF16)


def _dot_exact_lhs(m, x):
    xh, xl = _split(x)
    return _bdot(m, xh) + _bdot(m, xl)


def _dot_exact_rhs(x, m):
    xh, xl = _split(x)
    return _bdot(xh, m) + _bdot(xl, m)


def _tri_inverse(a, blk, eye, nblk, mm, bd):
    d = jnp.where(blk, a, 0.0)
    off = a - d
    d2 = mm(d, bd(d))
    d4 = mm(d2, bd(d2))
    d8 = mm(d4, bd(d4))
    p = eye + d
    p = p + mm(p, bd(d2))
    p = p + mm(p, bd(d4))
    p = p + mm(p, bd(d8))
    n = mm(p, bd(off))
    q = eye + n
    pw, reach = n, 1
    while 2 * reach < nblk:
        pw = mm(pw, bd(pw))
        q = q + mm(q, bd(pw))
        reach *= 2
    return mm(q, bd(p))


def _rwkv_scan_kernel(r_ref, k_ref, v_ref, wl_ref, a_ref, kk_ref, ka_ref, y_ref, s_ref):
    d = pl.program_id(2)
    L, N = RW_CHUNK, RW_HEAD
    W = 2 * N
    tb, hw = r_ref.shape
    nch, P = tb // L, hw // W

    @pl.when(pl.program_id(3) == 0)
    def _():
        s_ref[...] = jnp.zeros_like(s_ref)

    fwd = d == 0
    sgn = jnp.where(fwd, 1, -1)
    row = lax.broadcasted_iota(jnp.int32, (L, W), 0)
    lane = lax.broadcasted_iota(jnp.int32, (L, W), 1)
    cs = lane % L
    ahead = (cs - row) * sgn
    strict = ahead < 0
    incl = ahead <= 0
    eye = jnp.where(cs == row, 1.0, 0.0)
    blk = (row // RW_SUB) == (cs // RW_SUB)
    low = lane < N
    rl = lax.broadcasted_iota(jnp.int32, (L, L), 0)
    cl = lax.broadcasted_iota(jnp.int32, (L, L), 1)
    incl_bf = jnp.where((cl - rl) * sgn <= 0, 1.0, 0.0).astype(BF16)
    br = lax.broadcasted_iota(jnp.int32, (W, W), 0) // N
    bc = lax.broadcasted_iota(jnp.int32, (W, W), 1) // N
    same = br == bc
    seg = jnp.where(same, 1.0, 0.0).astype(BF16)

    def stack(x):
        return jnp.stack([x[:, p * W:(p + 1) * W] for p in range(P)], axis=0)

    def bd(x):
        return jnp.where(same, jnp.concatenate([x, x], axis=1), 0.0).astype(BF16)

    def mm(a, b):
        return lax.dot_general(a.astype(BF16), b, (((2,), (1,)), ((0,), (0,))), preferred_element_type=F32)

    def mm_nt(a, b):
        return lax.dot_general(a.astype(BF16), b, (((2,), (2,)), ((0,), (0,))), preferred_element_type=F32)

    for ci in range(nch):
        t0 = pl.multiple_of(jnp.where(fwd, ci * L, (nch - 1 - ci) * L), L)
        rows = pl.ds(t0, L)
        r = r_ref[rows, :]
        k = k_ref[rows, :]
        v = v_ref[rows, :]
        rate = a_ref[rows, :]
        logw = -jnp.exp(-_softplus(-wl_ref[rows, :]) - 0.5)
        kk = k * kk_ref[...]
        ssq = kk * kk
        nrm2 = jnp.concatenate([_dot_exact_rhs(ssq[:, p * W:(p + 1) * W], seg) for p in range(P)], axis=1)
        kk = kk / jnp.maximum(jnp.sqrt(nrm2), 1e-12)
        kd = k * (1.0 + (rate - 1.0) * ka_ref[...])
        bv = kk * rate
        c = _dot_exact_lhs(incl_bf, logw)
        cend = jnp.sum(logw, axis=0, keepdims=True)
        einv = jnp.exp(-c)
        eend = jnp.exp(cend - c)
        at = stack(-kk * jnp.exp(c - logw))
        rt = stack(r * jnp.exp(c))
        vp = stack(v)
        ar = jnp.concatenate([at, rt], axis=1)
        g_b = mm_nt(ar, bd(stack(bv * einv)))
        g_k = mm_nt(ar, bd(stack(kd * einv)))
        aab = jnp.where(strict, g_b[:, :L], 0.0)
        rb = jnp.where(incl, g_b[:, L:], 0.0)
        aak = jnp.where(strict, g_k[:, :L], 0.0)
        rk = jnp.where(incl, g_k[:, L:], 0.0)
        tinv = _tri_inverse(aab, blk, eye, L // RW_SUB, mm, bd)
        s0 = s_ref[...]
        bd_s, bd_v = bd(s0), bd(vp)
        u = mm(tinv, bd(mm_nt(at, bd_s) + mm(aak, bd_v)))
        y = mm_nt(rt, bd_s) + mm(rb, bd(u)) + mm(rk, bd_v)
        for p in range(P):
            y_ref[0, rows, p * W:(p + 1) * W] = y[p]
        uv = jnp.concatenate([u, vp], axis=1).astype(BF16)
        bkh = jnp.concatenate([stack(bv * eend), stack(kd * eend)], axis=1).astype(BF16)
        g = lax.dot_general(uv, bkh, (((1,), (1,)), ((0,), (0,))), preferred_element_type=F32)
        s_ref[...] = s0 * stack(jnp.exp(cend)) + jnp.where(low, g[:, :N], g[:, N:])


def rwkv_scan(r, k, v, wl2, a2, k_k, k_a, B, S):
    T, D = r.shape
    tb = _tile(S, (RW_BLOCK_T, RW_CHUNK))
    hw = _tile(D, (RW_BLOCK_H * RW_HEAD, 4 * RW_HEAD, 2 * RW_HEAD))
    nt, nh = S // tb, D // hw

    def trow(b, d, c):
        return b * nt + jnp.where(d == 0, c, nt - 1 - c)

    shared = pl.BlockSpec((tb, hw), lambda b, g, d, c: (trow(b, d, c), g))
    perdir = pl.BlockSpec((tb, hw), lambda b, g, d, c: (trow(b, d, c), d * nh + g))
    par = pl.BlockSpec((1, hw), lambda b, g, d, c: (0, g))
    return pl.pallas_call(
        _rwkv_scan_kernel,
        grid=(B, nh, 2, nt),
        in_specs=[shared, shared, shared, perdir, perdir, par, par],
        out_specs=pl.BlockSpec((1, tb, hw), lambda b, g, d, c: (d, trow(b, d, c), g)),
        out_shape=jax.ShapeDtypeStruct((2, T, D), F32),
        scratch_shapes=[pltpu.VMEM((hw // (2 * RW_HEAD), RW_HEAD, 2 * RW_HEAD), F32)],
        compiler_params=_cparams("parallel", "parallel", "arbitrary", "arbitrary"),
        name="rwkv_scan",
    )(r, k, v, wl2, a2, k_k.reshape(1, D), k_a.reshape(1, D))


def _seg_sum(x, seg_bf):
    return _dot_exact_rhs(x, seg_bf)


def _rwkv_out_kernel(y_ref, r_ref, k_ref, v_ref, af_ref, ab_ref, g_ref, rk_ref, ka_ref, lg_ref, lb_ref, o_ref):
    w = y_ref.shape[2]
    li = lax.broadcasted_iota(jnp.int32, (w, w), 0) // RW_HEAD
    lj = lax.broadcasted_iota(jnp.int32, (w, w), 1) // RW_HEAD
    seg = jnp.where(li == lj, 1.0, 0.0).astype(BF16)
    inv_n = 1.0 / RW_HEAD
    y = y_ref[0] + y_ref[1]
    mean = _seg_sum(y, seg) * inv_n
    yc = y - mean
    var = _seg_sum(yc * yc, seg) * inv_n
    yn = yc * lax.rsqrt(var + RW_GN_EPS) * lg_ref[...] + lb_ref[...]
    a_mean = 0.5 * (af_ref[...] + ab_ref[...])
    k_bonus = k_ref[...] * (1.0 + (a_mean - 1.0) * ka_ref[...])
    bonus = _seg_sum(r_ref[...] * k_bonus * rk_ref[...], seg) * v_ref[...]
    o_ref[...] = ((yn + bonus) * g_ref[...]).astype(o_ref.dtype)


def rwkv_out(y, r, k, v, a2, gate, r_k, k_a, ln_gain, ln_bias):
    _, T, D = y.shape
    tm = _tile(T, (256, 128, 64, 32, 16, 8))
    w = _tile(D, (256, 128))
    nw = D // w
    tile = pl.BlockSpec((tm, w), lambda i, j: (i, j))
    par = pl.BlockSpec((1, w), lambda i, j: (0, j))
    return pl.pallas_call(
        _rwkv_out_kernel,
        grid=(T // tm, nw),
        in_specs=[pl.BlockSpec((2, tm, w), lambda i, j: (0, i, j)), tile, tile, tile,
                  tile, pl.BlockSpec((tm, w), lambda i, j: (i, nw + j)), tile, par, par, par, par],
        out_specs=tile,
        out_shape=jax.ShapeDtypeStruct((T, D), BF16),
        compiler_params=_cparams("parallel", "parallel"),
        name="rwkv_out",
    )(y, r, k, v, a2, a2, gate, r_k.reshape(1, D), k_a.reshape(1, D), ln_gain.reshape(1, D), ln_bias.reshape(1, D))


def _shift_mix_kernel(x_ref, mu_ref, *o_refs):
    x = x_ref[0].astype(F32)
    s = x.shape[0]
    t = lax.broadcasted_iota(jnp.int32, x.shape, 0)
    prev = jnp.where(t == 0, 0.0, pltpu.roll(x, 1, 0))
    nxt = jnp.where(t == s - 1, 0.0, pltpu.roll(x, s - 1, 0))
    dx = 0.5 * (prev + nxt) - x
    for j, o_ref in enumerate(o_refs):
        o_ref[0] = (x + dx * mu_ref[j:j + 1, :]).astype(o_ref.dtype)


def shift_mix(xn, mu):
    B, S, D = xn.shape
    J = mu.shape[0]
    w = LANES
    blk = pl.BlockSpec((1, S, w), lambda b, j: (b, 0, j))
    return pl.pallas_call(
        _shift_mix_kernel,
        grid=(B, D // w),
        in_specs=[blk, pl.BlockSpec((J, w), lambda b, j: (0, j))],
        out_specs=[blk] * J,
        out_shape=[jax.ShapeDtypeStruct((B, S, D), BF16)] * J,
        compiler_params=_cparams("parallel", "parallel"),
        name="shift_mix",
    )(xn, mu)


def rwkv_layer(h, hn, mu, w_rkv, w0, w1, w2, a0, a1, a2, g1, g2, k_k, k_a, r_k, ln_gain, ln_bias, w_o, B, S, lj=0):
    T, D = h.shape
    N = RW_HEAD
    H = D // N
    xm = [t.reshape(T, D) for t in shift_mix(hn.reshape(B, S, D), mu)]
    w_rkv = w_rkv.reshape(-1, D, D)
    r = matmul(xm[0], w_rkv, 3 * lj)
    k = matmul(xm[1], w_rkv, 3 * lj + 1)
    v = matmul(xm[2], w_rkv, 3 * lj + 2)

    def both_dirs(lo, hi):
        z = jnp.zeros_like(hi[0])
        return (jnp.concatenate([lo[0], lo[1]], axis=1),
                jnp.concatenate([jnp.concatenate([hi[0], z], axis=1), jnp.concatenate([z, hi[1]], axis=1)], axis=0))

    wa, wb = both_dirs(w1, w2)
    wl2 = lora(xm[3], wa, wb, bias=w0.reshape(-1), act1=jnp.tanh, tm=128)
    aa, ab = both_dirs(a1, a2)
    rate2 = lora(xm[4], aa, ab, bias=a0.reshape(-1), act2=jax.nn.sigmoid, tm=128)
    gate = lora(xm[5], g1, g2, act1=jax.nn.sigmoid)
    y = rwkv_scan(r, k, v, wl2, rate2, k_k, k_a, B, S)
    out = rwkv_out(y, r, k, v, rate2, gate, r_k, k_a, ln_gain, ln_bias)
    return matmul(out, w_o, lj, residual=h)


LOG2_E = 1.4426950408889634
HG_SUB = 16
HG_BLOCK_T = 128
HG_BLOCK_H = 8


def _hgrn_scan_kernel(q_ref, zf_ref, v_ref, fb_ref, lb_ref, o_ref, s_ref):
    d = pl.program_id(2)
    L, N, SUB = HG_CHUNK, HG_DK, HG_SUB
    tb, hw = q_ref.shape
    nch, P = tb // L, hw // N

    @pl.when(pl.program_id(3) == 0)
    def _():
        s_ref[...] = jnp.zeros_like(s_ref)

    fwd = d == 0
    sgn = jnp.where(fwd, 1, -1)
    rl = lax.broadcasted_iota(jnp.int32, (L, L), 0)
    cl = lax.broadcasted_iota(jnp.int32, (L, L), 1)
    incl_bf = jnp.where((cl - rl) * sgn <= 0, 1.0, 0.0).astype(BF16)
    blk_r, blk_c = rl // SUB, cl // SUB
    first_r = blk_r * SUB + jnp.where(fwd, 0, SUB - 1)
    sel_first = jnp.where(cl == first_r, 1.0, 0.0).astype(BF16)
    blk_before = (blk_c - blk_r) * sgn < 0
    at_or_before = (cl - rl) * sgn <= 0
    ones_nl = jnp.ones((N, L), BF16)

    def stack(x):
        return jnp.stack([x[:, p * N:(p + 1) * N] for p in range(P)], axis=0)

    def within_block(x, j):
        return jnp.concatenate(
            [jnp.broadcast_to(x[:, i * SUB + j:i * SUB + j + 1, :], (P, SUB, N)) for i in range(L // SUB)], axis=1)

    lb = lb_ref[...]

    for ci in range(nch):
        t0 = pl.multiple_of(jnp.where(fwd, ci * L, (nch - 1 - ci) * L), L)
        rows = pl.ds(t0, L)
        qv = q_ref[rows, :].astype(F32)
        q = qv * jax.nn.sigmoid(qv) * (N ** -0.5)
        sig = jax.nn.sigmoid(zf_ref[rows, :].astype(F32) + fb_ref[0])
        lf = jnp.log(lb + (1.0 - lb) * sig)
        kx = (1.0 - lb) * (1.0 - sig)
        vb = stack(v_ref[rows, :])
        b = _dot_exact_lhs(incl_bf, lf)
        ref = _dot_exact_lhs(sel_first, b - lf)
        bend = jnp.sum(lf, axis=0, keepdims=True)
        qh = stack(q * jnp.exp(b - ref)).astype(BF16)
        rows_att = []
        for i in range(L // SUB):
            ki = kx * jnp.exp(jnp.minimum(ref[i * SUB:i * SUB + 1, :] - b, 0.0))
            rows_att.append(lax.dot_general(qh[:, i * SUB:(i + 1) * SUB], stack(ki).astype(BF16),
                                            (((2,), (2,)), ((0,), (0,))), preferred_element_type=F32))
        att = jnp.where(blk_before, jnp.concatenate(rows_att, axis=1), 0.0)
        bs, ks, qs = stack(b * LOG2_E), stack(kx), stack(q)
        for j in range(SUB):
            pj = (qs * within_block(ks, j) * jnp.exp2(jnp.minimum(bs - within_block(bs, j), 0.0))).astype(BF16)
            tot = jnp.dot(pj.reshape(P * L, N), ones_nl, preferred_element_type=F32).reshape(P, L, L)
            att = jnp.where((cl == blk_r * SUB + j) & at_or_before, tot, att)
        s0 = s_ref[...]
        o = lax.dot_general(att.astype(BF16), vb, (((2,), (1,)), ((0,), (0,))), preferred_element_type=F32)
        o = o + lax.dot_general(stack(q * jnp.exp(b)).astype(BF16), s0.astype(BF16),
                                (((2,), (2,)), ((0,), (0,))), preferred_element_type=F32)
        for p in range(P):
            o_ref[0, rows, p * N:(p + 1) * N] = o[p]
        kend = stack(kx * jnp.exp(bend - b)).astype(BF16)
        s_ref[...] = s0 * stack(jnp.exp(bend)) + lax.dot_general(
            vb, kend, (((1,), (1,)), ((0,), (0,))), preferred_element_type=F32)


def hgrn_scan(z, f_bias, lb, B, S):
    T, D5 = z.shape
    D = D5 // 5
    tb = _tile(S, (HG_BLOCK_T, HG_CHUNK))
    hw = _tile(D, (HG_BLOCK_H * HG_DK, 4 * HG_DK, 2 * HG_DK, HG_DK))
    nt, nh = S // tb, D // hw

    def trow(b, d, c):
        return b * nt + jnp.where(d == 0, c, nt - 1 - c)

    return pl.pallas_call(
        _hgrn_scan_kernel,
        grid=(B, nh, 2, nt),
        in_specs=[pl.BlockSpec((tb, hw), lambda b, g, d, c: (trow(b, d, c), g)),
                  pl.BlockSpec((tb, hw), lambda b, g, d, c: (trow(b, d, c), (1 + d) * nh + g)),
                  pl.BlockSpec((tb, hw), lambda b, g, d, c: (trow(b, d, c), 3 * nh + g)),
                  pl.BlockSpec((1, 1, hw), lambda b, g, d, c: (d, 0, g)),
                  pl.BlockSpec((1, hw), lambda b, g, d, c: (0, g))],
        out_specs=pl.BlockSpec((1, tb, hw), lambda b, g, d, c: (d, trow(b, d, c), g)),
        out_shape=jax.ShapeDtypeStruct((2, T, D), F32),
        scratch_shapes=[pltpu.VMEM((hw // HG_DK, HG_DK, HG_DK), F32)],
        compiler_params=_cparams("parallel", "parallel", "arbitrary", "arbitrary"),
        name="hgrn_scan",
    )(z, z, z, f_bias.reshape(2, 1, D), lb.reshape(1, D))


def _head_out_kernel(y_ref, g_ref, gain_ref, o_ref, *, head, act):
    y = y_ref[0] + y_ref[1]
    g = g_ref[...].astype(F32)
    gain = gain_ref[...]
    for p in range(y.shape[1] // head):
        cols = slice(p * head, (p + 1) * head)
        yp = y[:, cols]
        yn = yp * lax.rsqrt(jnp.mean(yp * yp, axis=-1, keepdims=True) + NORM_EPS) * gain[:, cols]
        o_ref[:, cols] = (yn * act(g[:, cols])).astype(o_ref.dtype)


def head_out(y, z, gate_col, gain, head, act):
    _, T, D = y.shape
    tm = _tile(T, (256, 128, 64, 32, 16, 8))
    w = _tile(D, (512, 256, 128))
    w = max(w, head)
    gb = gate_col // w
    return pl.pallas_call(
        functools.partial(_head_out_kernel, head=head, act=act),
        grid=(T // tm, D // w),
        in_specs=[pl.BlockSpec((2, tm, w), lambda i, j: (0, i, j)),
                  pl.BlockSpec((tm, w), lambda i, j: (i, gb + j)),
                  pl.BlockSpec((1, w), lambda i, j: (0, j))],
        out_specs=pl.BlockSpec((tm, w), lambda i, j: (i, j)),
        out_shape=jax.ShapeDtypeStruct((T, D), BF16),
        compiler_params=_cparams("parallel", "parallel"),
        name="head_out",
    )(y, z, gain.reshape(1, D))


def hgrn2_layer(h, hn, w_in, f_bias, lb_logits, layer_idx, head_gain, w_out, B, S, lj=0):
    T, D = h.shape
    z = matmul(hn, w_in, lj, out_dtype=BF16)
    probs = jax.nn.softmax(lb_logits, axis=0)
    lb = (jnp.cumsum(probs, axis=0) - probs[0])[layer_idx]
    o = hgrn_scan(z, f_bias, lb, B, S)
    gated = head_out(o, z, 4 * D, head_gain, HG_DK, jax.nn.silu)
    return matmul(gated, w_out, lj, residual=h)


def kernel(x, p, norm_mix, norm_ffn, norm_ple_gate, norm_ple_post, norm_final, ml_w_in, ml_gate_bias, ml_conv, ml_head_gain, ml_w_out, rw_mu, rw_w_rkv, rw_w0, rw_w1, rw_w2, rw_a0, rw_a1, rw_a2, rw_g1, rw_g2, rw_k_k, rw_k_a, rw_r_k, rw_ln_gain, rw_ln_bias, rw_w_o, hg_w_in, hg_f_bias, hg_lb, hg_head_gain, hg_w_out, moe_router, moe_w_gate, moe_w_up, moe_w_down, ple_w, ple_gate_down, ple_gate_up):
    B, S, D = x.shape
    depth = p.shape[0]
    T = B * S
    h = x.reshape(T, D)
    hn = rms_norm_bf16(h, norm_mix[0])
    for i in range(depth):
        kind, j = i % 3, i // 3
        if kind == 0:
            h = mlstm_layer(h, hn, ml_w_in, ml_gate_bias[j], ml_conv[j], ml_head_gain[j], ml_w_out, B, S, j)
        elif kind == 1:
            h = rwkv_layer(h, hn, rw_mu[j], rw_w_rkv, rw_w0[j], rw_w1[j], rw_w2[j], rw_a0[j], rw_a1[j],
                           rw_a2[j], rw_g1[j], rw_g2[j], rw_k_k[j], rw_k_a[j], rw_r_k[j], rw_ln_gain[j],
                           rw_ln_bias[j], rw_w_o, B, S, j)
        else:
            h = hgrn2_layer(h, hn, hg_w_in, hg_f_bias[j], hg_lb, i, hg_head_gain[j], hg_w_out, B, S, j)
        h = moe_layer(h, norm_ffn[i], moe_router, moe_w_gate, moe_w_up, moe_w_down, i, B, S)
        last = i + 1 == depth
        h, hn = ple_layer(h, p[i].reshape(T, -1), ple_w[i], ple_gate_down[i], ple_gate_up[i], norm_ple_gate[i],
                          norm_ple_post[i], norm_final if last else norm_mix[i + 1], F32 if last else BF16)
    return hn.reshape(B, S, D)
```

```python
import functools

import jax
import jax.numpy as jnp
from jax import lax
from jax.experimental import pallas as pl
from jax.experimental.pallas import tpu as pltpu

F32 = jnp.float32
BF16 = jnp.bfloat16

NORM_EPS = 1e-6
GATE_CAP = 15.0
ML_CHUNK = 128
RW_HEAD = 64
RW_GN_EPS = 64e-5
HG_DK = 128
HG_CHUNK = 64
EC_CAPACITY = 2

LANES = 128
VMEM_LIMIT = 52 * 1024 * 1024


def _cparams(*sem):
    return pltpu.CompilerParams(dimension_semantics=sem, vmem_limit_bytes=VMEM_LIMIT)


def _tile(n, prefs):
    for t in prefs:
        if n % t == 0:
            return t
    return n


def _rms(x, gain):
    return x * lax.rsqrt(jnp.mean(x * x, axis=-1, keepdims=True) + NORM_EPS) * gain


def _norm_kernel(h_ref, g_ref, o_ref):
    o_ref[...] = _rms(h_ref[...], g_ref[...]).astype(o_ref.dtype)


def rms_norm_bf16(h, gain):
    T, D = h.shape
    tm = _tile(T, (256, 128, 64, 32, 16, 8))
    return pl.pallas_call(
        _norm_kernel,
        grid=(T // tm,),
        in_specs=[pl.BlockSpec((tm, D), lambda i: (i, 0)), pl.BlockSpec((1, D), lambda i: (0, 0))],
        out_specs=pl.BlockSpec((tm, D), lambda i: (i, 0)),
        out_shape=jax.ShapeDtypeStruct((T, D), BF16),
        compiler_params=_cparams("parallel"),
        name="rms_norm",
    )(h, gain.reshape(1, D))


def _router_kernel(h_ref, g_ref, r_ref, l_ref):
    y = _rms(h_ref[...], g_ref[...])
    l_ref[...] = jnp.dot(y, r_ref[0], preferred_element_type=F32, precision=lax.Precision.HIGHEST)


def router_logits(h, gain, router, layer):
    T, D = h.shape
    E = router.shape[2]
    rp = jnp.pad(router, ((0, 0), (0, 0), (0, LANES - E)))
    tm = _tile(T, (256, 128, 64, 32, 16, 8))
    logits = pl.pallas_call(
        _router_kernel,
        grid=(T // tm,),
        in_specs=[pl.BlockSpec((tm, D), lambda i: (i, 0)), pl.BlockSpec((1, D), lambda i: (0, 0)),
                  pl.BlockSpec((1, D, LANES), lambda i: (layer, 0, 0))],
        out_specs=pl.BlockSpec((tm, LANES), lambda i: (i, 0)),
        out_shape=jax.ShapeDtypeStruct((T, LANES), F32),
        compiler_params=_cparams("parallel"),
        name="router_logits",
    )(h, gain.reshape(1, D), rp)
    return logits[:, :E]


_CAST_ROWS = 256


def _cast_weight(w_ref, wb_ref):
    k = w_ref.shape[0]
    rows = _tile(k, (_CAST_ROWS, 128, 64, 32, 16))

    def body(c, _):
        r = pl.multiple_of(c * rows, rows)
        wb_ref[pl.ds(r, rows), :] = w_ref[pl.ds(r, rows), :].astype(BF16)
        return 0

    lax.fori_loop(0, k // rows, body, 0)


def _mm_kernel(x_ref, w_ref, *rest, has_res, has_bias, act, w_is_nk):
    rest = list(rest)
    b_ref = rest.pop(0) if has_bias else None
    r_ref = rest.pop(0) if has_res else None
    o_ref, wb_ref = rest

    @pl.when(pl.program_id(1) == 0)
    def _():
        _cast_weight(w_ref.at[0], wb_ref)

    contract = (((1,), (1,)), ((), ())) if w_is_nk else (((1,), (0,)), ((), ()))
    acc = lax.dot_general(x_ref[...], wb_ref[...], contract, preferred_element_type=F32)
    if has_bias:
        acc = acc + b_ref[...]
    if act is not None:
        acc = act(acc)
    if has_res:
        acc = acc + r_ref[...]
    o_ref[...] = acc.astype(o_ref.dtype)


def matmul(x, w, layer=0, *, n=None, col_off=0, bias=None, residual=None, act=None, out_dtype=F32, tm=None, tn=None,
           w_is_nk=False):
    M, K = x.shape
    if w.ndim == 2:
        w = w[None]
    if n is None:
        n = w.shape[1 if w_is_nk else 2] - col_off
    tm = tm or _tile(M, (1024, 512, 256, 128, 64, 32, 16))
    tn = tn or _tile(n, (512, 256, 128))
    assert n % tn == 0 and col_off % tn == 0 and M % tm == 0
    cb = col_off // tn
    if w_is_nk:
        w_spec = pl.BlockSpec((1, tn, K), lambda j, i: (layer, j + cb, 0))
    else:
        w_spec = pl.BlockSpec((1, K, tn), lambda j, i: (layer, 0, j + cb))
    in_specs = [pl.BlockSpec((tm, K), lambda j, i: (i, 0)), w_spec]
    args = [x, w]
    if bias is not None:
        in_specs.append(pl.BlockSpec((1, tn), lambda j, i: (0, j)))
        args.append(bias.reshape(1, n))
    if residual is not None:
        in_specs.append(pl.BlockSpec((tm, tn), lambda j, i: (i, j)))
        args.append(residual)
    return pl.pallas_call(
        functools.partial(_mm_kernel, has_res=residual is not None, has_bias=bias is not None, act=act,
                          w_is_nk=w_is_nk),
        grid=(n // tn, M // tm),
        in_specs=in_specs,
        out_specs=pl.BlockSpec((tm, tn), lambda j, i: (i, j)),
        out_shape=jax.ShapeDtypeStruct((M, n), out_dtype),
        scratch_shapes=[pltpu.VMEM((tn, K) if w_is_nk else (K, tn), BF16)],
        compiler_params=_cparams("arbitrary", "arbitrary"),
        name="matmul",
    )(*args)


def _lora_kernel(x_ref, a_ref, b_ref, bias_ref, o_ref, ab_ref, bb_ref, *, act1, act2):
    @pl.when(pl.program_id(0) == 0)
    def _():
        _cast_weight(a_ref, ab_ref)
        _cast_weight(b_ref, bb_ref)

    t = jnp.dot(x_ref[...], ab_ref[...], preferred_element_type=F32)
    if act1 is not None:
        t = act1(t)
    y = jnp.dot(t.astype(BF16), bb_ref[...], preferred_element_type=F32) + bias_ref[...]
    if act2 is not None:
        y = act2(y)
    o_ref[...] = y.astype(o_ref.dtype)


def lora(x, a, b, bias=None, act1=None, act2=None, out_dtype=F32, tm=256):
    M, K = x.shape
    R, N = b.shape
    rp = -R % LANES
    if rp:
        a = jnp.pad(a, ((0, 0), (0, rp)))
        b = jnp.pad(b, ((0, rp), (0, 0)))
        R += rp
    if bias is None:
        bias = jnp.zeros((N,), F32)
    tm = _tile(M, (tm, 128, 64, 32, 16))
    return pl.pallas_call(
        functools.partial(_lora_kernel, act1=act1, act2=act2),
        grid=(M // tm,),
        in_specs=[pl.BlockSpec((tm, K), lambda i: (i, 0)), pl.BlockSpec((K, R), lambda i: (0, 0)),
                  pl.BlockSpec((R, N), lambda i: (0, 0)), pl.BlockSpec((1, N), lambda i: (0, 0))],
        out_specs=pl.BlockSpec((tm, N), lambda i: (i, 0)),
        out_shape=jax.ShapeDtypeStruct((M, N), out_dtype),
        scratch_shapes=[pltpu.VMEM((K, R), BF16), pltpu.VMEM((R, N), BF16)],
        compiler_params=_cparams("arbitrary"),
        name="lora",
    )(x, a, b, bias.reshape(1, N))


def _ple_kernel(h_ref, p_ref, wp_ref, gd_ref, gu_ref, gg_ref, pg_ref, ng_ref, ho_ref, no_ref,
                wpb_ref, gdb_ref, gub_ref):
    @pl.when(pl.program_id(0) == 0)
    def _():
        _cast_weight(wp_ref, wpb_ref)
        _cast_weight(gd_ref, gdb_ref)
        _cast_weight(gu_ref, gub_ref)

    h = h_ref[...]
    e = jnp.dot(p_ref[...].astype(BF16), wpb_ref[...], preferred_element_type=F32)
    e = _rms(e, pg_ref[...])
    t = jnp.dot(_rms(h, gg_ref[...]).astype(BF16), gdb_ref[...], preferred_element_type=F32)
    g = jnp.dot(t.astype(BF16), gub_ref[...], preferred_element_type=F32)
    hn = h + e * jax.nn.sigmoid(g)
    ho_ref[...] = hn
    no_ref[...] = _rms(hn, ng_ref[...]).astype(no_ref.dtype)


def ple_layer(h, p, w_ple, g_down, g_up, gate_gain, post_gain, next_gain, next_dtype):
    T, D = h.shape
    P = p.shape[1]
    tm = _tile(T, (256, 128, 64, 32, 16, 8))
    row = lambda i: (i, 0)
    fix = lambda i: (0, 0)
    return pl.pallas_call(
        _ple_kernel,
        grid=(T // tm,),
        in_specs=[pl.BlockSpec((tm, D), row), pl.BlockSpec((tm, P), row), pl.BlockSpec((P, D), fix),
                  pl.BlockSpec((D, P), fix), pl.BlockSpec((P, D), fix), pl.BlockSpec((1, D), fix),
                  pl.BlockSpec((1, D), fix), pl.BlockSpec((1, D), fix)],
        out_specs=[pl.BlockSpec((tm, D), row), pl.BlockSpec((tm, D), row)],
        out_shape=[jax.ShapeDtypeStruct((T, D), F32), jax.ShapeDtypeStruct((T, D), next_dtype)],
        scratch_shapes=[pltpu.VMEM((P, D), BF16), pltpu.VMEM((D, P), BF16), pltpu.VMEM((P, D), BF16)],
        compiler_params=_cparams("arbitrary"),
        name="ple",
    )(h, p, w_ple, g_down, g_up, gate_gain.reshape(1, D), post_gain.reshape(1, D), next_gain.reshape(1, D))


MOE_NORM_ROWS = 256


def _moe_up_kernel(tok_ref, nxt_ref, h_ref, gain_ref, wg_ref, wu_ref, o_ref, xg_ref, xb_ref, wgb_ref, wub_ref, sem):
    rows = xg_ref.shape[0]
    e = pl.program_id(0)

    def gather(idx_ref):
        def issue(i, _):
            pltpu.make_async_copy(h_ref.at[pl.ds(idx_ref[0, 0, i], 1), :], xg_ref.at[pl.ds(i, 1), :], sem).start()
            return 0

        lax.fori_loop(0, rows, issue, 0)

    @pl.when(pl.program_id(1) == 0)
    def _():
        @pl.when(e == 0)
        def _():
            gather(tok_ref)

        pltpu.make_async_copy(h_ref.at[pl.ds(0, rows), :], xg_ref, sem).wait()
        step = _tile(rows, (MOE_NORM_ROWS, 128, 64, 32, 16))

        def norm(c, _):
            r = pl.multiple_of(c * step, step)
            xb_ref[pl.ds(r, step), :] = _rms(xg_ref[pl.ds(r, step), :], gain_ref[...]).astype(BF16)
            return 0

        lax.fori_loop(0, rows // step, norm, 0)

        @pl.when(e + 1 < pl.num_programs(0))
        def _():
            gather(nxt_ref)

    _cast_weight(wg_ref.at[0, 0], wgb_ref)
    _cast_weight(wu_ref.at[0, 0], wub_ref)
    x = xb_ref[...]
    g = jnp.dot(x, wgb_ref[...], preferred_element_type=F32)
    u = jnp.dot(x, wub_ref[...], preferred_element_type=F32)
    o_ref[0] = (g * jax.nn.sigmoid(g) * u).astype(o_ref.dtype)


def _moe_down_kernel(h_ref, wd_ref, s_ref, o_ref, wdb_ref):
    _cast_weight(wd_ref.at[0, 0], wdb_ref)
    o_ref[0] = jnp.dot(h_ref[0], wdb_ref[...], preferred_element_type=F32) * s_ref[0]


def moe_ffn(h, gain, tok, gsel, w_gate, w_up, w_down, layer):
    T, D = h.shape
    E, R = tok.shape
    FF = w_gate.shape[3]
    tf = _tile(FF, (256, 128))
    hid = pl.pallas_call(
        _moe_up_kernel,
        grid=(E, FF // tf),
        in_specs=[pl.BlockSpec((1, 1, R), lambda e, f: (e, 0, 0), memory_space=pltpu.SMEM),
                  pl.BlockSpec((1, 1, R), lambda e, f: (jnp.minimum(e + 1, E - 1), 0, 0), memory_space=pltpu.SMEM),
                  pl.BlockSpec(memory_space=pl.ANY),
                  pl.BlockSpec((1, D), lambda e, f: (0, 0)),
                  pl.BlockSpec((1, 1, D, tf), lambda e, f: (layer, e, 0, f)),
                  pl.BlockSpec((1, 1, D, tf), lambda e, f: (layer, e, 0, f))],
        out_specs=pl.BlockSpec((1, R, tf), lambda e, f: (e, 0, f)),
        out_shape=jax.ShapeDtypeStruct((E, R, FF), BF16),
        scratch_shapes=[pltpu.VMEM((R, D), F32), pltpu.VMEM((R, D), BF16),
                        pltpu.VMEM((D, tf), BF16), pltpu.VMEM((D, tf), BF16), pltpu.SemaphoreType.DMA(())],
        compiler_params=_cparams("arbitrary", "arbitrary"),
        name="moe_up",
    )(tok.reshape(E, 1, R), tok.reshape(E, 1, R), h, gain.reshape(1, D), w_gate, w_up)
    tn = _tile(D, (1024, 512, 256, 128))
    return pl.pallas_call(
        _moe_down_kernel,
        grid=(E, D // tn),
        in_specs=[pl.BlockSpec((1, R, FF), lambda e, j: (e, 0, 0)),
                  pl.BlockSpec((1, 1, FF, tn), lambda e, j: (layer, e, 0, j)),
                  pl.BlockSpec((1, R, 1), lambda e, j: (e, 0, 0))],
        out_specs=pl.BlockSpec((1, R, tn), lambda e, j: (e, 0, j)),
        out_shape=jax.ShapeDtypeStruct((E, R, D), F32),
        scratch_shapes=[pltpu.VMEM((FF, tn), BF16)],
        compiler_params=_cparams("parallel", "parallel"),
        name="moe_down",
    )(hid, w_down, gsel.reshape(E, R, 1))


MOE_TILE = 256
MOE_CHUNK = 256


def _moe_combine_kernel(tile_ref, chunk_ref, next_ref, flag_ref, h_ref, cur_ref, nxt_ref, tok_ref, y_ref, o_ref,
                        buf_ref, sem):
    w = pl.program_id(0)
    nw = pl.num_programs(0)
    ch = buf_ref.shape[1]
    tm = h_ref.shape[0]
    slot = w % 2

    def gather(idx_ref, s):
        def issue(j, _):
            pltpu.make_async_copy(y_ref.at[pl.ds(idx_ref[0, 0, j], 1), :], buf_ref.at[s, pl.ds(j, 1), :],
                                  sem.at[s]).start()
            return 0

        lax.fori_loop(0, ch, issue, 0)

    @pl.when(w == 0)
    def _():
        gather(cur_ref, 0)

    @pl.when(w + 1 < nw)
    def _():
        gather(nxt_ref, 1 - slot)

    pltpu.make_async_copy(y_ref.at[pl.ds(0, ch), :], buf_ref.at[slot], sem.at[slot]).wait()
    flags = flag_ref[w]

    @pl.when(flags >= 2)
    def _():
        o_ref[...] = h_ref[...]

    @pl.when(flags % 2 == 1)
    def _():
        local = tok_ref[0] - tile_ref[w] * tm
        rows = lax.broadcasted_iota(jnp.int32, (tm, ch), 0)
        onehot = jnp.where(rows == local, 1.0, 0.0).astype(BF16)
        o_ref[...] += _dot_exact_lhs(onehot, buf_ref[slot])


def moe_combine(h, y, tok):
    T, D = h.shape
    n = tok.shape[0]
    tm = _tile(T, (MOE_TILE, 128, 64, 32, 16, 8))
    ch = _tile(n, (MOE_CHUNK, 128))
    nt, nch = T // tm, n // ch
    order = jnp.argsort(tok).astype(jnp.int32)
    stok = tok[order]
    starts = jnp.searchsorted(stok, jnp.arange(nt + 1, dtype=jnp.int32) * tm).astype(jnp.int32)
    g_lo = jnp.minimum(starts[:-1] // ch, nch - 1)
    g_hi = jnp.clip((starts[1:] - 1) // ch, g_lo, nch - 1)
    cnt = g_hi - g_lo + 1
    off = jnp.cumsum(cnt)
    nw = nch + nt
    wi = jnp.arange(nw, dtype=jnp.int32)
    tile = jnp.minimum(jnp.searchsorted(off, wi, side='right'), nt - 1).astype(jnp.int32)
    first_w = (off - cnt)[tile]
    valid = wi < off[-1]
    chunk = jnp.where(valid, g_lo[tile] + wi - first_w, nch - 1).astype(jnp.int32)
    flags = (valid.astype(jnp.int32) + 2 * (valid & (wi == first_w)).astype(jnp.int32))
    nxt = jnp.concatenate([chunk[1:], chunk[-1:]])
    idx3 = order.reshape(nch, 1, ch)
    grid_spec = pltpu.PrefetchScalarGridSpec(
        num_scalar_prefetch=4,
        grid=(nw,),
        in_specs=[pl.BlockSpec((tm, D), lambda w, t, c, x, f: (t[w], 0)),
                  pl.BlockSpec((1, 1, ch), lambda w, t, c, x, f: (c[w], 0, 0), memory_space=pltpu.SMEM),
                  pl.BlockSpec((1, 1, ch), lambda w, t, c, x, f: (x[w], 0, 0), memory_space=pltpu.SMEM),
                  pl.BlockSpec((1, 1, ch), lambda w, t, c, x, f: (c[w], 0, 0)),
                  pl.BlockSpec(memory_space=pl.ANY)],
        out_specs=pl.BlockSpec((tm, D), lambda w, t, c, x, f: (t[w], 0)),
        scratch_shapes=[pltpu.VMEM((2, ch, D), F32), pltpu.SemaphoreType.DMA((2,))],
    )

    return pl.pallas_call(
        _moe_combine_kernel,
        grid_spec=grid_spec,
        out_shape=jax.ShapeDtypeStruct((T, D), F32),
        compiler_params=_cparams("arbitrary"),
        name="moe_combine",
    )(tile, chunk, nxt, flags, h, idx3, idx3, stok.reshape(nch, 1, ch), y)


def moe_layer(h, gain, router, w_gate, w_up, w_down, layer, B, S):
    T, D = h.shape
    E = router.shape[2]
    cap = EC_CAPACITY * S // E
    logits = router_logits(h, gain, router, layer)
    aff = jax.nn.softmax(logits, axis=-1).reshape(B, S, E)
    gsel, idx = lax.top_k(jnp.swapaxes(aff, 1, 2), cap)
    tok = idx + (jnp.arange(B, dtype=idx.dtype) * S)[:, None, None]
    tok = jnp.swapaxes(tok, 0, 1).reshape(E, B * cap)
    gsel = jnp.swapaxes(gsel, 0, 1).reshape(E, B * cap)
    out = moe_ffn(h, gain, tok, gsel, w_gate, w_up, w_down, layer)
    return moe_combine(h, out.reshape(-1, D), tok.reshape(-1))


def _softplus(x):
    return jnp.maximum(x, 0.0) + jnp.log(1.0 + jnp.exp(-jnp.abs(x)))


def _dwconv3_kernel(x_ref, w_ref, o_ref):
    x = x_ref[0].astype(F32)
    s = x.shape[0]
    t = lax.broadcasted_iota(jnp.int32, x.shape, 0)
    prev = jnp.where(t == 0, 0.0, pltpu.roll(x, 1, 0))
    nxt = jnp.where(t == s - 1, 0.0, pltpu.roll(x, s - 1, 0))
    o_ref[0] = (w_ref[0:1, :] * prev + w_ref[1:2, :] * x + w_ref[2:3, :] * nxt).astype(o_ref.dtype)


def dwconv3(x, w):
    B, S, _ = x.shape
    assert w.shape[0] == 3
    C = w.shape[1]
    blk = pl.BlockSpec((1, S, LANES), lambda b, j: (b, 0, j))
    return pl.pallas_call(
        _dwconv3_kernel,
        grid=(B, C // LANES),
        in_specs=[blk, pl.BlockSpec((3, LANES), lambda b, j: (0, j))],
        out_specs=blk,
        out_shape=jax.ShapeDtypeStruct((B, S, C), BF16),
        compiler_params=_cparams("parallel", "parallel"),
        name="dwconv3",
    )(x, w)


ML_EXT = LANES
ML_BLOCK_H = 4


def _mlstm_scan_kernel(q_ref, k_ref, v_ref, g_ref, gb_ref, o_ref, c_ref, m_ref, *, heads):
    hg = pl.program_id(1)
    d = pl.program_id(2)
    L = q_ref.shape[0]
    hb = c_ref.shape[0]
    dk = q_ref.shape[1] // hb
    dv = v_ref.shape[1] // hb

    @pl.when(pl.program_id(3) == 0)
    def _():
        c_ref[...] = jnp.zeros_like(c_ref)
        m_ref[...] = jnp.full_like(m_ref, -1e30)

    sgn = jnp.where(d == 0, 1, -1)
    rl = lax.broadcasted_iota(jnp.int32, (L, L), 0)
    cl = lax.broadcasted_iota(jnp.int32, (L, L), 1)
    incl = (cl - rl) * sgn <= 0
    incl_t = (rl - cl) * sgn <= 0
    eye = rl == cl

    g = g_ref[...] + gb_ref[...]
    g = GATE_CAP * jnp.tanh(g / GATE_CAP)
    lane = lax.broadcasted_iota(jnp.int32, g.shape, 1)
    ones_col = jnp.where(lax.broadcasted_iota(jnp.int32, (L, ML_EXT), 1) == 0, 1.0, 0.0).astype(BF16)

    for hh in range(hb):
        i_idx = d * 2 * heads + hg * hb + hh
        i_col = jnp.sum(jnp.where(lane == i_idx, g, 0.0), axis=1, keepdims=True)
        f_col = -_softplus(-jnp.sum(jnp.where(lane == i_idx + heads, g, 0.0), axis=1, keepdims=True))
        i_row = jnp.sum(jnp.where(eye, i_col, 0.0), axis=0, keepdims=True)
        f_row = jnp.sum(jnp.where(eye, f_col, 0.0), axis=0, keepdims=True)
        b_col = jnp.sum(jnp.where(incl, f_row, 0.0), axis=1, keepdims=True)
        b_row = jnp.sum(jnp.where(incl_t, f_col, 0.0), axis=0, keepdims=True)
        gtot = jnp.sum(f_col, axis=0, keepdims=True)
        m = m_ref[hh, 0:1, 0:1]

        dm = b_col - b_row + i_row
        inter_log = b_col + m
        m_t = jnp.maximum(inter_log, jnp.max(jnp.where(incl, dm, -jnp.inf), axis=1, keepdims=True))
        pmat = jnp.where(incl, jnp.exp(jnp.where(incl, dm - m_t, 0.0)), 0.0)
        w_inter = jnp.exp(inter_log - m_t)

        q = q_ref[:, hh * dk:(hh + 1) * dk]
        k = k_ref[:, hh * dk:(hh + 1) * dk]
        s = lax.dot_general(q, k, (((1,), (1,)), ((), ())), preferred_element_type=F32) * pmat
        vext = jnp.concatenate([v_ref[:, hh * dv:(hh + 1) * dv], ones_col], axis=1)
        c = c_ref[hh]
        tot = (jnp.dot(s.astype(BF16), vext, preferred_element_type=F32)
               + w_inter * jnp.dot(q, c.astype(BF16), preferred_element_type=F32))
        den = tot[:, dv:dv + 1]
        o_ref[0, :, hh * dv:(hh + 1) * dv] = tot[:, :dv] / jnp.maximum(jnp.abs(den), jnp.exp(-m_t))

        src = gtot - b_col + i_col
        m_new = jnp.maximum(gtot + m, jnp.max(src, axis=0, keepdims=True))
        kw = (k.astype(F32) * jnp.exp(src - m_new)).astype(BF16)
        c_ref[hh] = jnp.exp(gtot + m - m_new) * c + lax.dot_general(
            kw, vext, (((0,), (0,)), ((), ())), preferred_element_type=F32)
        m_ref[hh] = jnp.broadcast_to(m_new, (1, LANES))


def mlstm_scan(qk, z, gates, gate_bias, heads, B, S):
    T, QK2 = qk.shape
    L = ML_CHUNK
    dk = QK2 // 2 // heads
    V = (z.shape[1] - QK2) // 2
    dv = V // heads
    nc = S // L
    hb = _tile(heads, (ML_BLOCK_H, 1))
    ng = heads // hb

    def trow(b, d, c):
        return b * nc + jnp.where(d == 0, c, nc - 1 - c)

    return pl.pallas_call(
        functools.partial(_mlstm_scan_kernel, heads=heads),
        grid=(B, ng, 2, nc),
        in_specs=[pl.BlockSpec((L, hb * dk), lambda b, h, d, c: (trow(b, d, c), h)),
                  pl.BlockSpec((L, hb * dk), lambda b, h, d, c: (trow(b, d, c), ng + h)),
                  pl.BlockSpec((L, hb * dv), lambda b, h, d, c: (trow(b, d, c), QK2 // (hb * dv) + h)),
                  pl.BlockSpec((L, LANES), lambda b, h, d, c: (trow(b, d, c), 0)),
                  pl.BlockSpec((1, LANES), lambda b, h, d, c: (0, 0))],
        out_specs=pl.BlockSpec((1, L, hb * dv), lambda b, h, d, c: (d, trow(b, d, c), h)),
        out_shape=jax.ShapeDtypeStruct((2, T, V), F32),
        scratch_shapes=[pltpu.VMEM((hb, dk, dv + ML_EXT), F32), pltpu.VMEM((hb, 1, LANES), F32)],
        compiler_params=_cparams("parallel", "parallel", "arbitrary", "arbitrary"),
        name="mlstm_scan",
    )(qk, qk, z, gates, gate_bias.reshape(1, LANES))


def mlstm_layer(h, hn, w_in, gate_bias, conv_w, head_gain, w_out, B, S, lj=0):
    T, D = h.shape
    H = gate_bias.shape[0] // 4
    QK = conv_w.shape[1] // 2
    V = w_out.shape[-2]
    DK, DV = QK // H, V // H
    nmain = 2 * QK + 2 * V
    w_t = jnp.swapaxes(w_in if w_in.ndim == 3 else w_in[None], 1, 2)
    z = matmul(hn, w_t, lj, n=nmain, out_dtype=BF16, w_is_nk=True)
    pad = LANES - 4 * H
    w_gates = lax.slice(w_t, (lj, nmain, 0), (lj + 1, nmain + 4 * H, D))[0]
    gates = matmul(hn, jnp.pad(w_gates, ((0, pad), (0, 0))), w_is_nk=True)
    taps = conv_w * jnp.concatenate([jnp.full((QK,), DK ** -0.5, F32), jnp.ones((QK,), F32)])
    qk = dwconv3(z.reshape(B, S, nmain), taps).reshape(T, 2 * QK)
    hs = mlstm_scan(qk, z, gates, jnp.pad(gate_bias, (0, pad)), H, B, S)
    gated = head_out(hs, z, 2 * QK + V, head_gain, DV, jax.nn.sigmoid)
    return matmul(gated, w_out, lj, residual=h)


RW_CHUNK = 64
RW_SUB = 16
RW_BLOCK_T = 128
RW_BLOCK_H = 32
assert RW_CHUNK == RW_HEAD


def _bdot(a, b):
    return jnp.dot(a, b, preferred_element_type=F32)


def _split(x):
    hi = x.astype(BF16)
    return hi, (x - hi.astype(F32)).astype(BF16)


def _dot_exact_lhs(m, x):
    xh, xl = _split(x)
    return _bdot(m, xh) + _bdot(m, xl)


def _dot_exact_rhs(x, m):
    xh, xl = _split(x)
    return _bdot(xh, m) + _bdot(xl, m)


def _tri_inverse(a, blk, eye, nblk, mm, bd):
    d = jnp.where(blk, a, 0.0)
    off = a - d
    d2 = mm(d, bd(d))
    d4 = mm(d2, bd(d2))
    d8 = mm(d4, bd(d4))
    p = eye + d
    p = p + mm(p, bd(d2))
    p = p + mm(p, bd(d4))
    p = p + mm(p, bd(d8))
    n = mm(p, bd(off))
    q = eye + n
    pw, reach = n, 1
    while 2 * reach < nblk:
        pw = mm(pw, bd(pw))
        q = q + mm(q, bd(pw))
        reach *= 2
    return mm(q, bd(p))


def _rwkv_scan_kernel(r_ref, k_ref, v_ref, wl_ref, a_ref, kk_ref, ka_ref, y_ref, s_ref):
    d = pl.program_id(2)
    L, N = RW_CHUNK, RW_HEAD
    W = 2 * N
    tb, hw = r_ref.shape
    nch, P = tb // L, hw // W

    @pl.when(pl.program_id(3) == 0)
    def _():
        s_ref[...] = jnp.zeros_like(s_ref)

    fwd = d == 0
    sgn = jnp.where(fwd, 1, -1)
    row = lax.broadcasted_iota(jnp.int32, (L, W), 0)
    lane = lax.broadcasted_iota(jnp.int32, (L, W), 1)
    cs = lane % L
    ahead = (cs - row) * sgn
    strict = ahead < 0
    incl = ahead <= 0
    eye = jnp.where(cs == row, 1.0, 0.0)
    blk = (row // RW_SUB) == (cs // RW_SUB)
    low = lane < N
    rl = lax.broadcasted_iota(jnp.int32, (L, L), 0)
    cl = lax.broadcasted_iota(jnp.int32, (L, L), 1)
    incl_bf = jnp.where((cl - rl) * sgn <= 0, 1.0, 0.0).astype(BF16)
    br = lax.broadcasted_iota(jnp.int32, (W, W), 0) // N
    bc = lax.broadcasted_iota(jnp.int32, (W, W), 1) // N
    same = br == bc
    seg = jnp.where(same, 1.0, 0.0).astype(BF16)

    def stack(x):
        return jnp.stack([x[:, p * W:(p + 1) * W] for p in range(P)], axis=0)

    def bd(x):
        return jnp.where(same, jnp.concatenate([x, x], axis=1), 0.0).astype(BF16)

    def mm(a, b):
        return lax.dot_general(a.astype(BF16), b, (((2,), (1,)), ((0,), (0,))), preferred_element_type=F32)

    def mm_nt(a, b):
        return lax.dot_general(a.astype(BF16), b, (((2,), (2,)), ((0,), (0,))), preferred_element_type=F32)

    for ci in range(nch):
        t0 = pl.multiple_of(jnp.where(fwd, ci * L, (nch - 1 - ci) * L), L)
        rows = pl.ds(t0, L)
        r = r_ref[rows, :]
        k = k_ref[rows, :]
        v = v_ref[rows, :]
        rate = a_ref[rows, :]
        logw = -jnp.exp(-_softplus(-wl_ref[rows, :]) - 0.5)
        kk = k * kk_ref[...]
        ssq = kk * kk
        nrm2 = jnp.concatenate([_dot_exact_rhs(ssq[:, p * W:(p + 1) * W], seg) for p in range(P)], axis=1)
        kk = kk / jnp.maximum(jnp.sqrt(nrm2), 1e-12)
        kd = k * (1.0 + (rate - 1.0) * ka_ref[...])
        bv = kk * rate
        c = _dot_exact_lhs(incl_bf, logw)
        cend = jnp.sum(logw, axis=0, keepdims=True)
        einv = jnp.exp(-c)
        eend = jnp.exp(cend - c)
        at = stack(-kk * jnp.exp(c - logw))
        rt = stack(r * jnp.exp(c))
        vp = stack(v)
        ar = jnp.concatenate([at, rt], axis=1)
        g_b = mm_nt(ar, bd(stack(bv * einv)))
        g_k = mm_nt(ar, bd(stack(kd * einv)))
        aab = jnp.where(strict, g_b[:, :L], 0.0)
        rb = jnp.where(incl, g_b[:, L:], 0.0)
        aak = jnp.where(strict, g_k[:, :L], 0.0)
        rk = jnp.where(incl, g_k[:, L:], 0.0)
        tinv = _tri_inverse(aab, blk, eye, L // RW_SUB, mm, bd)
        s0 = s_ref[...]
        bd_s, bd_v = bd(s0), bd(vp)
        u = mm(tinv, bd(mm_nt(at, bd_s) + mm(aak, bd_v)))
        y = mm_nt(rt, bd_s) + mm(rb, bd(u)) + mm(rk, bd_v)
        for p in range(P):
            y_ref[0, rows, p * W:(p + 1) * W] = y[p]
        uv = jnp.concatenate([u, vp], axis=1).astype(BF16)
        bkh = jnp.concatenate([stack(bv * eend), stack(kd * eend)], axis=1).astype(BF16)
        g = lax.dot_general(uv, bkh, (((1,), (1,)), ((0,), (0,))), preferred_element_type=F32)
        s_ref[...] = s0 * stack(jnp.exp(cend)) + jnp.where(low, g[:, :N], g[:, N:])


def rwkv_scan(r, k, v, wl2, a2, k_k, k_a, B, S):
    T, D = r.shape
    tb = _tile(S, (RW_BLOCK_T, RW_CHUNK))
    hw = _tile(D, (RW_BLOCK_H * RW_HEAD, 4 * RW_HEAD, 2 * RW_HEAD))
    nt, nh = S // tb, D // hw

    def trow(b, d, c):
        return b * nt + jnp.where(d == 0, c, nt - 1 - c)

    shared = pl.BlockSpec((tb, hw), lambda b, g, d, c: (trow(b, d, c), g))
    perdir = pl.BlockSpec((tb, hw), lambda b, g, d, c: (trow(b, d, c), d * nh + g))
    par = pl.BlockSpec((1, hw), lambda b, g, d, c: (0, g))
    return pl.pallas_call(
        _rwkv_scan_kernel,
        grid=(B, nh, 2, nt),
        in_specs=[shared, shared, shared, perdir, perdir, par, par],
        out_specs=pl.BlockSpec((1, tb, hw), lambda b, g, d, c: (d, trow(b, d, c), g)),
        out_shape=jax.ShapeDtypeStruct((2, T, D), F32),
        scratch_shapes=[pltpu.VMEM((hw // (2 * RW_HEAD), RW_HEAD, 2 * RW_HEAD), F32)],
        compiler_params=_cparams("parallel", "parallel", "arbitrary", "arbitrary"),
        name="rwkv_scan",
    )(r, k, v, wl2, a2, k_k.reshape(1, D), k_a.reshape(1, D))


def _seg_sum(x, seg_bf):
    return _dot_exact_rhs(x, seg_bf)


def _rwkv_out_kernel(y_ref, r_ref, k_ref, v_ref, af_ref, ab_ref, g_ref, rk_ref, ka_ref, lg_ref, lb_ref, o_ref):
    w = y_ref.shape[2]
    li = lax.broadcasted_iota(jnp.int32, (w, w), 0) // RW_HEAD
    lj = lax.broadcasted_iota(jnp.int32, (w, w), 1) // RW_HEAD
    seg = jnp.where(li == lj, 1.0, 0.0).astype(BF16)
    inv_n = 1.0 / RW_HEAD
    y = y_ref[0] + y_ref[1]
    mean = _seg_sum(y, seg) * inv_n
    yc = y - mean
    var = _seg_sum(yc * yc, seg) * inv_n
    yn = yc * lax.rsqrt(var + RW_GN_EPS) * lg_ref[...] + lb_ref[...]
    a_mean = 0.5 * (af_ref[...] + ab_ref[...])
    k_bonus = k_ref[...] * (1.0 + (a_mean - 1.0) * ka_ref[...])
    bonus = _seg_sum(r_ref[...] * k_bonus * rk_ref[...], seg) * v_ref[...]
    o_ref[...] = ((yn + bonus) * g_ref[...]).astype(o_ref.dtype)


def rwkv_out(y, r, k, v, a2, gate, r_k, k_a, ln_gain, ln_bias):
    _, T, D = y.shape
    tm = _tile(T, (256, 128, 64, 32, 16, 8))
    w = _tile(D, (256, 128))
    nw = D // w
    tile = pl.BlockSpec((tm, w), lambda i, j: (i, j))
    par = pl.BlockSpec((1, w), lambda i, j: (0, j))
    return pl.pallas_call(
        _rwkv_out_kernel,
        grid=(T // tm, nw),
        in_specs=[pl.BlockSpec((2, tm, w), lambda i, j: (0, i, j)), tile, tile, tile,
                  tile, pl.BlockSpec((tm, w), lambda i, j: (i, nw + j)), tile, par, par, par, par],
        out_specs=tile,
        out_shape=jax.ShapeDtypeStruct((T, D), BF16),
        compiler_params=_cparams("parallel", "parallel"),
        name="rwkv_out",
    )(y, r, k, v, a2, a2, gate, r_k.reshape(1, D), k_a.reshape(1, D), ln_gain.reshape(1, D), ln_bias.reshape(1, D))


def _shift_mix_kernel(x_ref, mu_ref, *o_refs):
    x = x_ref[0].astype(F32)
    s = x.shape[0]
    t = lax.broadcasted_iota(jnp.int32, x.shape, 0)
    prev = jnp.where(t == 0, 0.0, pltpu.roll(x, 1, 0))
    nxt = jnp.where(t == s - 1, 0.0, pltpu.roll(x, s - 1, 0))
    dx = 0.5 * (prev + nxt) - x
    for j, o_ref in enumerate(o_refs):
        o_ref[0] = (x + dx * mu_ref[j:j + 1, :]).astype(o_ref.dtype)


def shift_mix(xn, mu):
    B, S, D = xn.shape
    J = mu.shape[0]
    w = LANES
    blk = pl.BlockSpec((1, S, w), lambda b, j: (b, 0, j))
    return pl.pallas_call(
        _shift_mix_kernel,
        grid=(B, D // w),
        in_specs=[blk, pl.BlockSpec((J, w), lambda b, j: (0, j))],
        out_specs=[blk] * J,
        out_shape=[jax.ShapeDtypeStruct((B, S, D), BF16)] * J,
        compiler_params=_cparams("parallel", "parallel"),
        name="shift_mix",
    )(xn, mu)


def rwkv_layer(h, hn, mu, w_rkv, w0, w1, w2, a0, a1, a2, g1, g2, k_k, k_a, r_k, ln_gain, ln_bias, w_o, B, S, lj=0):
    T, D = h.shape
    N = RW_HEAD
    H = D // N
    xm = [t.reshape(T, D) for t in shift_mix(hn.reshape(B, S, D), mu)]
    w_rkv = w_rkv.reshape(-1, D, D)
    r = matmul(xm[0], w_rkv, 3 * lj)
    k = matmul(xm[1], w_rkv, 3 * lj + 1)
    v = matmul(xm[2], w_rkv, 3 * lj + 2)

    def both_dirs(lo, hi):
        z = jnp.zeros_like(hi[0])
        return (jnp.concatenate([lo[0], lo[1]], axis=1),
                jnp.concatenate([jnp.concatenate([hi[0], z], axis=1), jnp.concatenate([z, hi[1]], axis=1)], axis=0))

    wa, wb = both_dirs(w1, w2)
    wl2 = lora(xm[3], wa, wb, bias=w0.reshape(-1), act1=jnp.tanh, tm=128)
    aa, ab = both_dirs(a1, a2)
    rate2 = lora(xm[4], aa, ab, bias=a0.reshape(-1), act2=jax.nn.sigmoid, tm=128)
    gate = lora(xm[5], g1, g2, act1=jax.nn.sigmoid)
    y = rwkv_scan(r, k, v, wl2, rate2, k_k, k_a, B, S)
    out = rwkv_out(y, r, k, v, rate2, gate, r_k, k_a, ln_gain, ln_bias)
    return matmul(out, w_o, lj, residual=h)


LOG2_E = 1.4426950408889634
HG_SUB = 16
HG_BLOCK_T = 128
HG_BLOCK_H = 8


def _hgrn_scan_kernel(q_ref, zf_ref, v_ref, fb_ref, lb_ref, o_ref, s_ref):
    d = pl.program_id(2)
    L, N, SUB = HG_CHUNK, HG_DK, HG_SUB
    tb, hw = q_ref.shape
    nch, P = tb // L, hw // N

    @pl.when(pl.program_id(3) == 0)
    def _():
        s_ref[...] = jnp.zeros_like(s_ref)

    fwd = d == 0
    sgn = jnp.where(fwd, 1, -1)
    rl = lax.broadcasted_iota(jnp.int32, (L, L), 0)
    cl = lax.broadcasted_iota(jnp.int32, (L, L), 1)
    incl_bf = jnp.where((cl - rl) * sgn <= 0, 1.0, 0.0).astype(BF16)
    blk_r, blk_c = rl // SUB, cl // SUB
    first_r = blk_r * SUB + jnp.where(fwd, 0, SUB - 1)
    sel_first = jnp.where(cl == first_r, 1.0, 0.0).astype(BF16)
    blk_before = (blk_c - blk_r) * sgn < 0
    at_or_before = (cl - rl) * sgn <= 0
    ones_nl = jnp.ones((N, L), BF16)

    def stack(x):
        return jnp.stack([x[:, p * N:(p + 1) * N] for p in range(P)], axis=0)

    def within_block(x, j):
        return jnp.concatenate(
            [jnp.broadcast_to(x[:, i * SUB + j:i * SUB + j + 1, :], (P, SUB, N)) for i in range(L // SUB)], axis=1)

    lb = lb_ref[...]

    for ci in range(nch):
        t0 = pl.multiple_of(jnp.where(fwd, ci * L, (nch - 1 - ci) * L), L)
        rows = pl.ds(t0, L)
        qv = q_ref[rows, :].astype(F32)
        q = qv * jax.nn.sigmoid(qv) * (N ** -0.5)
        sig = jax.nn.sigmoid(zf_ref[rows, :].astype(F32) + fb_ref[0])
        lf = jnp.log(lb + (1.0 - lb) * sig)
        kx = (1.0 - lb) * (1.0 - sig)
        vb = stack(v_ref[rows, :])
        b = _dot_exact_lhs(incl_bf, lf)
        ref = _dot_exact_lhs(sel_first, b - lf)
        bend = jnp.sum(lf, axis=0, keepdims=True)
        qh = stack(q * jnp.exp(b - ref)).astype(BF16)
        rows_att = []
        for i in range(L // SUB):
            ki = kx * jnp.exp(jnp.minimum(ref[i * SUB:i * SUB + 1, :] - b, 0.0))
            rows_att.append(lax.dot_general(qh[:, i * SUB:(i + 1) * SUB], stack(ki).astype(BF16),
                                            (((2,), (2,)), ((0,), (0,))), preferred_element_type=F32))
        att = jnp.where(blk_before, jnp.concatenate(rows_att, axis=1), 0.0)
        bs, ks, qs = stack(b * LOG2_E), stack(kx), stack(q)
        for j in range(SUB):
            pj = (qs * within_block(ks, j) * jnp.exp2(jnp.minimum(bs - within_block(bs, j), 0.0))).astype(BF16)
            tot = jnp.dot(pj.reshape(P * L, N), ones_nl, preferred_element_type=F32).reshape(P, L, L)
            att = jnp.where((cl == blk_r * SUB + j) & at_or_before, tot, att)
        s0 = s_ref[...]
        o = lax.dot_general(att.astype(BF16), vb, (((2,), (1,)), ((0,), (0,))), preferred_element_type=F32)
        o = o + lax.dot_general(stack(q * jnp.exp(b)).astype(BF16), s0.astype(BF16),
                                (((2,), (2,)), ((0,), (0,))), preferred_element_type=F32)
        for p in range(P):
            o_ref[0, rows, p * N:(p + 1) * N] = o[p]
        kend = stack(kx * jnp.exp(bend - b)).astype(BF16)
        s_ref[...] = s0 * stack(jnp.exp(bend)) + lax.dot_general(
            vb, kend, (((1,), (1,)), ((0,), (0,))), preferred_element_type=F32)


def hgrn_scan(z, f_bias, lb, B, S):
    T, D5 = z.shape
    D = D5 // 5
    tb = _tile(S, (HG_BLOCK_T, HG_CHUNK))
    hw = _tile(D, (HG_BLOCK_H * HG_DK, 4 * HG_DK, 2 * HG_DK, HG_DK))
    nt, nh = S // tb, D // hw

    def trow(b, d, c):
        return b * nt + jnp.where(d == 0, c, nt - 1 - c)

    return pl.pallas_call(
        _hgrn_scan_kernel,
        grid=(B, nh, 2, nt),
        in_specs=[pl.BlockSpec((tb, hw), lambda b, g, d, c: (trow(b, d, c), g)),
                  pl.BlockSpec((tb, hw), lambda b, g, d, c: (trow(b, d, c), (1 + d) * nh + g)),
                  pl.BlockSpec((tb, hw), lambda b, g, d, c: (trow(b, d, c), 3 * nh + g)),
                  pl.BlockSpec((1, 1, hw), lambda b, g, d, c: (d, 0, g)),
                  pl.BlockSpec((1, hw), lambda b, g, d, c: (0, g))],
        out_specs=pl.BlockSpec((1, tb, hw), lambda b, g, d, c: (d, trow(b, d, c), g)),
        out_shape=jax.ShapeDtypeStruct((2, T, D), F32),
        scratch_shapes=[pltpu.VMEM((hw // HG_DK, HG_DK, HG_DK), F32)],
        compiler_params=_cparams("parallel", "parallel", "arbitrary", "arbitrary"),
        name="hgrn_scan",
    )(z, z, z, f_bias.reshape(2, 1, D), lb.reshape(1, D))


def _head_out_kernel(y_ref, g_ref, gain_ref, o_ref, *, head, act):
    y = y_ref[0] + y_ref[1]
    g = g_ref[...].astype(F32)
    gain = gain_ref[...]
    for p in range(y.shape[1] // head):
        cols = slice(p * head, (p + 1) * head)
        yp = y[:, cols]
        yn = yp * lax.rsqrt(jnp.mean(yp * yp, axis=-1, keepdims=True) + NORM_EPS) * gain[:, cols]
        o_ref[:, cols] = (yn * act(g[:, cols])).astype(o_ref.dtype)


def head_out(y, z, gate_col, gain, head, act):
    _, T, D = y.shape
    tm = _tile(T, (256, 128, 64, 32, 16, 8))
    w = _tile(D, (512, 256, 128))
    w = max(w, head)
    gb = gate_col // w
    return pl.pallas_call(
        functools.partial(_head_out_kernel, head=head, act=act),
        grid=(T // tm, D // w),
        in_specs=[pl.BlockSpec((2, tm, w), lambda i, j: (0, i, j)),
                  pl.BlockSpec((tm, w), lambda i, j: (i, gb + j)),
                  pl.BlockSpec((1, w), lambda i, j: (0, j))],
        out_specs=pl.BlockSpec((tm, w), lambda i, j: (i, j)),
        out_shape=jax.ShapeDtypeStruct((T, D), BF16),
        compiler_params=_cparams("parallel", "parallel"),
        name="head_out",
    )(y, z, gain.reshape(1, D))


def hgrn2_layer(h, hn, w_in, f_bias, lb_logits, layer_idx, head_gain, w_out, B, S, lj=0):
    T, D = h.shape
    z = matmul(hn, w_in, lj, out_dtype=BF16)
    probs = jax.nn.softmax(lb_logits, axis=0)
    lb = (jnp.cumsum(probs, axis=0) - probs[0])[layer_idx]
    o = hgrn_scan(z, f_bias, lb, B, S)
    gated = head_out(o, z, 4 * D, head_gain, HG_DK, jax.nn.silu)
    return matmul(gated, w_out, lj, residual=h)


def kernel(x, p, norm_mix, norm_ffn, norm_ple_gate, norm_ple_post, norm_final, ml_w_in, ml_gate_bias, ml_conv, ml_head_gain, ml_w_out, rw_mu, rw_w_rkv, rw_w0, rw_w1, rw_w2, rw_a0, rw_a1, rw_a2, rw_g1, rw_g2, rw_k_k, rw_k_a, rw_r_k, rw_ln_gain, rw_ln_bias, rw_w_o, hg_w_in, hg_f_bias, hg_lb, hg_head_gain, hg_w_out, moe_router, moe_w_gate, moe_w_up, moe_w_down, ple_w, ple_gate_down, ple_gate_up):
    B, S, D = x.shape
    depth = p.shape[0]
    T = B * S
    h = x.reshape(T, D)
    hn = rms_norm_bf16(h, norm_mix[0])
    for i in range(depth):
        kind, j = i % 3, i // 3
        if kind == 0:
            h = mlstm_layer(h, hn, ml_w_in, ml_gate_bias[j], ml_conv[j], ml_head_gain[j], ml_w_out, B, S, j)
        elif kind == 1:
            h = rwkv_layer(h, hn, rw_mu[j], rw_w_rkv, rw_w0[j], rw_w1[j], rw_w2[j], rw_a0[j], rw_a1[j],
                           rw_a2[j], rw_g1[j], rw_g2[j], rw_k_k[j], rw_k_a[j], rw_r_k[j], rw_ln_gain[j],
                           rw_ln_bias[j], rw_w_o, B, S, j)
        else:
            h = hgrn2_layer(h, hn, hg_w_in, hg_f_bias[j], hg_lb, i, hg_head_gain[j], hg_w_out, B, S, j)
        h = moe_layer(h, norm_ffn[i], moe_router, moe_w_gate, moe_w_up, moe_w_down, i, B, S)
        last = i + 1 == depth
        h, hn = ple_layer(h, p[i].reshape(T, -1), ple_w[i], ple_gate_down[i], ple_gate_up[i], norm_ple_gate[i],
                          norm_ple_post[i], norm_final if last else norm_mix[i + 1], F32 if last else BF16)
    return hn.reshape(B, S, D)
```

```python
import functools

import jax
import jax.numpy as jnp
from jax import lax
from jax.experimental import pallas as pl
from jax.experimental.pallas import tpu as pltpu

F32 = jnp.float32
BF16 = jnp.bfloat16

NORM_EPS = 1e-6
GATE_CAP = 15.0
ML_CHUNK = 128
RW_HEAD = 64
RW_GN_EPS = 64e-5
HG_DK = 128
HG_CHUNK = 64
EC_CAPACITY = 2

LANES = 128
VMEM_LIMIT = 52 * 1024 * 1024


def _cparams(*sem):
    return pltpu.CompilerParams(dimension_semantics=sem, vmem_limit_bytes=VMEM_LIMIT)


def _tile(n, prefs):
    for t in prefs:
        if n % t == 0:
            return t
    return n


def _rms(x, gain):
    return x * lax.rsqrt(jnp.mean(x * x, axis=-1, keepdims=True) + NORM_EPS) * gain


def _norm_kernel(h_ref, g_ref, o_ref):
    o_ref[...] = _rms(h_ref[...], g_ref[...]).astype(o_ref.dtype)


def rms_norm_bf16(h, gain):
    T, D = h.shape
    tm = _tile(T, (256, 128, 64, 32, 16, 8))
    return pl.pallas_call(
        _norm_kernel,
        grid=(T // tm,),
        in_specs=[pl.BlockSpec((tm, D), lambda i: (i, 0)), pl.BlockSpec((1, D), lambda i: (0, 0))],
        out_specs=pl.BlockSpec((tm, D), lambda i: (i, 0)),
        out_shape=jax.ShapeDtypeStruct((T, D), BF16),
        compiler_params=_cparams("parallel"),
        name="rms_norm",
    )(h, gain.reshape(1, D))


def _router_kernel(h_ref, g_ref, r_ref, l_ref):
    yh, yl = _split(_rms(h_ref[...], g_ref[...]))
    rh, rl = _split(r_ref[0])
    l_ref[...] = _bdot(yh, rh) + (_bdot(yh, rl) + _bdot(yl, rh))


def router_logits(h, gain, router, layer):
    T, D = h.shape
    E = router.shape[2]
    rp = jnp.pad(router, ((0, 0), (0, 0), (0, LANES - E)))
    tm = _tile(T, (256, 128, 64, 32, 16, 8))
    logits = pl.pallas_call(
        _router_kernel,
        grid=(T // tm,),
        in_specs=[pl.BlockSpec((tm, D), lambda i: (i, 0)), pl.BlockSpec((1, D), lambda i: (0, 0)),
                  pl.BlockSpec((1, D, LANES), lambda i: (layer, 0, 0))],
        out_specs=pl.BlockSpec((tm, LANES), lambda i: (i, 0)),
        out_shape=jax.ShapeDtypeStruct((T, LANES), F32),
        compiler_params=_cparams("parallel"),
        name="router_logits",
    )(h, gain.reshape(1, D), rp)
    return logits[:, :E]


_CAST_ROWS = 256


def _cast_weight(w_ref, wb_ref):
    k = w_ref.shape[0]
    rows = _tile(k, (_CAST_ROWS, 128, 64, 32, 16))

    def body(c, _):
        r = pl.multiple_of(c * rows, rows)
        wb_ref[pl.ds(r, rows), :] = w_ref[pl.ds(r, rows), :].astype(BF16)
        return 0

    lax.fori_loop(0, k // rows, body, 0)


def _mm_kernel(x_ref, w_ref, *rest, has_res, has_bias, act, w_is_nk):
    rest = list(rest)
    b_ref = rest.pop(0) if has_bias else None
    r_ref = rest.pop(0) if has_res else None
    o_ref, wb_ref = rest

    @pl.when(pl.program_id(1) == 0)
    def _():
        _cast_weight(w_ref.at[0], wb_ref)

    contract = (((1,), (1,)), ((), ())) if w_is_nk else (((1,), (0,)), ((), ()))
    acc = lax.dot_general(x_ref[...], wb_ref[...], contract, preferred_element_type=F32)
    if has_bias:
        acc = acc + b_ref[...]
    if act is not None:
        acc = act(acc)
    if has_res:
        acc = acc + r_ref[...]
    o_ref[...] = acc.astype(o_ref.dtype)


def matmul(x, w, layer=0, *, n=None, col_off=0, bias=None, residual=None, act=None, out_dtype=F32, tm=None, tn=None,
           w_is_nk=False):
    M, K = x.shape
    if w.ndim == 2:
        w = w[None]
    if n is None:
        n = w.shape[1 if w_is_nk else 2] - col_off
    tm = tm or _tile(M, (1024, 512, 256, 128, 64, 32, 16))
    tn = tn or _tile(n, (512, 256, 128))
    assert n % tn == 0 and col_off % tn == 0 and M % tm == 0
    cb = col_off // tn
    if w_is_nk:
        w_spec = pl.BlockSpec((1, tn, K), lambda j, i: (layer, j + cb, 0))
    else:
        w_spec = pl.BlockSpec((1, K, tn), lambda j, i: (layer, 0, j + cb))
    in_specs = [pl.BlockSpec((tm, K), lambda j, i: (i, 0)), w_spec]
    args = [x, w]
    if bias is not None:
        in_specs.append(pl.BlockSpec((1, tn), lambda j, i: (0, j)))
        args.append(bias.reshape(1, n))
    if residual is not None:
        in_specs.append(pl.BlockSpec((tm, tn), lambda j, i: (i, j)))
        args.append(residual)
    return pl.pallas_call(
        functools.partial(_mm_kernel, has_res=residual is not None, has_bias=bias is not None, act=act,
                          w_is_nk=w_is_nk),
        grid=(n // tn, M // tm),
        in_specs=in_specs,
        out_specs=pl.BlockSpec((tm, tn), lambda j, i: (i, j)),
        out_shape=jax.ShapeDtypeStruct((M, n), out_dtype),
        scratch_shapes=[pltpu.VMEM((tn, K) if w_is_nk else (K, tn), BF16)],
        compiler_params=_cparams("arbitrary", "arbitrary"),
        name="matmul",
    )(*args)


def _lora_kernel(x_ref, a_ref, b_ref, bias_ref, o_ref, ab_ref, bb_ref, *, act1, act2):
    @pl.when(pl.program_id(0) == 0)
    def _():
        _cast_weight(a_ref, ab_ref)
        _cast_weight(b_ref, bb_ref)

    t = jnp.dot(x_ref[...], ab_ref[...], preferred_element_type=F32)
    if act1 is not None:
        t = act1(t)
    y = jnp.dot(t.astype(BF16), bb_ref[...], preferred_element_type=F32) + bias_ref[...]
    if act2 is not None:
        y = act2(y)
    o_ref[...] = y.astype(o_ref.dtype)


def lora(x, a, b, bias=None, act1=None, act2=None, out_dtype=F32, tm=256):
    M, K = x.shape
    R, N = b.shape
    rp = -R % LANES
    if rp:
        a = jnp.pad(a, ((0, 0), (0, rp)))
        b = jnp.pad(b, ((0, rp), (0, 0)))
        R += rp
    if bias is None:
        bias = jnp.zeros((N,), F32)
    tm = _tile(M, (tm, 128, 64, 32, 16))
    return pl.pallas_call(
        functools.partial(_lora_kernel, act1=act1, act2=act2),
        grid=(M // tm,),
        in_specs=[pl.BlockSpec((tm, K), lambda i: (i, 0)), pl.BlockSpec((K, R), lambda i: (0, 0)),
                  pl.BlockSpec((R, N), lambda i: (0, 0)), pl.BlockSpec((1, N), lambda i: (0, 0))],
        out_specs=pl.BlockSpec((tm, N), lambda i: (i, 0)),
        out_shape=jax.ShapeDtypeStruct((M, N), out_dtype),
        scratch_shapes=[pltpu.VMEM((K, R), BF16), pltpu.VMEM((R, N), BF16)],
        compiler_params=_cparams("arbitrary"),
        name="lora",
    )(x, a, b, bias.reshape(1, N))


def _ple_kernel(h_ref, p_ref, wp_ref, gd_ref, gu_ref, gg_ref, pg_ref, ng_ref, ho_ref, no_ref,
                wpb_ref, gdb_ref, gub_ref):
    @pl.when(pl.program_id(0) == 0)
    def _():
        _cast_weight(wp_ref, wpb_ref)
        _cast_weight(gd_ref, gdb_ref)
        _cast_weight(gu_ref, gub_ref)

    h = h_ref[...]
    e = jnp.dot(p_ref[...].astype(BF16), wpb_ref[...], preferred_element_type=F32)
    e = _rms(e, pg_ref[...])
    t = jnp.dot(_rms(h, gg_ref[...]).astype(BF16), gdb_ref[...], preferred_element_type=F32)
    g = jnp.dot(t.astype(BF16), gub_ref[...], preferred_element_type=F32)
    hn = h + e * jax.nn.sigmoid(g)
    ho_ref[...] = hn
    no_ref[...] = _rms(hn, ng_ref[...]).astype(no_ref.dtype)


def ple_layer(h, p, w_ple, g_down, g_up, gate_gain, post_gain, next_gain, next_dtype):
    T, D = h.shape
    P = p.shape[1]
    tm = _tile(T, (256, 128, 64, 32, 16, 8))
    row = lambda i: (i, 0)
    fix = lambda i: (0, 0)
    return pl.pallas_call(
        _ple_kernel,
        grid=(T // tm,),
        in_specs=[pl.BlockSpec((tm, D), row), pl.BlockSpec((tm, P), row), pl.BlockSpec((P, D), fix),
                  pl.BlockSpec((D, P), fix), pl.BlockSpec((P, D), fix), pl.BlockSpec((1, D), fix),
                  pl.BlockSpec((1, D), fix), pl.BlockSpec((1, D), fix)],
        out_specs=[pl.BlockSpec((tm, D), row), pl.BlockSpec((tm, D), row)],
        out_shape=[jax.ShapeDtypeStruct((T, D), F32), jax.ShapeDtypeStruct((T, D), next_dtype)],
        scratch_shapes=[pltpu.VMEM((P, D), BF16), pltpu.VMEM((D, P), BF16), pltpu.VMEM((P, D), BF16)],
        compiler_params=_cparams("arbitrary"),
        name="ple",
    )(h, p, w_ple, g_down, g_up, gate_gain.reshape(1, D), post_gain.reshape(1, D), next_gain.reshape(1, D))


MOE_NORM_ROWS = 256
DMA_ISSUE_UNROLL = 8


def _moe_up_kernel(tok_ref, nxt_ref, h_ref, gain_ref, wg_ref, wu_ref, o_ref, xg_ref, xb_ref, wgb_ref, wub_ref, sem):
    rows = xg_ref.shape[0]
    e = pl.program_id(0)

    def gather(idx_ref):
        def issue(i, _):
            pltpu.make_async_copy(h_ref.at[pl.ds(idx_ref[0, 0, i], 1), :], xg_ref.at[pl.ds(i, 1), :], sem).start()
            return 0

        lax.fori_loop(0, rows, issue, 0, unroll=DMA_ISSUE_UNROLL)

    @pl.when(pl.program_id(1) == 0)
    def _():
        @pl.when(e == 0)
        def _():
            gather(tok_ref)

        pltpu.make_async_copy(h_ref.at[pl.ds(0, rows), :], xg_ref, sem).wait()
        step = _tile(rows, (MOE_NORM_ROWS, 128, 64, 32, 16))

        def norm(c, _):
            r = pl.multiple_of(c * step, step)
            xb_ref[pl.ds(r, step), :] = _rms(xg_ref[pl.ds(r, step), :], gain_ref[...]).astype(BF16)
            return 0

        lax.fori_loop(0, rows // step, norm, 0)

        @pl.when(e + 1 < pl.num_programs(0))
        def _():
            gather(nxt_ref)

    _cast_weight(wg_ref.at[0, 0], wgb_ref)
    _cast_weight(wu_ref.at[0, 0], wub_ref)
    x = xb_ref[...]
    g = jnp.dot(x, wgb_ref[...], preferred_element_type=F32)
    u = jnp.dot(x, wub_ref[...], preferred_element_type=F32)
    o_ref[0] = (g * jax.nn.sigmoid(g) * u).astype(o_ref.dtype)


def _moe_down_kernel(h_ref, wd_ref, s_ref, o_ref, wdb_ref):
    _cast_weight(wd_ref.at[0, 0], wdb_ref)
    o_ref[0] = jnp.dot(h_ref[0], wdb_ref[...], preferred_element_type=F32) * s_ref[0]


def moe_ffn(h, gain, tok, gsel, w_gate, w_up, w_down, layer):
    T, D = h.shape
    E, R = tok.shape
    FF = w_gate.shape[3]
    tf = _tile(FF, (256, 128))
    hid = pl.pallas_call(
        _moe_up_kernel,
        grid=(E, FF // tf),
        in_specs=[pl.BlockSpec((1, 1, R), lambda e, f: (e, 0, 0), memory_space=pltpu.SMEM),
                  pl.BlockSpec((1, 1, R), lambda e, f: (jnp.minimum(e + 1, E - 1), 0, 0), memory_space=pltpu.SMEM),
                  pl.BlockSpec(memory_space=pl.ANY),
                  pl.BlockSpec((1, D), lambda e, f: (0, 0)),
                  pl.BlockSpec((1, 1, D, tf), lambda e, f: (layer, e, 0, f)),
                  pl.BlockSpec((1, 1, D, tf), lambda e, f: (layer, e, 0, f))],
        out_specs=pl.BlockSpec((1, R, tf), lambda e, f: (e, 0, f)),
        out_shape=jax.ShapeDtypeStruct((E, R, FF), BF16),
        scratch_shapes=[pltpu.VMEM((R, D), F32), pltpu.VMEM((R, D), BF16),
                        pltpu.VMEM((D, tf), BF16), pltpu.VMEM((D, tf), BF16), pltpu.SemaphoreType.DMA(())],
        compiler_params=_cparams("arbitrary", "arbitrary"),
        name="moe_up",
    )(tok.reshape(E, 1, R), tok.reshape(E, 1, R), h, gain.reshape(1, D), w_gate, w_up)
    tn = _tile(D, (1024, 512, 256, 128))
    return pl.pallas_call(
        _moe_down_kernel,
        grid=(E, D // tn),
        in_specs=[pl.BlockSpec((1, R, FF), lambda e, j: (e, 0, 0)),
                  pl.BlockSpec((1, 1, FF, tn), lambda e, j: (layer, e, 0, j)),
                  pl.BlockSpec((1, R, 1), lambda e, j: (e, 0, 0))],
        out_specs=pl.BlockSpec((1, R, tn), lambda e, j: (e, 0, j)),
        out_shape=jax.ShapeDtypeStruct((E, R, D), F32),
        scratch_shapes=[pltpu.VMEM((FF, tn), BF16)],
        compiler_params=_cparams("parallel", "parallel"),
        name="moe_down",
    )(hid, w_down, gsel.reshape(E, R, 1))


MOE_TILE = 256
MOE_CHUNK = 256


def _moe_combine_kernel(tile_ref, chunk_ref, next_ref, flag_ref, h_ref, cur_ref, nxt_ref, tok_ref, y_ref, o_ref,
                        buf_ref, sem):
    w = pl.program_id(0)
    nw = pl.num_programs(0)
    ch = buf_ref.shape[1]
    tm = h_ref.shape[0]
    slot = w % 2

    def gather(idx_ref, s):
        def issue(j, _):
            pltpu.make_async_copy(y_ref.at[pl.ds(idx_ref[0, 0, j], 1), :], buf_ref.at[s, pl.ds(j, 1), :],
                                  sem.at[s]).start()
            return 0

        lax.fori_loop(0, ch, issue, 0, unroll=DMA_ISSUE_UNROLL)

    @pl.when(w == 0)
    def _():
        gather(cur_ref, 0)

    @pl.when(w + 1 < nw)
    def _():
        gather(nxt_ref, 1 - slot)

    pltpu.make_async_copy(y_ref.at[pl.ds(0, ch), :], buf_ref.at[slot], sem.at[slot]).wait()
    flags = flag_ref[w]

    @pl.when(flags >= 2)
    def _():
        o_ref[...] = h_ref[...]

    @pl.when(flags % 2 == 1)
    def _():
        local = tok_ref[0] - tile_ref[w] * tm
        rows = lax.broadcasted_iota(jnp.int32, (tm, ch), 0)
        onehot = jnp.where(rows == local, 1.0, 0.0).astype(BF16)
        o_ref[...] += _dot_exact_lhs(onehot, buf_ref[slot])


def moe_combine(h, y, tok):
    T, D = h.shape
    n = tok.shape[0]
    tm = _tile(T, (MOE_TILE, 128, 64, 32, 16, 8))
    ch = _tile(n, (MOE_CHUNK, 128))
    nt, nch = T // tm, n // ch
    order = jnp.argsort(tok).astype(jnp.int32)
    stok = tok[order]
    starts = jnp.searchsorted(stok, jnp.arange(nt + 1, dtype=jnp.int32) * tm).astype(jnp.int32)
    g_lo = jnp.minimum(starts[:-1] // ch, nch - 1)
    g_hi = jnp.clip((starts[1:] - 1) // ch, g_lo, nch - 1)
    cnt = g_hi - g_lo + 1
    off = jnp.cumsum(cnt)
    nw = nch + nt
    wi = jnp.arange(nw, dtype=jnp.int32)
    tile = jnp.minimum(jnp.searchsorted(off, wi, side='right'), nt - 1).astype(jnp.int32)
    first_w = (off - cnt)[tile]
    valid = wi < off[-1]
    chunk = jnp.where(valid, g_lo[tile] + wi - first_w, nch - 1).astype(jnp.int32)
    flags = (valid.astype(jnp.int32) + 2 * (valid & (wi == first_w)).astype(jnp.int32))
    nxt = jnp.concatenate([chunk[1:], chunk[-1:]])
    idx3 = order.reshape(nch, 1, ch)
    grid_spec = pltpu.PrefetchScalarGridSpec(
        num_scalar_prefetch=4,
        grid=(nw,),
        in_specs=[pl.BlockSpec((tm, D), lambda w, t, c, x, f: (t[w], 0)),
                  pl.BlockSpec((1, 1, ch), lambda w, t, c, x, f: (c[w], 0, 0), memory_space=pltpu.SMEM),
                  pl.BlockSpec((1, 1, ch), lambda w, t, c, x, f: (x[w], 0, 0), memory_space=pltpu.SMEM),
                  pl.BlockSpec((1, 1, ch), lambda w, t, c, x, f: (c[w], 0, 0)),
                  pl.BlockSpec(memory_space=pl.ANY)],
        out_specs=pl.BlockSpec((tm, D), lambda w, t, c, x, f: (t[w], 0)),
        scratch_shapes=[pltpu.VMEM((2, ch, D), F32), pltpu.SemaphoreType.DMA((2,))],
    )

    return pl.pallas_call(
        _moe_combine_kernel,
        grid_spec=grid_spec,
        out_shape=jax.ShapeDtypeStruct((T, D), F32),
        compiler_params=_cparams("arbitrary"),
        name="moe_combine",
    )(tile, chunk, nxt, flags, h, idx3, idx3, stok.reshape(nch, 1, ch), y)


def moe_layer(h, gain, router, w_gate, w_up, w_down, layer, B, S):
    T, D = h.shape
    E = router.shape[2]
    cap = EC_CAPACITY * S // E
    logits = router_logits(h, gain, router, layer)
    aff = jax.nn.softmax(logits, axis=-1).reshape(B, S, E)
    gsel, idx = lax.top_k(jnp.swapaxes(aff, 1, 2), cap)
    tok = idx + (jnp.arange(B, dtype=idx.dtype) * S)[:, None, None]
    tok = jnp.swapaxes(tok, 0, 1).reshape(E, B * cap)
    gsel = jnp.swapaxes(gsel, 0, 1).reshape(E, B * cap)
    out = moe_ffn(h, gain, tok, gsel, w_gate, w_up, w_down, layer)
    return moe_combine(h, out.reshape(-1, D), tok.reshape(-1))


def _softplus(x):
    return jnp.maximum(x, 0.0) + jnp.log(1.0 + jnp.exp(-jnp.abs(x)))


def _dwconv3_kernel(x_ref, w_ref, o_ref):
    x = x_ref[0].astype(F32)
    s = x.shape[0]
    t = lax.broadcasted_iota(jnp.int32, x.shape, 0)
    prev = jnp.where(t == 0, 0.0, pltpu.roll(x, 1, 0))
    nxt = jnp.where(t == s - 1, 0.0, pltpu.roll(x, s - 1, 0))
    o_ref[0] = (w_ref[0:1, :] * prev + w_ref[1:2, :] * x + w_ref[2:3, :] * nxt).astype(o_ref.dtype)


def dwconv3(x, w):
    B, S, _ = x.shape
    assert w.shape[0] == 3
    C = w.shape[1]
    blk = pl.BlockSpec((1, S, LANES), lambda b, j: (b, 0, j))
    return pl.pallas_call(
        _dwconv3_kernel,
        grid=(B, C // LANES),
        in_specs=[blk, pl.BlockSpec((3, LANES), lambda b, j: (0, j))],
        out_specs=blk,
        out_shape=jax.ShapeDtypeStruct((B, S, C), BF16),
        compiler_params=_cparams("parallel", "parallel"),
        name="dwconv3",
    )(x, w)


ML_EXT = LANES
ML_BLOCK_H = 8


def _mlstm_scan_kernel(q_ref, k_ref, v_ref, g_ref, gb_ref, o_ref, c_ref, m_ref, *, heads):
    hg = pl.program_id(1)
    d = pl.program_id(2)
    L = q_ref.shape[0]
    hb = c_ref.shape[0]
    dk = q_ref.shape[1] // hb
    dv = v_ref.shape[1] // hb

    @pl.when(pl.program_id(3) == 0)
    def _():
        c_ref[...] = jnp.zeros_like(c_ref)
        m_ref[...] = jnp.full_like(m_ref, -1e30)

    sgn = jnp.where(d == 0, 1, -1)
    rl = lax.broadcasted_iota(jnp.int32, (L, L), 0)
    cl = lax.broadcasted_iota(jnp.int32, (L, L), 1)
    incl = (cl - rl) * sgn <= 0
    incl_t = (rl - cl) * sgn <= 0
    eye = rl == cl

    g = g_ref[...] + gb_ref[...]
    g = GATE_CAP * jnp.tanh(g / GATE_CAP)
    lane = lax.broadcasted_iota(jnp.int32, g.shape, 1)
    ones_col = jnp.where(lax.broadcasted_iota(jnp.int32, (L, ML_EXT), 1) == 0, 1.0, 0.0).astype(BF16)

    for hh in range(hb):
        i_idx = d * 2 * heads + hg * hb + hh
        i_col = jnp.sum(jnp.where(lane == i_idx, g, 0.0), axis=1, keepdims=True)
        f_col = -_softplus(-jnp.sum(jnp.where(lane == i_idx + heads, g, 0.0), axis=1, keepdims=True))
        i_row = jnp.sum(jnp.where(eye, i_col, 0.0), axis=0, keepdims=True)
        f_row = jnp.sum(jnp.where(eye, f_col, 0.0), axis=0, keepdims=True)
        b_col = jnp.sum(jnp.where(incl, f_row, 0.0), axis=1, keepdims=True)
        b_row = jnp.sum(jnp.where(incl_t, f_col, 0.0), axis=0, keepdims=True)
        gtot = jnp.sum(f_col, axis=0, keepdims=True)
        m = m_ref[hh, 0:1, 0:1]

        dm = b_col - b_row + i_row
        inter_log = b_col + m
        m_t = jnp.maximum(inter_log, jnp.max(jnp.where(incl, dm, -jnp.inf), axis=1, keepdims=True))
        pmat = jnp.where(incl, jnp.exp(jnp.where(incl, dm - m_t, 0.0)), 0.0)
        w_inter = jnp.exp(inter_log - m_t)

        q = q_ref[:, hh * dk:(hh + 1) * dk]
        k = k_ref[:, hh * dk:(hh + 1) * dk]
        s = lax.dot_general(q, k, (((1,), (1,)), ((), ())), preferred_element_type=F32) * pmat
        vext = jnp.concatenate([v_ref[:, hh * dv:(hh + 1) * dv], ones_col], axis=1)
        c = c_ref[hh]
        tot = (jnp.dot(s.astype(BF16), vext, preferred_element_type=F32)
               + w_inter * jnp.dot(q, c.astype(BF16), preferred_element_type=F32))
        den = tot[:, dv:dv + 1]
        o_ref[0, :, hh * dv:(hh + 1) * dv] = tot[:, :dv] / jnp.maximum(jnp.abs(den), jnp.exp(-m_t))

        src = gtot - b_col + i_col
        m_new = jnp.maximum(gtot + m, jnp.max(src, axis=0, keepdims=True))
        kw = (k.astype(F32) * jnp.exp(src - m_new)).astype(BF16)
        c_ref[hh] = jnp.exp(gtot + m - m_new) * c + lax.dot_general(
            kw, vext, (((0,), (0,)), ((), ())), preferred_element_type=F32)
        m_ref[hh] = jnp.broadcast_to(m_new, (1, LANES))


def mlstm_scan(qk, z, gates, gate_bias, heads, B, S):
    T, QK2 = qk.shape
    L = ML_CHUNK
    dk = QK2 // 2 // heads
    V = (z.shape[1] - QK2) // 2
    dv = V // heads
    nc = S // L
    hb = _tile(heads, (ML_BLOCK_H, 1))
    ng = heads // hb

    def trow(b, d, c):
        return b * nc + jnp.where(d == 0, c, nc - 1 - c)

    return pl.pallas_call(
        functools.partial(_mlstm_scan_kernel, heads=heads),
        grid=(B, ng, 2, nc),
        in_specs=[pl.BlockSpec((L, hb * dk), lambda b, h, d, c: (trow(b, d, c), h)),
                  pl.BlockSpec((L, hb * dk), lambda b, h, d, c: (trow(b, d, c), ng + h)),
                  pl.BlockSpec((L, hb * dv), lambda b, h, d, c: (trow(b, d, c), QK2 // (hb * dv) + h)),
                  pl.BlockSpec((L, LANES), lambda b, h, d, c: (trow(b, d, c), 0)),
                  pl.BlockSpec((1, LANES), lambda b, h, d, c: (0, 0))],
        out_specs=pl.BlockSpec((1, L, hb * dv), lambda b, h, d, c: (d, trow(b, d, c), h)),
        out_shape=jax.ShapeDtypeStruct((2, T, V), F32),
        scratch_shapes=[pltpu.VMEM((hb, dk, dv + ML_EXT), F32), pltpu.VMEM((hb, 1, LANES), F32)],
        compiler_params=_cparams("parallel", "parallel", "arbitrary", "arbitrary"),
        name="mlstm_scan",
    )(qk, qk, z, gates, gate_bias.reshape(1, LANES))


def mlstm_layer(h, hn, w_in, gate_bias, conv_w, head_gain, w_out, B, S, lj=0):
    T, D = h.shape
    H = gate_bias.shape[0] // 4
    QK = conv_w.shape[1] // 2
    V = w_out.shape[-2]
    DK, DV = QK // H, V // H
    nmain = 2 * QK + 2 * V
    w_t = jnp.swapaxes(w_in if w_in.ndim == 3 else w_in[None], 1, 2)
    z = matmul(hn, w_t, lj, n=nmain, out_dtype=BF16, w_is_nk=True)
    pad = LANES - 4 * H
    w_gates = lax.slice(w_t, (lj, nmain, 0), (lj + 1, nmain + 4 * H, D))[0]
    gates = matmul(hn, jnp.pad(w_gates, ((0, pad), (0, 0))), w_is_nk=True)
    taps = conv_w * jnp.concatenate([jnp.full((QK,), DK ** -0.5, F32), jnp.ones((QK,), F32)])
    qk = dwconv3(z.reshape(B, S, nmain), taps).reshape(T, 2 * QK)
    hs = mlstm_scan(qk, z, gates, jnp.pad(gate_bias, (0, pad)), H, B, S)
    gated = head_out(hs, z, 2 * QK + V, head_gain, DV, jax.nn.sigmoid)
    return matmul(gated, w_out, lj, residual=h)


RW_CHUNK = 64
RW_SUB = 16
RW_BLOCK_T = 128
RW_BLOCK_H = 32
assert RW_CHUNK == RW_HEAD


def _bdot(a, b):
    return jnp.dot(a, b, preferred_element_type=F32)


def _split(x):
    hi = x.astype(BF16)
    return hi, (x - hi.astype(F32)).astype(BF16)


def _dot_exact_lhs(m, x):
    xh, xl = _split(x)
    return _bdot(m, xh) + _bdot(m, xl)


def _dot_exact_rhs(x, m):
    xh, xl = _split(x)
    return _bdot(xh, m) + _bdot(xl, m)


def _tri_inverse(a, blk, eye, nblk, mm, bd):
    d = jnp.where(blk, a, 0.0)
    off = a - d
    d2 = mm(d, bd(d))
    d4 = mm(d2, bd(d2))
    d8 = mm(d4, bd(d4))
    p = eye + d
    p = p + mm(p, bd(d2))
    p = p + mm(p, bd(d4))
    p = p + mm(p, bd(d8))
    n = mm(p, bd(off))
    q = eye + n
    pw, reach = n, 1
    while 2 * reach < nblk:
        pw = mm(pw, bd(pw))
        q = q + mm(q, bd(pw))
        reach *= 2
    return mm(q, bd(p))


def _rwkv_scan_kernel(r_ref, k_ref, v_ref, wl_ref, a_ref, kk_ref, ka_ref, y_ref, s_ref):
    d = pl.program_id(2)
    L, N = RW_CHUNK, RW_HEAD
    W = 2 * N
    tb, hw = r_ref.shape
    nch, P = tb // L, hw // W

    @pl.when(pl.program_id(3) == 0)
    def _():
        s_ref[...] = jnp.zeros_like(s_ref)

    fwd = d == 0
    sgn = jnp.where(fwd, 1, -1)
    row = lax.broadcasted_iota(jnp.int32, (L, W), 0)
    lane = lax.broadcasted_iota(jnp.int32, (L, W), 1)
    cs = lane % L
    ahead = (cs - row) * sgn
    strict = ahead < 0
    incl = ahead <= 0
    eye = jnp.where(cs == row, 1.0, 0.0)
    blk = (row // RW_SUB) == (cs // RW_SUB)
    low = lane < N
    rl = lax.broadcasted_iota(jnp.int32, (L, L), 0)
    cl = lax.broadcasted_iota(jnp.int32, (L, L), 1)
    incl_bf = jnp.where((cl - rl) * sgn <= 0, 1.0, 0.0).astype(BF16)
    br = lax.broadcasted_iota(jnp.int32, (W, W), 0) // N
    bc = lax.broadcasted_iota(jnp.int32, (W, W), 1) // N
    same = br == bc
    seg = jnp.where(same, 1.0, 0.0).astype(BF16)

    def stack(x):
        return jnp.stack([x[:, p * W:(p + 1) * W] for p in range(P)], axis=0)

    def bd(x):
        return jnp.where(same, jnp.concatenate([x, x], axis=1), 0.0).astype(BF16)

    def mm(a, b):
        return lax.dot_general(a.astype(BF16), b, (((2,), (1,)), ((0,), (0,))), preferred_element_type=F32)

    def mm_nt(a, b):
        return lax.dot_general(a.astype(BF16), b, (((2,), (2,)), ((0,), (0,))), preferred_element_type=F32)

    for ci in range(nch):
        t0 = pl.multiple_of(jnp.where(fwd, ci * L, (nch - 1 - ci) * L), L)
        rows = pl.ds(t0, L)
        r = r_ref[rows, :]
        k = k_ref[rows, :]
        v = v_ref[rows, :]
        rate = a_ref[rows, :]
        logw = -jnp.exp(-_softplus(-wl_ref[rows, :]) - 0.5)
        kk = k * kk_ref[...]
        ssq = kk * kk
        nrm2 = jnp.concatenate([_dot_exact_rhs(ssq[:, p * W:(p + 1) * W], seg) for p in range(P)], axis=1)
        kk = kk / jnp.maximum(jnp.sqrt(nrm2), 1e-12)
        kd = k * (1.0 + (rate - 1.0) * ka_ref[...])
        bv = kk * rate
        c = _dot_exact_lhs(incl_bf, logw)
        cend = jnp.sum(logw, axis=0, keepdims=True)
        einv = jnp.exp(-c)
        eend = jnp.exp(cend - c)
        at = stack(-kk * jnp.exp(c - logw))
        rt = stack(r * jnp.exp(c))
        vp = stack(v)
        ar = jnp.concatenate([at, rt], axis=1)
        g_b = mm_nt(ar, bd(stack(bv * einv)))
        g_k = mm_nt(ar, bd(stack(kd * einv)))
        aab = jnp.where(strict, g_b[:, :L], 0.0)
        rb = jnp.where(incl, g_b[:, L:], 0.0)
        aak = jnp.where(strict, g_k[:, :L], 0.0)
        rk = jnp.where(incl, g_k[:, L:], 0.0)
        tinv = _tri_inverse(aab, blk, eye, L // RW_SUB, mm, bd)
        s0 = s_ref[...]
        bd_s, bd_v = bd(s0), bd(vp)
        u = mm(tinv, bd(mm_nt(at, bd_s) + mm(aak, bd_v)))
        y = mm_nt(rt, bd_s) + mm(rb, bd(u)) + mm(rk, bd_v)
        for p in range(P):
            y_ref[0, rows, p * W:(p + 1) * W] = y[p]
        uv = jnp.concatenate([u, vp], axis=1).astype(BF16)
        bkh = jnp.concatenate([stack(bv * eend), stack(kd * eend)], axis=1).astype(BF16)
        g = lax.dot_general(uv, bkh, (((1,), (1,)), ((0,), (0,))), preferred_element_type=F32)
        s_ref[...] = s0 * stack(jnp.exp(cend)) + jnp.where(low, g[:, :N], g[:, N:])


def rwkv_scan(r, k, v, wl2, a2, k_k, k_a, B, S):
    T, D = r.shape
    tb = _tile(S, (RW_BLOCK_T, RW_CHUNK))
    hw = _tile(D, (RW_BLOCK_H * RW_HEAD, 4 * RW_HEAD, 2 * RW_HEAD))
    nt, nh = S // tb, D // hw

    def trow(b, d, c):
        return b * nt + jnp.where(d == 0, c, nt - 1 - c)

    shared = pl.BlockSpec((tb, hw), lambda b, g, d, c: (trow(b, d, c), g))
    perdir = pl.BlockSpec((tb, hw), lambda b, g, d, c: (trow(b, d, c), d * nh + g))
    par = pl.BlockSpec((1, hw), lambda b, g, d, c: (0, g))
    return pl.pallas_call(
        _rwkv_scan_kernel,
        grid=(B, nh, 2, nt),
        in_specs=[shared, shared, shared, perdir, perdir, par, par],
        out_specs=pl.BlockSpec((1, tb, hw), lambda b, g, d, c: (d, trow(b, d, c), g)),
        out_shape=jax.ShapeDtypeStruct((2, T, D), F32),
        scratch_shapes=[pltpu.VMEM((hw // (2 * RW_HEAD), RW_HEAD, 2 * RW_HEAD), F32)],
        compiler_params=_cparams("parallel", "parallel", "arbitrary", "arbitrary"),
        name="rwkv_scan",
    )(r, k, v, wl2, a2, k_k.reshape(1, D), k_a.reshape(1, D))


def _seg_sum(x, seg_bf):
    return _dot_exact_rhs(x, seg_bf)


def _rwkv_out_kernel(y_ref, r_ref, k_ref, v_ref, af_ref, ab_ref, g_ref, rk_ref, ka_ref, lg_ref, lb_ref, o_ref):
    li = lax.broadcasted_iota(jnp.int32, (LANES, LANES), 0) // RW_HEAD
    lj = lax.broadcasted_iota(jnp.int32, (LANES, LANES), 1) // RW_HEAD
    seg = jnp.where(li == lj, 1.0, 0.0).astype(BF16)
    inv_n = 1.0 / RW_HEAD
    for c in range(y_ref.shape[2] // LANES):
        cols = slice(c * LANES, (c + 1) * LANES)
        y = y_ref[0, :, cols] + y_ref[1, :, cols]
        mean = _seg_sum(y, seg) * inv_n
        yc = y - mean
        var = _seg_sum(yc * yc, seg) * inv_n
        yn = yc * lax.rsqrt(var + RW_GN_EPS) * lg_ref[:, cols] + lb_ref[:, cols]
        a_mean = 0.5 * (af_ref[:, cols] + ab_ref[:, cols])
        k_bonus = k_ref[:, cols] * (1.0 + (a_mean - 1.0) * ka_ref[:, cols])
        bonus = _seg_sum(r_ref[:, cols] * k_bonus * rk_ref[:, cols], seg) * v_ref[:, cols]
        o_ref[:, cols] = ((yn + bonus) * g_ref[:, cols]).astype(o_ref.dtype)


def rwkv_out(y, r, k, v, a2, gate, r_k, k_a, ln_gain, ln_bias):
    _, T, D = y.shape
    tm = _tile(T, (256, 128, 64, 32, 16, 8))
    w = _tile(D, (512, 256, 128))
    nw = D // w
    tile = pl.BlockSpec((tm, w), lambda i, j: (i, j))
    par = pl.BlockSpec((1, w), lambda i, j: (0, j))
    return pl.pallas_call(
        _rwkv_out_kernel,
        grid=(T // tm, nw),
        in_specs=[pl.BlockSpec((2, tm, w), lambda i, j: (0, i, j)), tile, tile, tile,
                  tile, pl.BlockSpec((tm, w), lambda i, j: (i, nw + j)), tile, par, par, par, par],
        out_specs=tile,
        out_shape=jax.ShapeDtypeStruct((T, D), BF16),
        compiler_params=_cparams("parallel", "parallel"),
        name="rwkv_out",
    )(y, r, k, v, a2, a2, gate, r_k.reshape(1, D), k_a.reshape(1, D), ln_gain.reshape(1, D), ln_bias.reshape(1, D))


def _shift_mix_kernel(x_ref, mu_ref, *o_refs):
    x = x_ref[0].astype(F32)
    s = x.shape[0]
    t = lax.broadcasted_iota(jnp.int32, x.shape, 0)
    prev = jnp.where(t == 0, 0.0, pltpu.roll(x, 1, 0))
    nxt = jnp.where(t == s - 1, 0.0, pltpu.roll(x, s - 1, 0))
    dx = 0.5 * (prev + nxt) - x
    for j, o_ref in enumerate(o_refs):
        o_ref[0] = (x + dx * mu_ref[j:j + 1, :]).astype(o_ref.dtype)


def shift_mix(xn, mu):
    B, S, D = xn.shape
    J = mu.shape[0]
    w = LANES
    blk = pl.BlockSpec((1, S, w), lambda b, j: (b, 0, j))
    return pl.pallas_call(
        _shift_mix_kernel,
        grid=(B, D // w),
        in_specs=[blk, pl.BlockSpec((J, w), lambda b, j: (0, j))],
        out_specs=[blk] * J,
        out_shape=[jax.ShapeDtypeStruct((B, S, D), BF16)] * J,
        compiler_params=_cparams("parallel", "parallel"),
        name="shift_mix",
    )(xn, mu)


def rwkv_layer(h, hn, mu, w_rkv, w0, w1, w2, a0, a1, a2, g1, g2, k_k, k_a, r_k, ln_gain, ln_bias, w_o, B, S, lj=0):
    T, D = h.shape
    N = RW_HEAD
    H = D // N
    xm = [t.reshape(T, D) for t in shift_mix(hn.reshape(B, S, D), mu)]
    w_rkv = w_rkv.reshape(-1, D, D)
    r = matmul(xm[0], w_rkv, 3 * lj)
    k = matmul(xm[1], w_rkv, 3 * lj + 1)
    v = matmul(xm[2], w_rkv, 3 * lj + 2)

    def both_dirs(lo, hi):
        z = jnp.zeros_like(hi[0])
        return (jnp.concatenate([lo[0], lo[1]], axis=1),
                jnp.concatenate([jnp.concatenate([hi[0], z], axis=1), jnp.concatenate([z, hi[1]], axis=1)], axis=0))

    wa, wb = both_dirs(w1, w2)
    wl2 = lora(xm[3], wa, wb, bias=w0.reshape(-1), act1=jnp.tanh, tm=128)
    aa, ab = both_dirs(a1, a2)
    rate2 = lora(xm[4], aa, ab, bias=a0.reshape(-1), act2=jax.nn.sigmoid, tm=128)
    gate = lora(xm[5], g1, g2, act1=jax.nn.sigmoid)
    y = rwkv_scan(r, k, v, wl2, rate2, k_k, k_a, B, S)
    out = rwkv_out(y, r, k, v, rate2, gate, r_k, k_a, ln_gain, ln_bias)
    return matmul(out, w_o, lj, residual=h)


LOG2_E = 1.4426950408889634
HG_SAFE_EXP = 60.0
HG_SUB = 16
HG_BLOCK_T = 128
HG_BLOCK_H = 8


def _hgrn_scan_kernel(q_ref, zf_ref, v_ref, fb_ref, lb_ref, o_ref, s_ref):
    d = pl.program_id(2)
    L, N, SUB = HG_CHUNK, HG_DK, HG_SUB
    tb, hw = q_ref.shape
    nch, P = tb // L, hw // N

    @pl.when(pl.program_id(3) == 0)
    def _():
        s_ref[...] = jnp.zeros_like(s_ref)

    fwd = d == 0
    sgn = jnp.where(fwd, 1, -1)
    rl = lax.broadcasted_iota(jnp.int32, (L, L), 0)
    cl = lax.broadcasted_iota(jnp.int32, (L, L), 1)
    incl_bf = jnp.where((cl - rl) * sgn <= 0, 1.0, 0.0).astype(BF16)
    blk_r, blk_c = rl // SUB, cl // SUB
    first_r = blk_r * SUB + jnp.where(fwd, 0, SUB - 1)
    sel_first = jnp.where(cl == first_r, 1.0, 0.0).astype(BF16)
    blk_before = (blk_c - blk_r) * sgn < 0
    same_blk = blk_c == blk_r
    at_or_before = (cl - rl) * sgn <= 0
    ones_nl = jnp.ones((N, L), BF16)

    def stack(x):
        return jnp.stack([x[:, p * N:(p + 1) * N] for p in range(P)], axis=0)

    def within_block(x, j):
        return jnp.concatenate(
            [jnp.broadcast_to(x[:, i * SUB + j:i * SUB + j + 1, :], (P, SUB, N)) for i in range(L // SUB)], axis=1)

    lb = lb_ref[...]

    for ci in range(nch):
        t0 = pl.multiple_of(jnp.where(fwd, ci * L, (nch - 1 - ci) * L), L)
        rows = pl.ds(t0, L)
        qv = q_ref[rows, :].astype(F32)
        q = qv * jax.nn.sigmoid(qv) * (N ** -0.5)
        sig = jax.nn.sigmoid(zf_ref[rows, :].astype(F32) + fb_ref[0])
        lf = jnp.log(lb + (1.0 - lb) * sig)
        kx = (1.0 - lb) * (1.0 - sig)
        vb = stack(v_ref[rows, :])
        b = _dot_exact_lhs(incl_bf, lf)
        ref = _dot_exact_lhs(sel_first, b - lf)
        bend = jnp.sum(lf, axis=0, keepdims=True)
        qh = stack(q * jnp.exp(b - ref)).astype(BF16)
        rows_att = []
        for i in range(L // SUB):
            ki = kx * jnp.exp(jnp.minimum(ref[i * SUB:i * SUB + 1, :] - b, 0.0))
            rows_att.append(lax.dot_general(qh[:, i * SUB:(i + 1) * SUB], stack(ki).astype(BF16),
                                            (((2,), (2,)), ((0,), (0,))), preferred_element_type=F32))
        att_off = jnp.where(blk_before, jnp.concatenate(rows_att, axis=1), 0.0)
        decay_in_block = ref - b

        def factored_diag():
            kd = stack(kx * jnp.exp(decay_in_block)).astype(BF16)
            full = lax.dot_general(qh, kd, (((2,), (2,)), ((0,), (0,))), preferred_element_type=F32)
            return jnp.where(same_blk & at_or_before, full, att_off)

        def pairwise_diag():
            att = att_off
            bs, ks, qs = stack(b * LOG2_E), stack(kx), stack(q)
            for j in range(SUB):
                pj = (qs * within_block(ks, j) * jnp.exp2(jnp.minimum(bs - within_block(bs, j), 0.0))).astype(BF16)
                tot = jnp.dot(pj.reshape(P * L, N), ones_nl, preferred_element_type=F32).reshape(P, L, L)
                att = jnp.where((cl == blk_r * SUB + j) & at_or_before, tot, att)
            return att

        att = lax.cond(jnp.max(decay_in_block) < HG_SAFE_EXP, factored_diag, pairwise_diag)
        s0 = s_ref[...]
        o = lax.dot_general(att.astype(BF16), vb, (((2,), (1,)), ((0,), (0,))), preferred_element_type=F32)
        o = o + lax.dot_general(stack(q * jnp.exp(b)).astype(BF16), s0.astype(BF16),
                                (((2,), (2,)), ((0,), (0,))), preferred_element_type=F32)
        for p in range(P):
            o_ref[0, rows, p * N:(p + 1) * N] = o[p]
        kend = stack(kx * jnp.exp(bend - b)).astype(BF16)
        s_ref[...] = s0 * stack(jnp.exp(bend)) + lax.dot_general(
            vb, kend, (((1,), (1,)), ((0,), (0,))), preferred_element_type=F32)


def hgrn_scan(z, f_bias, lb, B, S):
    T, D5 = z.shape
    D = D5 // 5
    tb = _tile(S, (HG_BLOCK_T, HG_CHUNK))
    hw = _tile(D, (HG_BLOCK_H * HG_DK, 4 * HG_DK, 2 * HG_DK, HG_DK))
    nt, nh = S // tb, D // hw

    def trow(b, d, c):
        return b * nt + jnp.where(d == 0, c, nt - 1 - c)

    return pl.pallas_call(
        _hgrn_scan_kernel,
        grid=(B, nh, 2, nt),
        in_specs=[pl.BlockSpec((tb, hw), lambda b, g, d, c: (trow(b, d, c), g)),
                  pl.BlockSpec((tb, hw), lambda b, g, d, c: (trow(b, d, c), (1 + d) * nh + g)),
                  pl.BlockSpec((tb, hw), lambda b, g, d, c: (trow(b, d, c), 3 * nh + g)),
                  pl.BlockSpec((1, 1, hw), lambda b, g, d, c: (d, 0, g)),
                  pl.BlockSpec((1, hw), lambda b, g, d, c: (0, g))],
        out_specs=pl.BlockSpec((1, tb, hw), lambda b, g, d, c: (d, trow(b, d, c), g)),
        out_shape=jax.ShapeDtypeStruct((2, T, D), F32),
        scratch_shapes=[pltpu.VMEM((hw // HG_DK, HG_DK, HG_DK), F32)],
        compiler_params=_cparams("parallel", "parallel", "arbitrary", "arbitrary"),
        name="hgrn_scan",
    )(z, z, z, f_bias.reshape(2, 1, D), lb.reshape(1, D))


def _head_out_kernel(y_ref, g_ref, gain_ref, o_ref, *, head, act):
    y = y_ref[0] + y_ref[1]
    g = g_ref[...].astype(F32)
    gain = gain_ref[...]
    for p in range(y.shape[1] // head):
        cols = slice(p * head, (p + 1) * head)
        yp = y[:, cols]
        yn = yp * lax.rsqrt(jnp.mean(yp * yp, axis=-1, keepdims=True) + NORM_EPS) * gain[:, cols]
        o_ref[:, cols] = (yn * act(g[:, cols])).astype(o_ref.dtype)


def head_out(y, z, gate_col, gain, head, act):
    _, T, D = y.shape
    tm = _tile(T, (512, 256, 128, 64, 32, 16, 8))
    w = _tile(D, (1024, 512, 256, 128))
    w = max(w, head)
    assert gate_col % w == 0
    gb = gate_col // w
    return pl.pallas_call(
        functools.partial(_head_out_kernel, head=head, act=act),
        grid=(T // tm, D // w),
        in_specs=[pl.BlockSpec((2, tm, w), lambda i, j: (0, i, j)),
                  pl.BlockSpec((tm, w), lambda i, j: (i, gb + j)),
                  pl.BlockSpec((1, w), lambda i, j: (0, j))],
        out_specs=pl.BlockSpec((tm, w), lambda i, j: (i, j)),
        out_shape=jax.ShapeDtypeStruct((T, D), BF16),
        compiler_params=_cparams("parallel", "parallel"),
        name="head_out",
    )(y, z, gain.reshape(1, D))


def hgrn2_layer(h, hn, w_in, f_bias, lb_logits, layer_idx, head_gain, w_out, B, S, lj=0):
    T, D = h.shape
    z = matmul(hn, w_in, lj, out_dtype=BF16)
    probs = jax.nn.softmax(lb_logits, axis=0)
    lb = (jnp.cumsum(probs, axis=0) - probs[0])[layer_idx]
    o = hgrn_scan(z, f_bias, lb, B, S)
    gated = head_out(o, z, 4 * D, head_gain, HG_DK, jax.nn.silu)
    return matmul(gated, w_out, lj, residual=h)


def kernel(x, p, norm_mix, norm_ffn, norm_ple_gate, norm_ple_post, norm_final, ml_w_in, ml_gate_bias, ml_conv, ml_head_gain, ml_w_out, rw_mu, rw_w_rkv, rw_w0, rw_w1, rw_w2, rw_a0, rw_a1, rw_a2, rw_g1, rw_g2, rw_k_k, rw_k_a, rw_r_k, rw_ln_gain, rw_ln_bias, rw_w_o, hg_w_in, hg_f_bias, hg_lb, hg_head_gain, hg_w_out, moe_router, moe_w_gate, moe_w_up, moe_w_down, ple_w, ple_gate_down, ple_gate_up):
    B, S, D = x.shape
    depth = p.shape[0]
    T = B * S
    h = x.reshape(T, D)
    hn = rms_norm_bf16(h, norm_mix[0])
    for i in range(depth):
        kind, j = i % 3, i // 3
        if kind == 0:
            h = mlstm_layer(h, hn, ml_w_in, ml_gate_bias[j], ml_conv[j], ml_head_gain[j], ml_w_out, B, S, j)
        elif kind == 1:
            h = rwkv_layer(h, hn, rw_mu[j], rw_w_rkv, rw_w0[j], rw_w1[j], rw_w2[j], rw_a0[j], rw_a1[j],
                           rw_a2[j], rw_g1[j], rw_g2[j], rw_k_k[j], rw_k_a[j], rw_r_k[j], rw_ln_gain[j],
                           rw_ln_bias[j], rw_w_o, B, S, j)
        else:
            h = hgrn2_layer(h, hn, hg_w_in, hg_f_bias[j], hg_lb, i, hg_head_gain[j], hg_w_out, B, S, j)
        h = moe_layer(h, norm_ffn[i], moe_router, moe_w_gate, moe_w_up, moe_w_down, i, B, S)
        last = i + 1 == depth
        h, hn = ple_layer(h, p[i].reshape(T, -1), ple_w[i], ple_gate_down[i], ple_gate_up[i], norm_ple_gate[i],
                          norm_ple_post[i], norm_final if last else norm_mix[i + 1], F32 if last else BF16)
    return hn.reshape(B, S, D)
```

```python
import functools

import jax
import jax.numpy as jnp
from jax import lax
from jax.experimental import pallas as pl
from jax.experimental.pallas import tpu as pltpu

F32 = jnp.float32
BF16 = jnp.bfloat16

NORM_EPS = 1e-6
GATE_CAP = 15.0
ML_CHUNK = 128
RW_HEAD = 64
RW_GN_EPS = 64e-5
HG_DK = 128
HG_CHUNK = 64
EC_CAPACITY = 2

LANES = 128
VMEM_LIMIT = 52 * 1024 * 1024


def _cparams(*sem):
    return pltpu.CompilerParams(dimension_semantics=sem, vmem_limit_bytes=VMEM_LIMIT)


def _tile(n, prefs):
    for t in prefs:
        if n % t == 0:
            return t
    return n


def _rms(x, gain):
    return x * lax.rsqrt(jnp.mean(x * x, axis=-1, keepdims=True) + NORM_EPS) * gain


def _norm_kernel(h_ref, g_ref, o_ref):
    o_ref[...] = _rms(h_ref[...], g_ref[...]).astype(o_ref.dtype)


def rms_norm_bf16(h, gain):
    T, D = h.shape
    tm = _tile(T, (256, 128, 64, 32, 16, 8))
    return pl.pallas_call(
        _norm_kernel,
        grid=(T // tm,),
        in_specs=[pl.BlockSpec((tm, D), lambda i: (i, 0)), pl.BlockSpec((1, D), lambda i: (0, 0))],
        out_specs=pl.BlockSpec((tm, D), lambda i: (i, 0)),
        out_shape=jax.ShapeDtypeStruct((T, D), BF16),
        compiler_params=_cparams("parallel"),
        name="rms_norm",
    )(h, gain.reshape(1, D))


def _router_kernel(h_ref, g_ref, r_ref, l_ref):
    yh, yl = _split(_rms(h_ref[...], g_ref[...]))
    rh, rl = _split(r_ref[0])
    l_ref[...] = _bdot(yh, rh) + (_bdot(yh, rl) + _bdot(yl, rh))


def router_logits(h, gain, router, layer):
    T, D = h.shape
    E = router.shape[2]
    rp = jnp.pad(router, ((0, 0), (0, 0), (0, LANES - E)))
    tm = _tile(T, (256, 128, 64, 32, 16, 8))
    logits = pl.pallas_call(
        _router_kernel,
        grid=(T // tm,),
        in_specs=[pl.BlockSpec((tm, D), lambda i: (i, 0)), pl.BlockSpec((1, D), lambda i: (0, 0)),
                  pl.BlockSpec((1, D, LANES), lambda i: (layer, 0, 0))],
        out_specs=pl.BlockSpec((tm, LANES), lambda i: (i, 0)),
        out_shape=jax.ShapeDtypeStruct((T, LANES), F32),
        compiler_params=_cparams("parallel"),
        name="router_logits",
    )(h, gain.reshape(1, D), rp)
    return logits[:, :E]


_CAST_ROWS = 256


def _cast_weight(w_ref, wb_ref):
    k = w_ref.shape[0]
    rows = _tile(k, (_CAST_ROWS, 128, 64, 32, 16))

    def body(c, _):
        r = pl.multiple_of(c * rows, rows)
        wb_ref[pl.ds(r, rows), :] = w_ref[pl.ds(r, rows), :].astype(BF16)
        return 0

    lax.fori_loop(0, k // rows, body, 0)


def _mm_kernel(x_ref, w_ref, *rest, has_res, has_bias, act, w_is_nk):
    rest = list(rest)
    b_ref = rest.pop(0) if has_bias else None
    r_ref = rest.pop(0) if has_res else None
    o_ref, wb_ref = rest

    @pl.when(pl.program_id(1) == 0)
    def _():
        _cast_weight(w_ref.at[0], wb_ref)

    contract = (((1,), (1,)), ((), ())) if w_is_nk else (((1,), (0,)), ((), ()))
    acc = lax.dot_general(x_ref[...], wb_ref[...], contract, preferred_element_type=F32)
    if has_bias:
        acc = acc + b_ref[...]
    if act is not None:
        acc = act(acc)
    if has_res:
        acc = acc + r_ref[...]
    o_ref[...] = acc.astype(o_ref.dtype)


def matmul(x, w, layer=0, *, n=None, col_off=0, bias=None, residual=None, act=None, out_dtype=F32, tm=None, tn=None,
           w_is_nk=False):
    M, K = x.shape
    if w.ndim == 2:
        w = w[None]
    if n is None:
        n = w.shape[1 if w_is_nk else 2] - col_off
    tm = tm or _tile(M, (1024, 512, 256, 128, 64, 32, 16))
    tn = tn or _tile(n, (512, 256, 128))
    assert n % tn == 0 and col_off % tn == 0 and M % tm == 0
    cb = col_off // tn
    if w_is_nk:
        w_spec = pl.BlockSpec((1, tn, K), lambda j, i: (layer, j + cb, 0))
    else:
        w_spec = pl.BlockSpec((1, K, tn), lambda j, i: (layer, 0, j + cb))
    in_specs = [pl.BlockSpec((tm, K), lambda j, i: (i, 0)), w_spec]
    args = [x, w]
    if bias is not None:
        in_specs.append(pl.BlockSpec((1, tn), lambda j, i: (0, j)))
        args.append(bias.reshape(1, n))
    if residual is not None:
        in_specs.append(pl.BlockSpec((tm, tn), lambda j, i: (i, j)))
        args.append(residual)
    return pl.pallas_call(
        functools.partial(_mm_kernel, has_res=residual is not None, has_bias=bias is not None, act=act,
                          w_is_nk=w_is_nk),
        grid=(n // tn, M // tm),
        in_specs=in_specs,
        out_specs=pl.BlockSpec((tm, tn), lambda j, i: (i, j)),
        out_shape=jax.ShapeDtypeStruct((M, n), out_dtype),
        scratch_shapes=[pltpu.VMEM((tn, K) if w_is_nk else (K, tn), BF16)],
        compiler_params=_cparams("arbitrary", "arbitrary"),
        name="matmul",
    )(*args)


def _lora_kernel(x_ref, a_ref, b_ref, bias_ref, o_ref, ab_ref, bb_ref, *, act1, act2):
    @pl.when(pl.program_id(0) == 0)
    def _():
        _cast_weight(a_ref, ab_ref)
        _cast_weight(b_ref, bb_ref)

    t = jnp.dot(x_ref[...], ab_ref[...], preferred_element_type=F32)
    if act1 is not None:
        t = act1(t)
    y = jnp.dot(t.astype(BF16), bb_ref[...], preferred_element_type=F32) + bias_ref[...]
    if act2 is not None:
        y = act2(y)
    o_ref[...] = y.astype(o_ref.dtype)


def lora(x, a, b, bias=None, act1=None, act2=None, out_dtype=F32, tm=256):
    M, K = x.shape
    R, N = b.shape
    rp = -R % LANES
    if rp:
        a = jnp.pad(a, ((0, 0), (0, rp)))
        b = jnp.pad(b, ((0, rp), (0, 0)))
        R += rp
    if bias is None:
        bias = jnp.zeros((N,), F32)
    tm = _tile(M, (tm, 128, 64, 32, 16))
    return pl.pallas_call(
        functools.partial(_lora_kernel, act1=act1, act2=act2),
        grid=(M // tm,),
        in_specs=[pl.BlockSpec((tm, K), lambda i: (i, 0)), pl.BlockSpec((K, R), lambda i: (0, 0)),
                  pl.BlockSpec((R, N), lambda i: (0, 0)), pl.BlockSpec((1, N), lambda i: (0, 0))],
        out_specs=pl.BlockSpec((tm, N), lambda i: (i, 0)),
        out_shape=jax.ShapeDtypeStruct((M, N), out_dtype),
        scratch_shapes=[pltpu.VMEM((K, R), BF16), pltpu.VMEM((R, N), BF16)],
        compiler_params=_cparams("arbitrary"),
        name="lora",
    )(x, a, b, bias.reshape(1, N))


def _ple_kernel(h_ref, p_ref, wp_ref, gd_ref, gu_ref, gg_ref, pg_ref, ng_ref, ho_ref, no_ref,
                wpb_ref, gdb_ref, gub_ref):
    @pl.when(pl.program_id(0) == 0)
    def _():
        _cast_weight(wp_ref, wpb_ref)
        _cast_weight(gd_ref, gdb_ref)
        _cast_weight(gu_ref, gub_ref)

    h = h_ref[...]
    e = jnp.dot(p_ref[...].astype(BF16), wpb_ref[...], preferred_element_type=F32)
    e = _rms(e, pg_ref[...])
    t = jnp.dot(_rms(h, gg_ref[...]).astype(BF16), gdb_ref[...], preferred_element_type=F32)
    g = jnp.dot(t.astype(BF16), gub_ref[...], preferred_element_type=F32)
    hn = h + e * jax.nn.sigmoid(g)
    ho_ref[...] = hn
    no_ref[...] = _rms(hn, ng_ref[...]).astype(no_ref.dtype)


def ple_layer(h, p, w_ple, g_down, g_up, gate_gain, post_gain, next_gain, next_dtype):
    T, D = h.shape
    P = p.shape[1]
    tm = _tile(T, (256, 128, 64, 32, 16, 8))
    row = lambda i: (i, 0)
    fix = lambda i: (0, 0)
    return pl.pallas_call(
        _ple_kernel,
        grid=(T // tm,),
        in_specs=[pl.BlockSpec((tm, D), row), pl.BlockSpec((tm, P), row), pl.BlockSpec((P, D), fix),
                  pl.BlockSpec((D, P), fix), pl.BlockSpec((P, D), fix), pl.BlockSpec((1, D), fix),
                  pl.BlockSpec((1, D), fix), pl.BlockSpec((1, D), fix)],
        out_specs=[pl.BlockSpec((tm, D), row), pl.BlockSpec((tm, D), row)],
        out_shape=[jax.ShapeDtypeStruct((T, D), F32), jax.ShapeDtypeStruct((T, D), next_dtype)],
        scratch_shapes=[pltpu.VMEM((P, D), BF16), pltpu.VMEM((D, P), BF16), pltpu.VMEM((P, D), BF16)],
        compiler_params=_cparams("arbitrary"),
        name="ple",
    )(h, p, w_ple, g_down, g_up, gate_gain.reshape(1, D), post_gain.reshape(1, D), next_gain.reshape(1, D))


MOE_NORM_ROWS = 256
DMA_ISSUE_UNROLL = 8


def _moe_up_kernel(tok_ref, nxt_ref, h_ref, gain_ref, wg_ref, wu_ref, o_ref, xg_ref, xb_ref, wgb_ref, wub_ref, sem):
    rows = xg_ref.shape[0]
    e = pl.program_id(0)

    def gather(idx_ref):
        def issue(i, _):
            pltpu.make_async_copy(h_ref.at[pl.ds(idx_ref[0, 0, i], 1), :], xg_ref.at[pl.ds(i, 1), :], sem).start()
            return 0

        lax.fori_loop(0, rows, issue, 0, unroll=DMA_ISSUE_UNROLL)

    @pl.when(pl.program_id(1) == 0)
    def _():
        @pl.when(e == 0)
        def _():
            gather(tok_ref)

        pltpu.make_async_copy(h_ref.at[pl.ds(0, rows), :], xg_ref, sem).wait()
        step = _tile(rows, (MOE_NORM_ROWS, 128, 64, 32, 16))

        def norm(c, _):
            r = pl.multiple_of(c * step, step)
            xb_ref[pl.ds(r, step), :] = _rms(xg_ref[pl.ds(r, step), :], gain_ref[...]).astype(BF16)
            return 0

        lax.fori_loop(0, rows // step, norm, 0)

        @pl.when(e + 1 < pl.num_programs(0))
        def _():
            gather(nxt_ref)

    _cast_weight(wg_ref.at[0, 0], wgb_ref)
    _cast_weight(wu_ref.at[0, 0], wub_ref)
    x = xb_ref[...]
    g = jnp.dot(x, wgb_ref[...], preferred_element_type=F32)
    u = jnp.dot(x, wub_ref[...], preferred_element_type=F32)
    o_ref[0] = (g * jax.nn.sigmoid(g) * u).astype(o_ref.dtype)


def _moe_down_kernel(h_ref, wd_ref, s_ref, o_ref, wdb_ref):
    _cast_weight(wd_ref.at[0, 0], wdb_ref)
    o_ref[0] = jnp.dot(h_ref[0], wdb_ref[...], preferred_element_type=F32) * s_ref[0]


def moe_ffn(h, gain, tok, gsel, w_gate, w_up, w_down, layer):
    T, D = h.shape
    E, R = tok.shape
    FF = w_gate.shape[3]
    tf = _tile(FF, (256, 128))
    hid = pl.pallas_call(
        _moe_up_kernel,
        grid=(E, FF // tf),
        in_specs=[pl.BlockSpec((1, 1, R), lambda e, f: (e, 0, 0), memory_space=pltpu.SMEM),
                  pl.BlockSpec((1, 1, R), lambda e, f: (jnp.minimum(e + 1, E - 1), 0, 0), memory_space=pltpu.SMEM),
                  pl.BlockSpec(memory_space=pl.ANY),
                  pl.BlockSpec((1, D), lambda e, f: (0, 0)),
                  pl.BlockSpec((1, 1, D, tf), lambda e, f: (layer, e, 0, f)),
                  pl.BlockSpec((1, 1, D, tf), lambda e, f: (layer, e, 0, f))],
        out_specs=pl.BlockSpec((1, R, tf), lambda e, f: (e, 0, f)),
        out_shape=jax.ShapeDtypeStruct((E, R, FF), BF16),
        scratch_shapes=[pltpu.VMEM((R, D), F32), pltpu.VMEM((R, D), BF16),
                        pltpu.VMEM((D, tf), BF16), pltpu.VMEM((D, tf), BF16), pltpu.SemaphoreType.DMA(())],
        compiler_params=_cparams("arbitrary", "arbitrary"),
        name="moe_up",
    )(tok.reshape(E, 1, R), tok.reshape(E, 1, R), h, gain.reshape(1, D), w_gate, w_up)
    tn = _tile(D, (1024, 512, 256, 128))
    return pl.pallas_call(
        _moe_down_kernel,
        grid=(E, D // tn),
        in_specs=[pl.BlockSpec((1, R, FF), lambda e, j: (e, 0, 0)),
                  pl.BlockSpec((1, 1, FF, tn), lambda e, j: (layer, e, 0, j)),
                  pl.BlockSpec((1, R, 1), lambda e, j: (e, 0, 0))],
        out_specs=pl.BlockSpec((1, R, tn), lambda e, j: (e, 0, j)),
        out_shape=jax.ShapeDtypeStruct((E, R, D), F32),
        scratch_shapes=[pltpu.VMEM((FF, tn), BF16)],
        compiler_params=_cparams("parallel", "parallel"),
        name="moe_down",
    )(hid, w_down, gsel.reshape(E, R, 1))


MOE_TILE = 256
MOE_CHUNK = 256


def _moe_combine_kernel(tile_ref, chunk_ref, next_ref, flag_ref, h_ref, cur_ref, nxt_ref, tok_ref, y_ref, o_ref,
                        buf_ref, sem):
    w = pl.program_id(0)
    nw = pl.num_programs(0)
    ch = buf_ref.shape[1]
    tm = h_ref.shape[0]
    slot = w % 2

    def gather(idx_ref, s):
        def issue(j, _):
            pltpu.make_async_copy(y_ref.at[pl.ds(idx_ref[0, 0, j], 1), :], buf_ref.at[s, pl.ds(j, 1), :],
                                  sem.at[s]).start()
            return 0

        lax.fori_loop(0, ch, issue, 0, unroll=DMA_ISSUE_UNROLL)

    @pl.when(w == 0)
    def _():
        gather(cur_ref, 0)

    @pl.when(w + 1 < nw)
    def _():
        gather(nxt_ref, 1 - slot)

    pltpu.make_async_copy(y_ref.at[pl.ds(0, ch), :], buf_ref.at[slot], sem.at[slot]).wait()
    flags = flag_ref[w]

    @pl.when(flags >= 2)
    def _():
        o_ref[...] = h_ref[...]

    @pl.when(flags % 2 == 1)
    def _():
        local = tok_ref[0] - tile_ref[w] * tm
        rows = lax.broadcasted_iota(jnp.int32, (tm, ch), 0)
        onehot = jnp.where(rows == local, 1.0, 0.0).astype(BF16)
        o_ref[...] += _dot_exact_lhs(onehot, buf_ref[slot])


def moe_combine(h, y, tok):
    T, D = h.shape
    n = tok.shape[0]
    tm = _tile(T, (MOE_TILE, 128, 64, 32, 16, 8))
    ch = _tile(n, (MOE_CHUNK, 128))
    nt, nch = T // tm, n // ch
    order = jnp.argsort(tok).astype(jnp.int32)
    stok = tok[order]
    edges = jnp.arange(nt + 1, dtype=jnp.int32) * tm
    starts = jnp.sum((tok[None, :] < edges[:, None]).astype(jnp.int32), axis=1)
    g_lo = jnp.minimum(starts[:-1] // ch, nch - 1)
    g_hi = jnp.clip((starts[1:] - 1) // ch, g_lo, nch - 1)
    cnt = g_hi - g_lo + 1
    off = jnp.cumsum(cnt)
    nw = nch + nt
    wi = jnp.arange(nw, dtype=jnp.int32)
    tile = jnp.minimum(jnp.sum((off[None, :] <= wi[:, None]).astype(jnp.int32), axis=1), nt - 1)
    first_w = (off - cnt)[tile]
    valid = wi < off[-1]
    chunk = jnp.where(valid, g_lo[tile] + wi - first_w, nch - 1).astype(jnp.int32)
    flags = (valid.astype(jnp.int32) + 2 * (valid & (wi == first_w)).astype(jnp.int32))
    nxt = jnp.concatenate([chunk[1:], chunk[-1:]])
    idx3 = order.reshape(nch, 1, ch)
    grid_spec = pltpu.PrefetchScalarGridSpec(
        num_scalar_prefetch=4,
        grid=(nw,),
        in_specs=[pl.BlockSpec((tm, D), lambda w, t, c, x, f: (t[w], 0)),
                  pl.BlockSpec((1, 1, ch), lambda w, t, c, x, f: (c[w], 0, 0), memory_space=pltpu.SMEM),
                  pl.BlockSpec((1, 1, ch), lambda w, t, c, x, f: (x[w], 0, 0), memory_space=pltpu.SMEM),
                  pl.BlockSpec((1, 1, ch), lambda w, t, c, x, f: (c[w], 0, 0)),
                  pl.BlockSpec(memory_space=pl.ANY)],
        out_specs=pl.BlockSpec((tm, D), lambda w, t, c, x, f: (t[w], 0)),
        scratch_shapes=[pltpu.VMEM((2, ch, D), F32), pltpu.SemaphoreType.DMA((2,))],
    )

    return pl.pallas_call(
        _moe_combine_kernel,
        grid_spec=grid_spec,
        out_shape=jax.ShapeDtypeStruct((T, D), F32),
        compiler_params=_cparams("arbitrary"),
        name="moe_combine",
    )(tile, chunk, nxt, flags, h, idx3, idx3, stok.reshape(nch, 1, ch), y)


def moe_layer(h, gain, router, w_gate, w_up, w_down, layer, B, S):
    T, D = h.shape
    E = router.shape[2]
    cap = EC_CAPACITY * S // E
    logits = router_logits(h, gain, router, layer)
    aff = jax.nn.softmax(logits, axis=-1).reshape(B, S, E)
    gsel, idx = lax.top_k(jnp.swapaxes(aff, 1, 2), cap)
    tok = idx + (jnp.arange(B, dtype=idx.dtype) * S)[:, None, None]
    tok = jnp.swapaxes(tok, 0, 1).reshape(E, B * cap)
    gsel = jnp.swapaxes(gsel, 0, 1).reshape(E, B * cap)
    out = moe_ffn(h, gain, tok, gsel, w_gate, w_up, w_down, layer)
    return moe_combine(h, out.reshape(-1, D), tok.reshape(-1))


def _softplus(x):
    return jnp.maximum(x, 0.0) + jnp.log(1.0 + jnp.exp(-jnp.abs(x)))


def _dwconv3_kernel(x_ref, w_ref, o_ref):
    x = x_ref[0].astype(F32)
    s = x.shape[0]
    t = lax.broadcasted_iota(jnp.int32, x.shape, 0)
    prev = jnp.where(t == 0, 0.0, pltpu.roll(x, 1, 0))
    nxt = jnp.where(t == s - 1, 0.0, pltpu.roll(x, s - 1, 0))
    o_ref[0] = (w_ref[0:1, :] * prev + w_ref[1:2, :] * x + w_ref[2:3, :] * nxt).astype(o_ref.dtype)


def dwconv3(x, w):
    B, S, _ = x.shape
    assert w.shape[0] == 3
    C = w.shape[1]
    blk = pl.BlockSpec((1, S, LANES), lambda b, j: (b, 0, j))
    return pl.pallas_call(
        _dwconv3_kernel,
        grid=(B, C // LANES),
        in_specs=[blk, pl.BlockSpec((3, LANES), lambda b, j: (0, j))],
        out_specs=blk,
        out_shape=jax.ShapeDtypeStruct((B, S, C), BF16),
        compiler_params=_cparams("parallel", "parallel"),
        name="dwconv3",
    )(x, w)


ML_EXT = LANES
ML_BLOCK_H = 8


def _mlstm_scan_kernel(q_ref, k_ref, v_ref, g_ref, gb_ref, o_ref, c_ref, m_ref, *, heads):
    hg = pl.program_id(1)
    d = pl.program_id(2)
    L = q_ref.shape[0]
    hb = c_ref.shape[0]
    dk = q_ref.shape[1] // hb
    dv = v_ref.shape[1] // hb

    @pl.when(pl.program_id(3) == 0)
    def _():
        c_ref[...] = jnp.zeros_like(c_ref)
        m_ref[...] = jnp.full_like(m_ref, -1e30)

    sgn = jnp.where(d == 0, 1, -1)
    rl = lax.broadcasted_iota(jnp.int32, (L, L), 0)
    cl = lax.broadcasted_iota(jnp.int32, (L, L), 1)
    incl = (cl - rl) * sgn <= 0
    incl_t = (rl - cl) * sgn <= 0
    eye = rl == cl

    g = g_ref[...] + gb_ref[...]
    g = GATE_CAP * jnp.tanh(g / GATE_CAP)
    lane = lax.broadcasted_iota(jnp.int32, g.shape, 1)
    ones_col = jnp.where(lax.broadcasted_iota(jnp.int32, (L, ML_EXT), 1) == 0, 1.0, 0.0).astype(BF16)

    for hh in range(hb):
        i_idx = d * 2 * heads + hg * hb + hh
        i_col = jnp.sum(jnp.where(lane == i_idx, g, 0.0), axis=1, keepdims=True)
        f_col = -_softplus(-jnp.sum(jnp.where(lane == i_idx + heads, g, 0.0), axis=1, keepdims=True))
        i_row = jnp.sum(jnp.where(eye, i_col, 0.0), axis=0, keepdims=True)
        f_row = jnp.sum(jnp.where(eye, f_col, 0.0), axis=0, keepdims=True)
        b_col = jnp.sum(jnp.where(incl, f_row, 0.0), axis=1, keepdims=True)
        b_row = jnp.sum(jnp.where(incl_t, f_col, 0.0), axis=0, keepdims=True)
        gtot = jnp.sum(f_col, axis=0, keepdims=True)
        m = m_ref[hh, 0:1, 0:1]

        dm = b_col - b_row + i_row
        inter_log = b_col + m
        m_t = jnp.maximum(inter_log, jnp.max(jnp.where(incl, dm, -jnp.inf), axis=1, keepdims=True))
        pmat = jnp.where(incl, jnp.exp(jnp.where(incl, dm - m_t, 0.0)), 0.0)
        w_inter = jnp.exp(inter_log - m_t)

        q = q_ref[:, hh * dk:(hh + 1) * dk]
        k = k_ref[:, hh * dk:(hh + 1) * dk]
        s = lax.dot_general(q, k, (((1,), (1,)), ((), ())), preferred_element_type=F32) * pmat
        vext = jnp.concatenate([v_ref[:, hh * dv:(hh + 1) * dv], ones_col], axis=1)
        c = c_ref[hh]
        tot = (jnp.dot(s.astype(BF16), vext, preferred_element_type=F32)
               + w_inter * jnp.dot(q, c.astype(BF16), preferred_element_type=F32))
        den = tot[:, dv:dv + 1]
        o_ref[0, :, hh * dv:(hh + 1) * dv] = tot[:, :dv] / jnp.maximum(jnp.abs(den), jnp.exp(-m_t))

        src = gtot - b_col + i_col
        m_new = jnp.maximum(gtot + m, jnp.max(src, axis=0, keepdims=True))
        kw = (k.astype(F32) * jnp.exp(src - m_new)).astype(BF16)
        c_ref[hh] = jnp.exp(gtot + m - m_new) * c + lax.dot_general(
            kw, vext, (((0,), (0,)), ((), ())), preferred_element_type=F32)
        m_ref[hh] = jnp.broadcast_to(m_new, (1, LANES))


def mlstm_scan(qk, z, gates, gate_bias, heads, B, S):
    T, QK2 = qk.shape
    L = ML_CHUNK
    dk = QK2 // 2 // heads
    V = (z.shape[1] - QK2) // 2
    dv = V // heads
    nc = S // L
    hb = _tile(heads, (ML_BLOCK_H, 1))
    ng = heads // hb

    def trow(b, d, c):
        return b * nc + jnp.where(d == 0, c, nc - 1 - c)

    return pl.pallas_call(
        functools.partial(_mlstm_scan_kernel, heads=heads),
        grid=(B, ng, 2, nc),
        in_specs=[pl.BlockSpec((L, hb * dk), lambda b, h, d, c: (trow(b, d, c), h)),
                  pl.BlockSpec((L, hb * dk), lambda b, h, d, c: (trow(b, d, c), ng + h)),
                  pl.BlockSpec((L, hb * dv), lambda b, h, d, c: (trow(b, d, c), QK2 // (hb * dv) + h)),
                  pl.BlockSpec((L, LANES), lambda b, h, d, c: (trow(b, d, c), 0)),
                  pl.BlockSpec((1, LANES), lambda b, h, d, c: (0, 0))],
        out_specs=pl.BlockSpec((1, L, hb * dv), lambda b, h, d, c: (d, trow(b, d, c), h)),
        out_shape=jax.ShapeDtypeStruct((2, T, V), F32),
        scratch_shapes=[pltpu.VMEM((hb, dk, dv + ML_EXT), F32), pltpu.VMEM((hb, 1, LANES), F32)],
        compiler_params=_cparams("parallel", "parallel", "arbitrary", "arbitrary"),
        name="mlstm_scan",
    )(qk, qk, z, gates, gate_bias.reshape(1, LANES))


def mlstm_layer(h, hn, w_in, gate_bias, conv_w, head_gain, w_out, B, S, lj=0):
    T, D = h.shape
    H = gate_bias.shape[0] // 4
    QK = conv_w.shape[1] // 2
    V = w_out.shape[-2]
    DK, DV = QK // H, V // H
    nmain = 2 * QK + 2 * V
    w_t = jnp.swapaxes(w_in if w_in.ndim == 3 else w_in[None], 1, 2)
    z = matmul(hn, w_t, lj, n=nmain, out_dtype=BF16, w_is_nk=True)
    pad = LANES - 4 * H
    w_gates = lax.slice(w_t, (lj, nmain, 0), (lj + 1, nmain + 4 * H, D))[0]
    gates = matmul(hn, jnp.pad(w_gates, ((0, pad), (0, 0))), w_is_nk=True)
    taps = conv_w * jnp.concatenate([jnp.full((QK,), DK ** -0.5, F32), jnp.ones((QK,), F32)])
    qk = dwconv3(z.reshape(B, S, nmain), taps).reshape(T, 2 * QK)
    hs = mlstm_scan(qk, z, gates, jnp.pad(gate_bias, (0, pad)), H, B, S)
    gated = head_out(hs, z, 2 * QK + V, head_gain, DV, jax.nn.sigmoid)
    return matmul(gated, w_out, lj, residual=h)


RW_CHUNK = 64
RW_SUB = 16
RW_BLOCK_T = 128
RW_BLOCK_H = 32
assert RW_CHUNK == RW_HEAD


def _bdot(a, b):
    return jnp.dot(a, b, preferred_element_type=F32)


def _split(x):
    hi = x.astype(BF16)
    return hi, (x - hi.astype(F32)).astype(BF16)


def _dot_exact_lhs(m, x):
    xh, xl = _split(x)
    return _bdot(m, xh) + _bdot(m, xl)


def _dot_exact_rhs(x, m):
    xh, xl = _split(x)
    return _bdot(xh, m) + _bdot(xl, m)


def _tri_inverse(a, blk, eye, nblk, mm, bd):
    d = jnp.where(blk, a, 0.0)
    off = a - d
    rows = a.shape[1]
    p = eye + d
    pw = mm(d, bd(d))
    for _ in range(RW_SUB.bit_length() - 3):
        both = mm(jnp.concatenate([pw, p], axis=1), bd(pw))
        pw, p = both[:, :rows], p + both[:, rows:]
    p = p + mm(p, bd(pw))
    n = mm(p, bd(off))
    q = eye + n
    pw, reach = n, 1
    while 2 * reach < nblk:
        pw = mm(pw, bd(pw))
        q = q + mm(q, bd(pw))
        reach *= 2
    return mm(q, bd(p))


def _rwkv_scan_kernel(r_ref, k_ref, v_ref, wl_ref, a_ref, kk_ref, ka_ref, y_ref, s_ref):
    d = pl.program_id(2)
    L, N = RW_CHUNK, RW_HEAD
    W = 2 * N
    tb, hw = r_ref.shape
    nch, P = tb // L, hw // W

    @pl.when(pl.program_id(3) == 0)
    def _():
        s_ref[...] = jnp.zeros_like(s_ref)

    fwd = d == 0
    sgn = jnp.where(fwd, 1, -1)
    row = lax.broadcasted_iota(jnp.int32, (L, W), 0)
    lane = lax.broadcasted_iota(jnp.int32, (L, W), 1)
    cs = lane % L
    ahead = (cs - row) * sgn
    strict = ahead < 0
    incl = ahead <= 0
    eye = jnp.where(cs == row, 1.0, 0.0)
    blk = (row // RW_SUB) == (cs // RW_SUB)
    low = lane < N
    rl = lax.broadcasted_iota(jnp.int32, (L, L), 0)
    cl = lax.broadcasted_iota(jnp.int32, (L, L), 1)
    incl_bf = jnp.where((cl - rl) * sgn <= 0, 1.0, 0.0).astype(BF16)
    br = lax.broadcasted_iota(jnp.int32, (W, W), 0) // N
    bc = lax.broadcasted_iota(jnp.int32, (W, W), 1) // N
    same = br == bc
    seg = jnp.where(same, 1.0, 0.0).astype(BF16)

    def stack(x):
        return jnp.stack([x[:, p * W:(p + 1) * W] for p in range(P)], axis=0)

    def bd(x):
        return jnp.where(same, jnp.concatenate([x, x], axis=1), 0.0).astype(BF16)

    def mm(a, b):
        return lax.dot_general(a.astype(BF16), b, (((2,), (1,)), ((0,), (0,))), preferred_element_type=F32)

    def mm_nt(a, b):
        return lax.dot_general(a.astype(BF16), b, (((2,), (2,)), ((0,), (0,))), preferred_element_type=F32)

    for ci in range(nch):
        t0 = pl.multiple_of(jnp.where(fwd, ci * L, (nch - 1 - ci) * L), L)
        rows = pl.ds(t0, L)
        r = r_ref[rows, :]
        k = k_ref[rows, :]
        v = v_ref[rows, :]
        rate = a_ref[rows, :]
        logw = -jnp.exp(-_softplus(-wl_ref[rows, :]) - 0.5)
        kk = k * kk_ref[...]
        ssq = kk * kk
        nrm2 = jnp.concatenate([_bdot(ssq[:, p * W:(p + 1) * W].astype(BF16), seg) for p in range(P)], axis=1)
        kk = kk / jnp.maximum(jnp.sqrt(nrm2), 1e-12)
        kd = k * (1.0 + (rate - 1.0) * ka_ref[...])
        bv = kk * rate
        c = _dot_exact_lhs(incl_bf, logw)
        cend = jnp.sum(logw, axis=0, keepdims=True)
        einv = jnp.exp(-c)
        eend = jnp.exp(cend - c)
        at = stack(-kk * jnp.exp(c - logw))
        rt = stack(r * jnp.exp(c))
        vp = stack(v)
        ar = jnp.concatenate([at, rt], axis=1)
        g_b = mm_nt(ar, bd(stack(bv * einv)))
        g_k = mm_nt(ar, bd(stack(kd * einv)))
        aab = jnp.where(strict, g_b[:, :L], 0.0)
        rb = jnp.where(incl, g_b[:, L:], 0.0)
        aak = jnp.where(strict, g_k[:, :L], 0.0)
        rk = jnp.where(incl, g_k[:, L:], 0.0)
        tinv = _tri_inverse(aab, blk, eye, L // RW_SUB, mm, bd)
        s0 = s_ref[...]
        bd_v = bd(vp)
        from_state = mm_nt(ar, bd(s0))
        u = mm(tinv, bd(from_state[:, :L] + mm(aak, bd_v)))
        y = from_state[:, L:] + mm(rb, bd(u)) + mm(rk, bd_v)
        for p in range(P):
            y_ref[0, rows, p * W:(p + 1) * W] = y[p]
        uv = jnp.concatenate([u, vp], axis=1).astype(BF16)
        bkh = jnp.concatenate([stack(bv * eend), stack(kd * eend)], axis=1).astype(BF16)
        g = lax.dot_general(uv, bkh, (((1,), (1,)), ((0,), (0,))), preferred_element_type=F32)
        s_ref[...] = s0 * stack(jnp.exp(cend)) + jnp.where(low, g[:, :N], g[:, N:])


def rwkv_scan(r, k, v, wl2, a2, k_k, k_a, B, S):
    T, D = r.shape
    tb = _tile(S, (RW_BLOCK_T, RW_CHUNK))
    hw = _tile(D, (RW_BLOCK_H * RW_HEAD, 4 * RW_HEAD, 2 * RW_HEAD))
    nt, nh = S // tb, D // hw

    def trow(b, d, c):
        return b * nt + jnp.where(d == 0, c, nt - 1 - c)

    shared = pl.BlockSpec((tb, hw), lambda b, g, d, c: (trow(b, d, c), g))
    perdir = pl.BlockSpec((tb, hw), lambda b, g, d, c: (trow(b, d, c), d * nh + g))
    par = pl.BlockSpec((1, hw), lambda b, g, d, c: (0, g))
    return pl.pallas_call(
        _rwkv_scan_kernel,
        grid=(B, nh, 2, nt),
        in_specs=[shared, shared, shared, perdir, perdir, par, par],
        out_specs=pl.BlockSpec((1, tb, hw), lambda b, g, d, c: (d, trow(b, d, c), g)),
        out_shape=jax.ShapeDtypeStruct((2, T, D), F32),
        scratch_shapes=[pltpu.VMEM((hw // (2 * RW_HEAD), RW_HEAD, 2 * RW_HEAD), F32)],
        compiler_params=_cparams("parallel", "parallel", "arbitrary", "arbitrary"),
        name="rwkv_scan",
    )(r, k, v, wl2, a2, k_k.reshape(1, D), k_a.reshape(1, D))


def _seg_sum(x, seg_bf):
    return _dot_exact_rhs(x, seg_bf)


def _rwkv_out_kernel(y_ref, r_ref, k_ref, v_ref, af_ref, ab_ref, g_ref, rk_ref, ka_ref, lg_ref, lb_ref, o_ref):
    li = lax.broadcasted_iota(jnp.int32, (LANES, LANES), 0) // RW_HEAD
    lj = lax.broadcasted_iota(jnp.int32, (LANES, LANES), 1) // RW_HEAD
    seg = jnp.where(li == lj, 1.0, 0.0).astype(BF16)
    inv_n = 1.0 / RW_HEAD
    for c in range(y_ref.shape[2] // LANES):
        cols = slice(c * LANES, (c + 1) * LANES)
        y = y_ref[0, :, cols] + y_ref[1, :, cols]
        mean = _seg_sum(y, seg) * inv_n
        yc = y - mean
        var = _seg_sum(yc * yc, seg) * inv_n
        yn = yc * lax.rsqrt(var + RW_GN_EPS) * lg_ref[:, cols] + lb_ref[:, cols]
        a_mean = 0.5 * (af_ref[:, cols] + ab_ref[:, cols])
        k_bonus = k_ref[:, cols] * (1.0 + (a_mean - 1.0) * ka_ref[:, cols])
        bonus = _seg_sum(r_ref[:, cols] * k_bonus * rk_ref[:, cols], seg) * v_ref[:, cols]
        o_ref[:, cols] = ((yn + bonus) * g_ref[:, cols]).astype(o_ref.dtype)


def rwkv_out(y, r, k, v, a2, gate, r_k, k_a, ln_gain, ln_bias):
    _, T, D = y.shape
    tm = _tile(T, (256, 128, 64, 32, 16, 8))
    w = _tile(D, (512, 256, 128))
    nw = D // w
    tile = pl.BlockSpec((tm, w), lambda i, j: (i, j))
    par = pl.BlockSpec((1, w), lambda i, j: (0, j))
    return pl.pallas_call(
        _rwkv_out_kernel,
        grid=(T // tm, nw),
        in_specs=[pl.BlockSpec((2, tm, w), lambda i, j: (0, i, j)), tile, tile, tile,
                  tile, pl.BlockSpec((tm, w), lambda i, j: (i, nw + j)), tile, par, par, par, par],
        out_specs=tile,
        out_shape=jax.ShapeDtypeStruct((T, D), BF16),
        compiler_params=_cparams("parallel", "parallel"),
        name="rwkv_out",
    )(y, r, k, v, a2, a2, gate, r_k.reshape(1, D), k_a.reshape(1, D), ln_gain.reshape(1, D), ln_bias.reshape(1, D))


def _shift_mix_kernel(x_ref, mu_ref, *o_refs):
    x = x_ref[0].astype(F32)
    s = x.shape[0]
    t = lax.broadcasted_iota(jnp.int32, x.shape, 0)
    prev = jnp.where(t == 0, 0.0, pltpu.roll(x, 1, 0))
    nxt = jnp.where(t == s - 1, 0.0, pltpu.roll(x, s - 1, 0))
    dx = 0.5 * (prev + nxt) - x
    for j, o_ref in enumerate(o_refs):
        o_ref[0] = (x + dx * mu_ref[j:j + 1, :]).astype(o_ref.dtype)


def shift_mix(xn, mu):
    B, S, D = xn.shape
    J = mu.shape[0]
    w = LANES
    blk = pl.BlockSpec((1, S, w), lambda b, j: (b, 0, j))
    return pl.pallas_call(
        _shift_mix_kernel,
        grid=(B, D // w),
        in_specs=[blk, pl.BlockSpec((J, w), lambda b, j: (0, j))],
        out_specs=[blk] * J,
        out_shape=[jax.ShapeDtypeStruct((B, S, D), BF16)] * J,
        compiler_params=_cparams("parallel", "parallel"),
        name="shift_mix",
    )(xn, mu)


def rwkv_layer(h, hn, mu, w_rkv, w0, w1, w2, a0, a1, a2, g1, g2, k_k, k_a, r_k, ln_gain, ln_bias, w_o, B, S, lj=0):
    T, D = h.shape
    N = RW_HEAD
    H = D // N
    xm = [t.reshape(T, D) for t in shift_mix(hn.reshape(B, S, D), mu)]
    w_rkv = w_rkv.reshape(-1, D, D)
    r = matmul(xm[0], w_rkv, 3 * lj)
    k = matmul(xm[1], w_rkv, 3 * lj + 1)
    v = matmul(xm[2], w_rkv, 3 * lj + 2)

    def both_dirs(lo, hi):
        z = jnp.zeros_like(hi[0])
        return (jnp.concatenate([lo[0], lo[1]], axis=1),
                jnp.concatenate([jnp.concatenate([hi[0], z], axis=1), jnp.concatenate([z, hi[1]], axis=1)], axis=0))

    wa, wb = both_dirs(w1, w2)
    wl2 = lora(xm[3], wa, wb, bias=w0.reshape(-1), act1=jnp.tanh, tm=128)
    aa, ab = both_dirs(a1, a2)
    rate2 = lora(xm[4], aa, ab, bias=a0.reshape(-1), act2=jax.nn.sigmoid, tm=128)
    gate = lora(xm[5], g1, g2, act1=jax.nn.sigmoid)
    y = rwkv_scan(r, k, v, wl2, rate2, k_k, k_a, B, S)
    out = rwkv_out(y, r, k, v, rate2, gate, r_k, k_a, ln_gain, ln_bias)
    return matmul(out, w_o, lj, residual=h)


LOG2_E = 1.4426950408889634
HG_SAFE_EXP = 60.0
HG_SUB = 16
HG_BLOCK_T = 128
HG_BLOCK_H = 8


def _hgrn_scan_kernel(q_ref, zf_ref, v_ref, fb_ref, lb_ref, o_ref, s_ref):
    d = pl.program_id(2)
    L, N, SUB = HG_CHUNK, HG_DK, HG_SUB
    tb, hw = q_ref.shape
    nch, P = tb // L, hw // N

    @pl.when(pl.program_id(3) == 0)
    def _():
        s_ref[...] = jnp.zeros_like(s_ref)

    fwd = d == 0
    sgn = jnp.where(fwd, 1, -1)
    rl = lax.broadcasted_iota(jnp.int32, (L, L), 0)
    cl = lax.broadcasted_iota(jnp.int32, (L, L), 1)
    incl_bf = jnp.where((cl - rl) * sgn <= 0, 1.0, 0.0).astype(BF16)
    blk_r, blk_c = rl // SUB, cl // SUB
    first_r = blk_r * SUB + jnp.where(fwd, 0, SUB - 1)
    sel_first = jnp.where(cl == first_r, 1.0, 0.0).astype(BF16)
    blk_before = (blk_c - blk_r) * sgn < 0
    same_blk = blk_c == blk_r
    at_or_before = (cl - rl) * sgn <= 0
    ones_nl = jnp.ones((N, L), BF16)

    def stack(x):
        return jnp.stack([x[:, p * N:(p + 1) * N] for p in range(P)], axis=0)

    def within_block(x, j):
        return jnp.concatenate(
            [jnp.broadcast_to(x[:, i * SUB + j:i * SUB + j + 1, :], (P, SUB, N)) for i in range(L // SUB)], axis=1)

    lb = lb_ref[...]

    for ci in range(nch):
        t0 = pl.multiple_of(jnp.where(fwd, ci * L, (nch - 1 - ci) * L), L)
        rows = pl.ds(t0, L)
        qv = q_ref[rows, :].astype(F32)
        q = qv * jax.nn.sigmoid(qv) * (N ** -0.5)
        sig = jax.nn.sigmoid(zf_ref[rows, :].astype(F32) + fb_ref[0])
        lf = jnp.log(lb + (1.0 - lb) * sig)
        kx = (1.0 - lb) * (1.0 - sig)
        vb = stack(v_ref[rows, :])
        b = _dot_exact_lhs(incl_bf, lf)
        ref = _dot_exact_lhs(sel_first, b - lf)
        bend = jnp.sum(lf, axis=0, keepdims=True)
        qh = stack(q * jnp.exp(b - ref)).astype(BF16)
        rows_att = []
        for i in range(L // SUB):
            ki = kx * jnp.exp(jnp.minimum(ref[i * SUB:i * SUB + 1, :] - b, 0.0))
            rows_att.append(lax.dot_general(qh[:, i * SUB:(i + 1) * SUB], stack(ki).astype(BF16),
                                            (((2,), (2,)), ((0,), (0,))), preferred_element_type=F32))
        att_off = jnp.where(blk_before, jnp.concatenate(rows_att, axis=1), 0.0)
        decay_in_block = ref - b

        def factored_diag():
            kd = stack(kx * jnp.exp(decay_in_block)).astype(BF16)
            full = lax.dot_general(qh, kd, (((2,), (2,)), ((0,), (0,))), preferred_element_type=F32)
            return jnp.where(same_blk & at_or_before, full, att_off)

        def pairwise_diag():
            att = att_off
            bs, ks, qs = stack(b * LOG2_E), stack(kx), stack(q)
            for j in range(SUB):
                pj = (qs * within_block(ks, j) * jnp.exp2(jnp.minimum(bs - within_block(bs, j), 0.0))).astype(BF16)
                tot = jnp.dot(pj.reshape(P * L, N), ones_nl, preferred_element_type=F32).reshape(P, L, L)
                att = jnp.where((cl == blk_r * SUB + j) & at_or_before, tot, att)
            return att

        att = lax.cond(jnp.max(decay_in_block) < HG_SAFE_EXP, factored_diag, pairwise_diag)
        s0 = s_ref[...]
        o = lax.dot_general(att.astype(BF16), vb, (((2,), (1,)), ((0,), (0,))), preferred_element_type=F32)
        o = o + lax.dot_general(stack(q * jnp.exp(b)).astype(BF16), s0.astype(BF16),
                                (((2,), (2,)), ((0,), (0,))), preferred_element_type=F32)
        for p in range(P):
            o_ref[0, rows, p * N:(p + 1) * N] = o[p]
        kend = stack(kx * jnp.exp(bend - b)).astype(BF16)
        s_ref[...] = s0 * stack(jnp.exp(bend)) + lax.dot_general(
            vb, kend, (((1,), (1,)), ((0,), (0,))), preferred_element_type=F32)


def hgrn_scan(z, f_bias, lb, B, S):
    T, D5 = z.shape
    D = D5 // 5
    tb = _tile(S, (HG_BLOCK_T, HG_CHUNK))
    hw = _tile(D, (HG_BLOCK_H * HG_DK, 4 * HG_DK, 2 * HG_DK, HG_DK))
    nt, nh = S // tb, D // hw

    def trow(b, d, c):
        return b * nt + jnp.where(d == 0, c, nt - 1 - c)

    return pl.pallas_call(
        _hgrn_scan_kernel,
        grid=(B, nh, 2, nt),
        in_specs=[pl.BlockSpec((tb, hw), lambda b, g, d, c: (trow(b, d, c), g)),
                  pl.BlockSpec((tb, hw), lambda b, g, d, c: (trow(b, d, c), (1 + d) * nh + g)),
                  pl.BlockSpec((tb, hw), lambda b, g, d, c: (trow(b, d, c), 3 * nh + g)),
                  pl.BlockSpec((1, 1, hw), lambda b, g, d, c: (d, 0, g)),
                  pl.BlockSpec((1, hw), lambda b, g, d, c: (0, g))],
        out_specs=pl.BlockSpec((1, tb, hw), lambda b, g, d, c: (d, trow(b, d, c), g)),
        out_shape=jax.ShapeDtypeStruct((2, T, D), F32),
        scratch_shapes=[pltpu.VMEM((hw // HG_DK, HG_DK, HG_DK), F32)],
        compiler_params=_cparams("parallel", "parallel", "arbitrary", "arbitrary"),
        name="hgrn_scan",
    )(z, z, z, f_bias.reshape(2, 1, D), lb.reshape(1, D))


def _head_out_kernel(y_ref, g_ref, gain_ref, o_ref, *, head, act):
    y = y_ref[0] + y_ref[1]
    g = g_ref[...].astype(F32)
    gain = gain_ref[...]
    for p in range(y.shape[1] // head):
        cols = slice(p * head, (p + 1) * head)
        yp = y[:, cols]
        yn = yp * lax.rsqrt(jnp.mean(yp * yp, axis=-1, keepdims=True) + NORM_EPS) * gain[:, cols]
        o_ref[:, cols] = (yn * act(g[:, cols])).astype(o_ref.dtype)


def head_out(y, z, gate_col, gain, head, act):
    _, T, D = y.shape
    tm = _tile(T, (512, 256, 128, 64, 32, 16, 8))
    w = _tile(D, (1024, 512, 256, 128))
    w = max(w, head)
    assert gate_col % w == 0
    gb = gate_col // w
    return pl.pallas_call(
        functools.partial(_head_out_kernel, head=head, act=act),
        grid=(T // tm, D // w),
        in_specs=[pl.BlockSpec((2, tm, w), lambda i, j: (0, i, j)),
                  pl.BlockSpec((tm, w), lambda i, j: (i, gb + j)),
                  pl.BlockSpec((1, w), lambda i, j: (0, j))],
        out_specs=pl.BlockSpec((tm, w), lambda i, j: (i, j)),
        out_shape=jax.ShapeDtypeStruct((T, D), BF16),
        compiler_params=_cparams("parallel", "parallel"),
        name="head_out",
    )(y, z, gain.reshape(1, D))


def hgrn2_layer(h, hn, w_in, f_bias, lb_logits, layer_idx, head_gain, w_out, B, S, lj=0):
    T, D = h.shape
    z = matmul(hn, w_in, lj, out_dtype=BF16)
    probs = jax.nn.softmax(lb_logits, axis=0)
    lb = (jnp.cumsum(probs, axis=0) - probs[0])[layer_idx]
    o = hgrn_scan(z, f_bias, lb, B, S)
    gated = head_out(o, z, 4 * D, head_gain, HG_DK, jax.nn.silu)
    return matmul(gated, w_out, lj, residual=h)


def kernel(x, p, norm_mix, norm_ffn, norm_ple_gate, norm_ple_post, norm_final, ml_w_in, ml_gate_bias, ml_conv, ml_head_gain, ml_w_out, rw_mu, rw_w_rkv, rw_w0, rw_w1, rw_w2, rw_a0, rw_a1, rw_a2, rw_g1, rw_g2, rw_k_k, rw_k_a, rw_r_k, rw_ln_gain, rw_ln_bias, rw_w_o, hg_w_in, hg_f_bias, hg_lb, hg_head_gain, hg_w_out, moe_router, moe_w_gate, moe_w_up, moe_w_down, ple_w, ple_gate_down, ple_gate_up):
    B, S, D = x.shape
    depth = p.shape[0]
    T = B * S
    h = x.reshape(T, D)
    hn = rms_norm_bf16(h, norm_mix[0])
    for i in range(depth):
        kind, j = i % 3, i // 3
        if kind == 0:
            h = mlstm_layer(h, hn, ml_w_in, ml_gate_bias[j], ml_conv[j], ml_head_gain[j], ml_w_out, B, S, j)
        elif kind == 1:
            h = rwkv_layer(h, hn, rw_mu[j], rw_w_rkv, rw_w0[j], rw_w1[j], rw_w2[j], rw_a0[j], rw_a1[j],
                           rw_a2[j], rw_g1[j], rw_g2[j], rw_k_k[j], rw_k_a[j], rw_r_k[j], rw_ln_gain[j],
                           rw_ln_bias[j], rw_w_o, B, S, j)
        else:
            h = hgrn2_layer(h, hn, hg_w_in, hg_f_bias[j], hg_lb, i, hg_head_gain[j], hg_w_out, B, S, j)
        h = moe_layer(h, norm_ffn[i], moe_router, moe_w_gate, moe_w_up, moe_w_down, i, B, S)
        last = i + 1 == depth
        h, hn = ple_layer(h, p[i].reshape(T, -1), ple_w[i], ple_gate_down[i], ple_gate_up[i], norm_ple_gate[i],
                          norm_ple_post[i], norm_final if last else norm_mix[i + 1], F32 if last else BF16)
    return hn.reshape(B, S, D)
```

```python
import functools

import jax
import jax.numpy as jnp
from jax import lax
from jax.experimental import pallas as pl
from jax.experimental.pallas import tpu as pltpu

F32 = jnp.float32
BF16 = jnp.bfloat16

NORM_EPS = 1e-6
GATE_CAP = 15.0
ML_CHUNK = 128
RW_HEAD = 64
RW_GN_EPS = 64e-5
HG_DK = 128
HG_CHUNK = 64
EC_CAPACITY = 2

LANES = 128
VMEM_LIMIT = 52 * 1024 * 1024


def _cparams(*sem):
    return pltpu.CompilerParams(dimension_semantics=sem, vmem_limit_bytes=VMEM_LIMIT)


def _tile(n, prefs):
    for t in prefs:
        if n % t == 0:
            return t
    return n


def _rms(x, gain):
    return x * lax.rsqrt(jnp.mean(x * x, axis=-1, keepdims=True) + NORM_EPS) * gain


def _norm_kernel(h_ref, g_ref, o_ref):
    o_ref[...] = _rms(h_ref[...], g_ref[...]).astype(o_ref.dtype)


def rms_norm_bf16(h, gain):
    T, D = h.shape
    tm = _tile(T, (256, 128, 64, 32, 16, 8))
    return pl.pallas_call(
        _norm_kernel,
        grid=(T // tm,),
        in_specs=[pl.BlockSpec((tm, D), lambda i: (i, 0)), pl.BlockSpec((1, D), lambda i: (0, 0))],
        out_specs=pl.BlockSpec((tm, D), lambda i: (i, 0)),
        out_shape=jax.ShapeDtypeStruct((T, D), BF16),
        compiler_params=_cparams("parallel"),
        name="rms_norm",
    )(h, gain.reshape(1, D))


def _router_kernel(h_ref, g_ref, r_ref, l_ref):
    yh, yl = _split(_rms(h_ref[...], g_ref[...]))
    rh, rl = _split(r_ref[0])
    l_ref[...] = _bdot(yh, rh) + (_bdot(yh, rl) + _bdot(yl, rh))


def router_logits(h, gain, router, layer):
    T, D = h.shape
    E = router.shape[2]
    rp = jnp.pad(router, ((0, 0), (0, 0), (0, LANES - E)))
    tm = _tile(T, (256, 128, 64, 32, 16, 8))
    logits = pl.pallas_call(
        _router_kernel,
        grid=(T // tm,),
        in_specs=[pl.BlockSpec((tm, D), lambda i: (i, 0)), pl.BlockSpec((1, D), lambda i: (0, 0)),
                  pl.BlockSpec((1, D, LANES), lambda i: (layer, 0, 0))],
        out_specs=pl.BlockSpec((tm, LANES), lambda i: (i, 0)),
        out_shape=jax.ShapeDtypeStruct((T, LANES), F32),
        compiler_params=_cparams("parallel"),
        name="router_logits",
    )(h, gain.reshape(1, D), rp)
    return logits[:, :E]


_CAST_ROWS = 256


def _cast_weight(w_ref, wb_ref):
    k = w_ref.shape[0]
    rows = _tile(k, (_CAST_ROWS, 128, 64, 32, 16))

    def body(c, _):
        r = pl.multiple_of(c * rows, rows)
        wb_ref[pl.ds(r, rows), :] = w_ref[pl.ds(r, rows), :].astype(BF16)
        return 0

    lax.fori_loop(0, k // rows, body, 0)


def _mm_kernel(x_ref, w_ref, *rest, has_res, has_bias, act, w_is_nk):
    rest = list(rest)
    b_ref = rest.pop(0) if has_bias else None
    r_ref = rest.pop(0) if has_res else None
    o_ref, wb_ref = rest

    @pl.when(pl.program_id(1) == 0)
    def _():
        _cast_weight(w_ref.at[0], wb_ref)

    contract = (((1,), (1,)), ((), ())) if w_is_nk else (((1,), (0,)), ((), ()))
    acc = lax.dot_general(x_ref[...], wb_ref[...], contract, preferred_element_type=F32)
    if has_bias:
        acc = acc + b_ref[...]
    if act is not None:
        acc = act(acc)
    if has_res:
        acc = acc + r_ref[...]
    o_ref[...] = acc.astype(o_ref.dtype)


def matmul(x, w, layer=0, *, n=None, col_off=0, bias=None, residual=None, act=None, out_dtype=F32, tm=None, tn=None,
           w_is_nk=False):
    M, K = x.shape
    if w.ndim == 2:
        w = w[None]
    if n is None:
        n = w.shape[1 if w_is_nk else 2] - col_off
    tm = tm or _tile(M, (1024, 512, 256, 128, 64, 32, 16))
    tn = tn or _tile(n, (512, 256, 128))
    assert n % tn == 0 and col_off % tn == 0 and M % tm == 0
    cb = col_off // tn
    if w_is_nk:
        w_spec = pl.BlockSpec((1, tn, K), lambda j, i: (layer, j + cb, 0))
    else:
        w_spec = pl.BlockSpec((1, K, tn), lambda j, i: (layer, 0, j + cb))
    in_specs = [pl.BlockSpec((tm, K), lambda j, i: (i, 0)), w_spec]
    args = [x, w]
    if bias is not None:
        in_specs.append(pl.BlockSpec((1, tn), lambda j, i: (0, j)))
        args.append(bias.reshape(1, n))
    if residual is not None:
        in_specs.append(pl.BlockSpec((tm, tn), lambda j, i: (i, j)))
        args.append(residual)
    return pl.pallas_call(
        functools.partial(_mm_kernel, has_res=residual is not None, has_bias=bias is not None, act=act,
                          w_is_nk=w_is_nk),
        grid=(n // tn, M // tm),
        in_specs=in_specs,
        out_specs=pl.BlockSpec((tm, tn), lambda j, i: (i, j)),
        out_shape=jax.ShapeDtypeStruct((M, n), out_dtype),
        scratch_shapes=[pltpu.VMEM((tn, K) if w_is_nk else (K, tn), BF16)],
        compiler_params=_cparams("arbitrary", "arbitrary"),
        name="matmul",
    )(*args)


def _lora_kernel(x_ref, a_ref, b_ref, bias_ref, o_ref, ab_ref, bb_ref, *, act1, act2):
    @pl.when(pl.program_id(0) == 0)
    def _():
        _cast_weight(a_ref, ab_ref)
        _cast_weight(b_ref, bb_ref)

    t = jnp.dot(x_ref[...], ab_ref[...], preferred_element_type=F32)
    if act1 is not None:
        t = act1(t)
    y = jnp.dot(t.astype(BF16), bb_ref[...], preferred_element_type=F32) + bias_ref[...]
    if act2 is not None:
        y = act2(y)
    o_ref[...] = y.astype(o_ref.dtype)


def lora(x, a, b, bias=None, act1=None, act2=None, out_dtype=F32, tm=256):
    M, K = x.shape
    R, N = b.shape
    rp = -R % LANES
    if rp:
        a = jnp.pad(a, ((0, 0), (0, rp)))
        b = jnp.pad(b, ((0, rp), (0, 0)))
        R += rp
    if bias is None:
        bias = jnp.zeros((N,), F32)
    tm = _tile(M, (tm, 128, 64, 32, 16))
    return pl.pallas_call(
        functools.partial(_lora_kernel, act1=act1, act2=act2),
        grid=(M // tm,),
        in_specs=[pl.BlockSpec((tm, K), lambda i: (i, 0)), pl.BlockSpec((K, R), lambda i: (0, 0)),
                  pl.BlockSpec((R, N), lambda i: (0, 0)), pl.BlockSpec((1, N), lambda i: (0, 0))],
        out_specs=pl.BlockSpec((tm, N), lambda i: (i, 0)),
        out_shape=jax.ShapeDtypeStruct((M, N), out_dtype),
        scratch_shapes=[pltpu.VMEM((K, R), BF16), pltpu.VMEM((R, N), BF16)],
        compiler_params=_cparams("arbitrary"),
        name="lora",
    )(x, a, b, bias.reshape(1, N))


def _ple_kernel(h_ref, p_ref, wp_ref, gd_ref, gu_ref, gg_ref, pg_ref, ng_ref, ho_ref, no_ref,
                wpb_ref, gdb_ref, gub_ref):
    @pl.when(pl.program_id(0) == 0)
    def _():
        _cast_weight(wp_ref, wpb_ref)
        _cast_weight(gd_ref, gdb_ref)
        _cast_weight(gu_ref, gub_ref)

    h = h_ref[...]
    e = jnp.dot(p_ref[...].astype(BF16), wpb_ref[...], preferred_element_type=F32)
    e = _rms(e, pg_ref[...])
    t = jnp.dot(_rms(h, gg_ref[...]).astype(BF16), gdb_ref[...], preferred_element_type=F32)
    g = jnp.dot(t.astype(BF16), gub_ref[...], preferred_element_type=F32)
    hn = h + e * jax.nn.sigmoid(g)
    ho_ref[...] = hn
    no_ref[...] = _rms(hn, ng_ref[...]).astype(no_ref.dtype)


def ple_layer(h, p, w_ple, g_down, g_up, gate_gain, post_gain, next_gain, next_dtype):
    T, D = h.shape
    P = p.shape[1]
    tm = _tile(T, (256, 128, 64, 32, 16, 8))
    row = lambda i: (i, 0)
    fix = lambda i: (0, 0)
    return pl.pallas_call(
        _ple_kernel,
        grid=(T // tm,),
        in_specs=[pl.BlockSpec((tm, D), row), pl.BlockSpec((tm, P), row), pl.BlockSpec((P, D), fix),
                  pl.BlockSpec((D, P), fix), pl.BlockSpec((P, D), fix), pl.BlockSpec((1, D), fix),
                  pl.BlockSpec((1, D), fix), pl.BlockSpec((1, D), fix)],
        out_specs=[pl.BlockSpec((tm, D), row), pl.BlockSpec((tm, D), row)],
        out_shape=[jax.ShapeDtypeStruct((T, D), F32), jax.ShapeDtypeStruct((T, D), next_dtype)],
        scratch_shapes=[pltpu.VMEM((P, D), BF16), pltpu.VMEM((D, P), BF16), pltpu.VMEM((P, D), BF16)],
        compiler_params=_cparams("arbitrary"),
        name="ple",
    )(h, p, w_ple, g_down, g_up, gate_gain.reshape(1, D), post_gain.reshape(1, D), next_gain.reshape(1, D))


MOE_NORM_ROWS = 256
DMA_ISSUE_UNROLL = 8


def _moe_up_kernel(tok_ref, nxt_ref, h_ref, gain_ref, wg_ref, wu_ref, o_ref, xg_ref, xb_ref, wgb_ref, wub_ref, sem):
    rows = xg_ref.shape[0]
    e = pl.program_id(0)

    def gather(idx_ref):
        def issue(i, _):
            pltpu.make_async_copy(h_ref.at[pl.ds(idx_ref[0, 0, i], 1), :], xg_ref.at[pl.ds(i, 1), :], sem).start()
            return 0

        lax.fori_loop(0, rows, issue, 0, unroll=DMA_ISSUE_UNROLL)

    @pl.when(pl.program_id(1) == 0)
    def _():
        @pl.when(e == 0)
        def _():
            gather(tok_ref)

        pltpu.make_async_copy(h_ref.at[pl.ds(0, rows), :], xg_ref, sem).wait()
        step = _tile(rows, (MOE_NORM_ROWS, 128, 64, 32, 16))

        def norm(c, _):
            r = pl.multiple_of(c * step, step)
            xb_ref[pl.ds(r, step), :] = _rms(xg_ref[pl.ds(r, step), :], gain_ref[...]).astype(BF16)
            return 0

        lax.fori_loop(0, rows // step, norm, 0)

        @pl.when(e + 1 < pl.num_programs(0))
        def _():
            gather(nxt_ref)

    _cast_weight(wg_ref.at[0, 0], wgb_ref)
    _cast_weight(wu_ref.at[0, 0], wub_ref)
    x = xb_ref[...]
    g = jnp.dot(x, wgb_ref[...], preferred_element_type=F32)
    u = jnp.dot(x, wub_ref[...], preferred_element_type=F32)
    o_ref[0] = (g * jax.nn.sigmoid(g) * u).astype(o_ref.dtype)


def _moe_down_kernel(h_ref, wd_ref, s_ref, o_ref, wdb_ref):
    _cast_weight(wd_ref.at[0, 0], wdb_ref)
    o_ref[0] = jnp.dot(h_ref[0], wdb_ref[...], preferred_element_type=F32) * s_ref[0]


def moe_ffn(h, gain, tok, gsel, w_gate, w_up, w_down, layer):
    T, D = h.shape
    E, R = tok.shape
    FF = w_gate.shape[3]
    tf = _tile(FF, (256, 128))
    hid = pl.pallas_call(
        _moe_up_kernel,
        grid=(E, FF // tf),
        in_specs=[pl.BlockSpec((1, 1, R), lambda e, f: (e, 0, 0), memory_space=pltpu.SMEM),
                  pl.BlockSpec((1, 1, R), lambda e, f: (jnp.minimum(e + 1, E - 1), 0, 0), memory_space=pltpu.SMEM),
                  pl.BlockSpec(memory_space=pl.ANY),
                  pl.BlockSpec((1, D), lambda e, f: (0, 0)),
                  pl.BlockSpec((1, 1, D, tf), lambda e, f: (layer, e, 0, f)),
                  pl.BlockSpec((1, 1, D, tf), lambda e, f: (layer, e, 0, f))],
        out_specs=pl.BlockSpec((1, R, tf), lambda e, f: (e, 0, f)),
        out_shape=jax.ShapeDtypeStruct((E, R, FF), BF16),
        scratch_shapes=[pltpu.VMEM((R, D), F32), pltpu.VMEM((R, D), BF16),
                        pltpu.VMEM((D, tf), BF16), pltpu.VMEM((D, tf), BF16), pltpu.SemaphoreType.DMA(())],
        compiler_params=_cparams("arbitrary", "arbitrary"),
        name="moe_up",
    )(tok.reshape(E, 1, R), tok.reshape(E, 1, R), h, gain.reshape(1, D), w_gate, w_up)
    tn = _tile(D, (1024, 512, 256, 128))
    return pl.pallas_call(
        _moe_down_kernel,
        grid=(E, D // tn),
        in_specs=[pl.BlockSpec((1, R, FF), lambda e, j: (e, 0, 0)),
                  pl.BlockSpec((1, 1, FF, tn), lambda e, j: (layer, e, 0, j)),
                  pl.BlockSpec((1, R, 1), lambda e, j: (e, 0, 0))],
        out_specs=pl.BlockSpec((1, R, tn), lambda e, j: (e, 0, j)),
        out_shape=jax.ShapeDtypeStruct((E, R, D), F32),
        scratch_shapes=[pltpu.VMEM((FF, tn), BF16)],
        compiler_params=_cparams("parallel", "parallel"),
        name="moe_down",
    )(hid, w_down, gsel.reshape(E, R, 1))


MOE_TILE = 256
MOE_CHUNK = 256


def _moe_combine_kernel(tile_ref, chunk_ref, next_ref, flag_ref, h_ref, cur_ref, nxt_ref, tok_ref, y_ref, o_ref,
                        buf_ref, sem):
    w = pl.program_id(0)
    nw = pl.num_programs(0)
    ch = buf_ref.shape[1]
    tm = h_ref.shape[0]
    slot = w % 2

    def gather(idx_ref, s):
        def issue(j, _):
            pltpu.make_async_copy(y_ref.at[pl.ds(idx_ref[0, 0, j], 1), :], buf_ref.at[s, pl.ds(j, 1), :],
                                  sem.at[s]).start()
            return 0

        lax.fori_loop(0, ch, issue, 0, unroll=DMA_ISSUE_UNROLL)

    @pl.when(w == 0)
    def _():
        gather(cur_ref, 0)

    @pl.when(w + 1 < nw)
    def _():
        gather(nxt_ref, 1 - slot)

    pltpu.make_async_copy(y_ref.at[pl.ds(0, ch), :], buf_ref.at[slot], sem.at[slot]).wait()
    flags = flag_ref[w]

    @pl.when(flags >= 2)
    def _():
        o_ref[...] = h_ref[...]

    @pl.when(flags % 2 == 1)
    def _():
        local = tok_ref[0] - tile_ref[w] * tm
        rows = lax.broadcasted_iota(jnp.int32, (tm, ch), 0)
        onehot = jnp.where(rows == local, 1.0, 0.0).astype(BF16)
        o_ref[...] += _dot_exact_lhs(onehot, buf_ref[slot])


def moe_combine(h, y, tok):
    T, D = h.shape
    n = tok.shape[0]
    tm = _tile(T, (MOE_TILE, 128, 64, 32, 16, 8))
    ch = _tile(n, (MOE_CHUNK, 128))
    nt, nch = T // tm, n // ch
    order = jnp.argsort(tok).astype(jnp.int32)
    stok = tok[order]
    edges = jnp.arange(nt + 1, dtype=jnp.int32) * tm
    starts = jnp.sum((tok[None, :] < edges[:, None]).astype(jnp.int32), axis=1)
    g_lo = jnp.minimum(starts[:-1] // ch, nch - 1)
    g_hi = jnp.clip((starts[1:] - 1) // ch, g_lo, nch - 1)
    cnt = g_hi - g_lo + 1
    off = jnp.cumsum(cnt)
    nw = nch + nt
    wi = jnp.arange(nw, dtype=jnp.int32)
    tile = jnp.minimum(jnp.sum((off[None, :] <= wi[:, None]).astype(jnp.int32), axis=1), nt - 1)
    first_w = (off - cnt)[tile]
    valid = wi < off[-1]
    chunk = jnp.where(valid, g_lo[tile] + wi - first_w, nch - 1).astype(jnp.int32)
    flags = (valid.astype(jnp.int32) + 2 * (valid & (wi == first_w)).astype(jnp.int32))
    nxt = jnp.concatenate([chunk[1:], chunk[-1:]])
    idx3 = order.reshape(nch, 1, ch)
    grid_spec = pltpu.PrefetchScalarGridSpec(
        num_scalar_prefetch=4,
        grid=(nw,),
        in_specs=[pl.BlockSpec((tm, D), lambda w, t, c, x, f: (t[w], 0)),
                  pl.BlockSpec((1, 1, ch), lambda w, t, c, x, f: (c[w], 0, 0), memory_space=pltpu.SMEM),
                  pl.BlockSpec((1, 1, ch), lambda w, t, c, x, f: (x[w], 0, 0), memory_space=pltpu.SMEM),
                  pl.BlockSpec((1, 1, ch), lambda w, t, c, x, f: (c[w], 0, 0)),
                  pl.BlockSpec(memory_space=pl.ANY)],
        out_specs=pl.BlockSpec((tm, D), lambda w, t, c, x, f: (t[w], 0)),
        scratch_shapes=[pltpu.VMEM((2, ch, D), F32), pltpu.SemaphoreType.DMA((2,))],
    )

    return pl.pallas_call(
        _moe_combine_kernel,
        grid_spec=grid_spec,
        out_shape=jax.ShapeDtypeStruct((T, D), F32),
        compiler_params=_cparams("arbitrary"),
        name="moe_combine",
    )(tile, chunk, nxt, flags, h, idx3, idx3, stok.reshape(nch, 1, ch), y)


def moe_layer(h, gain, router, w_gate, w_up, w_down, layer, B, S):
    T, D = h.shape
    E = router.shape[2]
    cap = EC_CAPACITY * S // E
    logits = router_logits(h, gain, router, layer)
    aff = jax.nn.softmax(logits, axis=-1).reshape(B, S, E)
    gsel, idx = lax.top_k(jnp.swapaxes(aff, 1, 2), cap)
    tok = idx + (jnp.arange(B, dtype=idx.dtype) * S)[:, None, None]
    tok = jnp.swapaxes(tok, 0, 1).reshape(E, B * cap)
    gsel = jnp.swapaxes(gsel, 0, 1).reshape(E, B * cap)
    out = moe_ffn(h, gain, tok, gsel, w_gate, w_up, w_down, layer)
    return moe_combine(h, out.reshape(-1, D), tok.reshape(-1))


def _softplus(x):
    return jnp.maximum(x, 0.0) + jnp.log(1.0 + jnp.exp(-jnp.abs(x)))


def _dwconv3_kernel(x_ref, w_ref, o_ref):
    x = x_ref[0].astype(F32)
    s = x.shape[0]
    t = lax.broadcasted_iota(jnp.int32, x.shape, 0)
    prev = jnp.where(t == 0, 0.0, pltpu.roll(x, 1, 0))
    nxt = jnp.where(t == s - 1, 0.0, pltpu.roll(x, s - 1, 0))
    o_ref[0] = (w_ref[0:1, :] * prev + w_ref[1:2, :] * x + w_ref[2:3, :] * nxt).astype(o_ref.dtype)


def dwconv3(x, w):
    B, S, _ = x.shape
    assert w.shape[0] == 3
    C = w.shape[1]
    blk = pl.BlockSpec((1, S, LANES), lambda b, j: (b, 0, j))
    return pl.pallas_call(
        _dwconv3_kernel,
        grid=(B, C // LANES),
        in_specs=[blk, pl.BlockSpec((3, LANES), lambda b, j: (0, j))],
        out_specs=blk,
        out_shape=jax.ShapeDtypeStruct((B, S, C), BF16),
        compiler_params=_cparams("parallel", "parallel"),
        name="dwconv3",
    )(x, w)


ML_EXT = LANES
ML_BLOCK_H = 8


def _mlstm_scan_kernel(q_ref, k_ref, v_ref, g_ref, gb_ref, o_ref, c_ref, m_ref, *, heads):
    hg = pl.program_id(1)
    d = pl.program_id(2)
    L = q_ref.shape[0]
    hb = c_ref.shape[0]
    dk = q_ref.shape[1] // hb
    dv = v_ref.shape[1] // hb

    @pl.when(pl.program_id(3) == 0)
    def _():
        c_ref[...] = jnp.zeros_like(c_ref)
        m_ref[...] = jnp.full_like(m_ref, -1e30)

    sgn = jnp.where(d == 0, 1, -1)
    rl = lax.broadcasted_iota(jnp.int32, (L, L), 0)
    cl = lax.broadcasted_iota(jnp.int32, (L, L), 1)
    incl = (cl - rl) * sgn <= 0
    incl_t = (rl - cl) * sgn <= 0
    eye = rl == cl

    g = g_ref[...] + gb_ref[...]
    g = GATE_CAP * jnp.tanh(g / GATE_CAP)
    lane = lax.broadcasted_iota(jnp.int32, g.shape, 1)
    ones_col = jnp.where(lax.broadcasted_iota(jnp.int32, (L, ML_EXT), 1) == 0, 1.0, 0.0).astype(BF16)

    for hh in range(hb):
        i_idx = d * 2 * heads + hg * hb + hh
        i_col = jnp.sum(jnp.where(lane == i_idx, g, 0.0), axis=1, keepdims=True)
        f_col = -_softplus(-jnp.sum(jnp.where(lane == i_idx + heads, g, 0.0), axis=1, keepdims=True))
        i_row = jnp.sum(jnp.where(eye, i_col, 0.0), axis=0, keepdims=True)
        f_row = jnp.sum(jnp.where(eye, f_col, 0.0), axis=0, keepdims=True)
        b_col = jnp.sum(jnp.where(incl, f_row, 0.0), axis=1, keepdims=True)
        b_row = jnp.sum(jnp.where(incl_t, f_col, 0.0), axis=0, keepdims=True)
        gtot = jnp.sum(f_col, axis=0, keepdims=True)
        m = m_ref[hh, 0:1, 0:1]

        dm = b_col - b_row + i_row
        inter_log = b_col + m
        m_t = jnp.maximum(inter_log, jnp.max(jnp.where(incl, dm, -jnp.inf), axis=1, keepdims=True))
        pmat = jnp.where(incl, jnp.exp(jnp.where(incl, dm - m_t, 0.0)), 0.0)
        w_inter = jnp.exp(inter_log - m_t)

        q = q_ref[:, hh * dk:(hh + 1) * dk]
        k = k_ref[:, hh * dk:(hh + 1) * dk]
        s = lax.dot_general(q, k, (((1,), (1,)), ((), ())), preferred_element_type=F32) * pmat
        vext = jnp.concatenate([v_ref[:, hh * dv:(hh + 1) * dv], ones_col], axis=1)
        c = c_ref[hh]
        tot = (jnp.dot(s.astype(BF16), vext, preferred_element_type=F32)
               + w_inter * jnp.dot(q, c.astype(BF16), preferred_element_type=F32))
        den = tot[:, dv:dv + 1]
        o_ref[0, :, hh * dv:(hh + 1) * dv] = tot[:, :dv] / jnp.maximum(jnp.abs(den), jnp.exp(-m_t))

        src = gtot - b_col + i_col
        m_new = jnp.maximum(gtot + m, jnp.max(src, axis=0, keepdims=True))
        kw = (k.astype(F32) * jnp.exp(src - m_new)).astype(BF16)
        c_ref[hh] = jnp.exp(gtot + m - m_new) * c + lax.dot_general(
            kw, vext, (((0,), (0,)), ((), ())), preferred_element_type=F32)
        m_ref[hh] = jnp.broadcast_to(m_new, (1, LANES))


def mlstm_scan(qk, z, gates, gate_bias, heads, B, S):
    T, QK2 = qk.shape
    L = ML_CHUNK
    dk = QK2 // 2 // heads
    V = (z.shape[1] - QK2) // 2
    dv = V // heads
    nc = S // L
    hb = _tile(heads, (ML_BLOCK_H, 1))
    ng = heads // hb

    def trow(b, d, c):
        return b * nc + jnp.where(d == 0, c, nc - 1 - c)

    return pl.pallas_call(
        functools.partial(_mlstm_scan_kernel, heads=heads),
        grid=(B, ng, 2, nc),
        in_specs=[pl.BlockSpec((L, hb * dk), lambda b, h, d, c: (trow(b, d, c), h)),
                  pl.BlockSpec((L, hb * dk), lambda b, h, d, c: (trow(b, d, c), ng + h)),
                  pl.BlockSpec((L, hb * dv), lambda b, h, d, c: (trow(b, d, c), QK2 // (hb * dv) + h)),
                  pl.BlockSpec((L, LANES), lambda b, h, d, c: (trow(b, d, c), 0)),
                  pl.BlockSpec((1, LANES), lambda b, h, d, c: (0, 0))],
        out_specs=pl.BlockSpec((1, L, hb * dv), lambda b, h, d, c: (d, trow(b, d, c), h)),
        out_shape=jax.ShapeDtypeStruct((2, T, V), F32),
        scratch_shapes=[pltpu.VMEM((hb, dk, dv + ML_EXT), F32), pltpu.VMEM((hb, 1, LANES), F32)],
        compiler_params=_cparams("parallel", "parallel", "arbitrary", "arbitrary"),
        name="mlstm_scan",
    )(qk, qk, z, gates, gate_bias.reshape(1, LANES))


def mlstm_layer(h, hn, w_in, gate_bias, conv_w, head_gain, w_out, B, S, lj=0):
    T, D = h.shape
    H = gate_bias.shape[0] // 4
    QK = conv_w.shape[1] // 2
    V = w_out.shape[-2]
    DK, DV = QK // H, V // H
    nmain = 2 * QK + 2 * V
    w_t = jnp.swapaxes(w_in if w_in.ndim == 3 else w_in[None], 1, 2)
    z = matmul(hn, w_t, lj, n=nmain, out_dtype=BF16, w_is_nk=True)
    pad = LANES - 4 * H
    w_gates = lax.slice(w_t, (lj, nmain, 0), (lj + 1, nmain + 4 * H, D))[0]
    gates = matmul(hn, jnp.pad(w_gates, ((0, pad), (0, 0))), w_is_nk=True)
    taps = conv_w * jnp.concatenate([jnp.full((QK,), DK ** -0.5, F32), jnp.ones((QK,), F32)])
    qk = dwconv3(z.reshape(B, S, nmain), taps).reshape(T, 2 * QK)
    hs = mlstm_scan(qk, z, gates, jnp.pad(gate_bias, (0, pad)), H, B, S)
    gated = head_out(hs, z, 2 * QK + V, head_gain, DV, jax.nn.sigmoid)
    return matmul(gated, w_out, lj, residual=h)


RW_CHUNK = 64
RW_SUB = 16
RW_BLOCK_T = 128
RW_BLOCK_H = 32
assert RW_CHUNK == RW_HEAD


def _bdot(a, b):
    return jnp.dot(a, b, preferred_element_type=F32)


def _split(x):
    hi = x.astype(BF16)
    return hi, (x - hi.astype(F32)).astype(BF16)


def _dot_exact_lhs(m, x):
    xh, xl = _split(x)
    return _bdot(m, xh) + _bdot(m, xl)


def _dot_exact_rhs(x, m):
    xh, xl = _split(x)
    return _bdot(xh, m) + _bdot(xl, m)


def _tri_inverse(a, blk, eye, nblk, mm, bd):
    d = jnp.where(blk, a, 0.0)
    off = a - d
    rows = a.shape[1]
    p = eye + d
    pw = mm(d, bd(d))
    for _ in range(RW_SUB.bit_length() - 3):
        both = mm(jnp.concatenate([pw, p], axis=1), bd(pw))
        pw, p = both[:, :rows], p + both[:, rows:]
    p = p + mm(p, bd(pw))
    n = mm(p, bd(off))
    q = eye + n
    pw, reach = n, 1
    while 2 * reach < nblk:
        pw = mm(pw, bd(pw))
        q = q + mm(q, bd(pw))
        reach *= 2
    return mm(q, bd(p))


def _rwkv_scan_kernel(r_ref, k_ref, v_ref, wl_ref, a_ref, kk_ref, ka_ref, y_ref, s_ref):
    d = pl.program_id(2)
    L, N = RW_CHUNK, RW_HEAD
    W = 2 * N
    tb, hw = r_ref.shape
    nch, P = tb // L, hw // W

    @pl.when(pl.program_id(3) == 0)
    def _():
        s_ref[...] = jnp.zeros_like(s_ref)

    fwd = d == 0
    sgn = jnp.where(fwd, 1, -1)
    row = lax.broadcasted_iota(jnp.int32, (L, W), 0)
    lane = lax.broadcasted_iota(jnp.int32, (L, W), 1)
    cs = lane % L
    ahead = (cs - row) * sgn
    strict = ahead < 0
    incl = ahead <= 0
    eye = jnp.where(cs == row, 1.0, 0.0)
    blk = (row // RW_SUB) == (cs // RW_SUB)
    low = lane < N
    rl = lax.broadcasted_iota(jnp.int32, (L, L), 0)
    cl = lax.broadcasted_iota(jnp.int32, (L, L), 1)
    incl_bf = jnp.where((cl - rl) * sgn <= 0, 1.0, 0.0).astype(BF16)
    br = lax.broadcasted_iota(jnp.int32, (W, W), 0) // N
    bc = lax.broadcasted_iota(jnp.int32, (W, W), 1) // N
    same = br == bc
    seg = jnp.where(same, 1.0, 0.0).astype(BF16)

    def stack(x):
        return jnp.stack([x[:, p * W:(p + 1) * W] for p in range(P)], axis=0)

    def bd(x):
        return jnp.where(same, jnp.concatenate([x, x], axis=1), 0.0).astype(BF16)

    def mm(a, b):
        return lax.dot_general(a.astype(BF16), b, (((2,), (1,)), ((0,), (0,))), preferred_element_type=F32)

    def mm_nt(a, b):
        return lax.dot_general(a.astype(BF16), b, (((2,), (2,)), ((0,), (0,))), preferred_element_type=F32)

    for ci in range(nch):
        t0 = pl.multiple_of(jnp.where(fwd, ci * L, (nch - 1 - ci) * L), L)
        rows = pl.ds(t0, L)
        r = r_ref[rows, :]
        k = k_ref[rows, :]
        v = v_ref[rows, :]
        rate = a_ref[rows, :]
        logw = -jnp.exp(-_softplus(-wl_ref[rows, :]) - 0.5)
        kk = k * kk_ref[...]
        ssq = kk * kk
        nrm2 = jnp.concatenate([_bdot(ssq[:, p * W:(p + 1) * W].astype(BF16), seg) for p in range(P)], axis=1)
        kk = kk / jnp.maximum(jnp.sqrt(nrm2), 1e-12)
        kd = k * (1.0 + (rate - 1.0) * ka_ref[...])
        bv = kk * rate
        c = _dot_exact_lhs(incl_bf, logw)
        cend = jnp.sum(logw, axis=0, keepdims=True)
        einv = jnp.exp(-c)
        eend = jnp.exp(cend - c)
        at = stack(-kk * jnp.exp(c - logw))
        rt = stack(r * jnp.exp(c))
        vp = stack(v)
        ar = jnp.concatenate([at, rt], axis=1)
        g_b = mm_nt(ar, bd(stack(bv * einv)))
        g_k = mm_nt(ar, bd(stack(kd * einv)))
        aab = jnp.where(strict, g_b[:, :L], 0.0)
        rb = jnp.where(incl, g_b[:, L:], 0.0)
        aak = jnp.where(strict, g_k[:, :L], 0.0)
        rk = jnp.where(incl, g_k[:, L:], 0.0)
        tinv = _tri_inverse(aab, blk, eye, L // RW_SUB, mm, bd)
        s0 = s_ref[...]
        bd_v = bd(vp)
        from_state = mm_nt(ar, bd(s0))
        u = mm(tinv, bd(from_state[:, :L] + mm(aak, bd_v)))
        y = from_state[:, L:] + mm(rb, bd(u)) + mm(rk, bd_v)
        for p in range(P):
            y_ref[0, rows, p * W:(p + 1) * W] = y[p]
        uv = jnp.concatenate([u, vp], axis=1).astype(BF16)
        bkh = jnp.concatenate([stack(bv * eend), stack(kd * eend)], axis=1).astype(BF16)
        g = lax.dot_general(uv, bkh, (((1,), (1,)), ((0,), (0,))), preferred_element_type=F32)
        s_ref[...] = s0 * stack(jnp.exp(cend)) + jnp.where(low, g[:, :N], g[:, N:])


def rwkv_scan(r, k, v, wl2, a2, k_k, k_a, B, S):
    T, D = r.shape
    tb = _tile(S, (RW_BLOCK_T, RW_CHUNK))
    hw = _tile(D, (RW_BLOCK_H * RW_HEAD, 4 * RW_HEAD, 2 * RW_HEAD))
    nt, nh = S // tb, D // hw

    def trow(b, d, c):
        return b * nt + jnp.where(d == 0, c, nt - 1 - c)

    shared = pl.BlockSpec((tb, hw), lambda b, g, d, c: (trow(b, d, c), g))
    perdir = pl.BlockSpec((tb, hw), lambda b, g, d, c: (trow(b, d, c), d * nh + g))
    par = pl.BlockSpec((1, hw), lambda b, g, d, c: (0, g))
    return pl.pallas_call(
        _rwkv_scan_kernel,
        grid=(B, nh, 2, nt),
        in_specs=[shared, shared, shared, perdir, perdir, par, par],
        out_specs=pl.BlockSpec((1, tb, hw), lambda b, g, d, c: (d, trow(b, d, c), g)),
        out_shape=jax.ShapeDtypeStruct((2, T, D), F32),
        scratch_shapes=[pltpu.VMEM((hw // (2 * RW_HEAD), RW_HEAD, 2 * RW_HEAD), F32)],
        compiler_params=_cparams("parallel", "parallel", "arbitrary", "arbitrary"),
        name="rwkv_scan",
    )(r, k, v, wl2, a2, k_k.reshape(1, D), k_a.reshape(1, D))


def _seg_sum(x, seg_bf):
    return _dot_exact_rhs(x, seg_bf)


def _rwkv_out_kernel(y_ref, r_ref, k_ref, v_ref, af_ref, ab_ref, g_ref, rk_ref, ka_ref, lg_ref, lb_ref, o_ref):
    li = lax.broadcasted_iota(jnp.int32, (LANES, LANES), 0) // RW_HEAD
    lj = lax.broadcasted_iota(jnp.int32, (LANES, LANES), 1) // RW_HEAD
    seg = jnp.where(li == lj, 1.0, 0.0).astype(BF16)
    inv_n = 1.0 / RW_HEAD
    for c in range(y_ref.shape[2] // LANES):
        cols = slice(c * LANES, (c + 1) * LANES)
        y = y_ref[0, :, cols] + y_ref[1, :, cols]
        mean = _seg_sum(y, seg) * inv_n
        yc = y - mean
        var = _seg_sum(yc * yc, seg) * inv_n
        yn = yc * lax.rsqrt(var + RW_GN_EPS) * lg_ref[:, cols] + lb_ref[:, cols]
        a_mean = 0.5 * (af_ref[:, cols] + ab_ref[:, cols])
        k_bonus = k_ref[:, cols] * (1.0 + (a_mean - 1.0) * ka_ref[:, cols])
        bonus = _seg_sum(r_ref[:, cols] * k_bonus * rk_ref[:, cols], seg) * v_ref[:, cols]
        o_ref[:, cols] = ((yn + bonus) * g_ref[:, cols]).astype(o_ref.dtype)


def rwkv_out(y, r, k, v, a2, gate, r_k, k_a, ln_gain, ln_bias):
    _, T, D = y.shape
    tm = _tile(T, (256, 128, 64, 32, 16, 8))
    w = _tile(D, (512, 256, 128))
    nw = D // w
    tile = pl.BlockSpec((tm, w), lambda i, j: (i, j))
    par = pl.BlockSpec((1, w), lambda i, j: (0, j))
    return pl.pallas_call(
        _rwkv_out_kernel,
        grid=(T // tm, nw),
        in_specs=[pl.BlockSpec((2, tm, w), lambda i, j: (0, i, j)), tile, tile, tile,
                  tile, pl.BlockSpec((tm, w), lambda i, j: (i, nw + j)), tile, par, par, par, par],
        out_specs=tile,
        out_shape=jax.ShapeDtypeStruct((T, D), BF16),
        compiler_params=_cparams("parallel", "parallel"),
        name="rwkv_out",
    )(y, r, k, v, a2, a2, gate, r_k.reshape(1, D), k_a.reshape(1, D), ln_gain.reshape(1, D), ln_bias.reshape(1, D))


def _shift_mix_kernel(x_ref, mu_ref, *o_refs):
    x = x_ref[0].astype(F32)
    s = x.shape[0]
    t = lax.broadcasted_iota(jnp.int32, x.shape, 0)
    prev = jnp.where(t == 0, 0.0, pltpu.roll(x, 1, 0))
    nxt = jnp.where(t == s - 1, 0.0, pltpu.roll(x, s - 1, 0))
    dx = 0.5 * (prev + nxt) - x
    for j, o_ref in enumerate(o_refs):
        o_ref[0] = (x + dx * mu_ref[j:j + 1, :]).astype(o_ref.dtype)


def shift_mix(xn, mu):
    B, S, D = xn.shape
    J = mu.shape[0]
    w = LANES
    blk = pl.BlockSpec((1, S, w), lambda b, j: (b, 0, j))
    return pl.pallas_call(
        _shift_mix_kernel,
        grid=(B, D // w),
        in_specs=[blk, pl.BlockSpec((J, w), lambda b, j: (0, j))],
        out_specs=[blk] * J,
        out_shape=[jax.ShapeDtypeStruct((B, S, D), BF16)] * J,
        compiler_params=_cparams("parallel", "parallel"),
        name="shift_mix",
    )(xn, mu)


def rwkv_layer(h, hn, mu, w_rkv, w0, w1, w2, a0, a1, a2, g1, g2, k_k, k_a, r_k, ln_gain, ln_bias, w_o, B, S, lj=0):
    T, D = h.shape
    N = RW_HEAD
    H = D // N
    xm = [t.reshape(T, D) for t in shift_mix(hn.reshape(B, S, D), mu)]
    w_rkv = w_rkv.reshape(-1, D, D)
    r = matmul(xm[0], w_rkv, 3 * lj)
    k = matmul(xm[1], w_rkv, 3 * lj + 1)
    v = matmul(xm[2], w_rkv, 3 * lj + 2)

    def both_dirs(lo, hi):
        z = jnp.zeros_like(hi[0])
        return (jnp.concatenate([lo[0], lo[1]], axis=1),
                jnp.concatenate([jnp.concatenate([hi[0], z], axis=1), jnp.concatenate([z, hi[1]], axis=1)], axis=0))

    wa, wb = both_dirs(w1, w2)
    wl2 = lora(xm[3], wa, wb, bias=w0.reshape(-1), act1=jnp.tanh, tm=128)
    aa, ab = both_dirs(a1, a2)
    rate2 = lora(xm[4], aa, ab, bias=a0.reshape(-1), act2=jax.nn.sigmoid, tm=128)
    gate = lora(xm[5], g1, g2, act1=jax.nn.sigmoid)
    y = rwkv_scan(r, k, v, wl2, rate2, k_k, k_a, B, S)
    out = rwkv_out(y, r, k, v, rate2, gate, r_k, k_a, ln_gain, ln_bias)
    return matmul(out, w_o, lj, residual=h)


LOG2_E = 1.4426950408889634
HG_SAFE_EXP = 60.0
HG_SUB = 16
HG_BLOCK_T = 128
HG_BLOCK_H = 8


def _hgrn_scan_kernel(q_ref, zf_ref, v_ref, fb_ref, lb_ref, o_ref, s_ref):
    d = pl.program_id(2)
    L, N, SUB = HG_CHUNK, HG_DK, HG_SUB
    tb, hw = q_ref.shape
    nch, P = tb // L, hw // N

    @pl.when(pl.program_id(3) == 0)
    def _():
        s_ref[...] = jnp.zeros_like(s_ref)

    fwd = d == 0
    sgn = jnp.where(fwd, 1, -1)
    rl = lax.broadcasted_iota(jnp.int32, (L, L), 0)
    cl = lax.broadcasted_iota(jnp.int32, (L, L), 1)
    incl_bf = jnp.where((cl - rl) * sgn <= 0, 1.0, 0.0).astype(BF16)
    blk_r, blk_c = rl // SUB, cl // SUB
    first_r = blk_r * SUB + jnp.where(fwd, 0, SUB - 1)
    sel_first = jnp.where(cl == first_r, 1.0, 0.0).astype(BF16)
    blk_before = (blk_c - blk_r) * sgn < 0
    same_blk = blk_c == blk_r
    at_or_before = (cl - rl) * sgn <= 0
    ones_nl = jnp.ones((N, L), BF16)

    def stack(x):
        return jnp.stack([x[:, p * N:(p + 1) * N] for p in range(P)], axis=0)

    def within_block(x, j):
        return jnp.concatenate(
            [jnp.broadcast_to(x[:, i * SUB + j:i * SUB + j + 1, :], (P, SUB, N)) for i in range(L // SUB)], axis=1)

    lb = lb_ref[...]

    for ci in range(nch):
        t0 = pl.multiple_of(jnp.where(fwd, ci * L, (nch - 1 - ci) * L), L)
        rows = pl.ds(t0, L)
        qv = q_ref[rows, :].astype(F32)
        q = qv * jax.nn.sigmoid(qv) * (N ** -0.5)
        sig = jax.nn.sigmoid(zf_ref[rows, :].astype(F32) + fb_ref[0])
        lf = jnp.log(lb + (1.0 - lb) * sig)
        kx = (1.0 - lb) * (1.0 - sig)
        vb = stack(v_ref[rows, :])
        b = _dot_exact_lhs(incl_bf, lf)
        bend = jnp.sum(lf, axis=0, keepdims=True)
        q_dec = stack(q * jnp.exp(b)).astype(BF16)

        def chunk_factored():
            full = lax.dot_general(q_dec, stack(kx * jnp.exp(-b)).astype(BF16),
                                   (((2,), (2,)), ((0,), (0,))), preferred_element_type=F32)
            return jnp.where(at_or_before, full, 0.0)

        def block_factored():
            ref = _dot_exact_lhs(sel_first, b - lf)
            qh = stack(q * jnp.exp(b - ref)).astype(BF16)
            rows_att = []
            for i in range(L // SUB):
                ki = kx * jnp.exp(jnp.minimum(ref[i * SUB:i * SUB + 1, :] - b, 0.0))
                rows_att.append(lax.dot_general(qh[:, i * SUB:(i + 1) * SUB], stack(ki).astype(BF16),
                                                (((2,), (2,)), ((0,), (0,))), preferred_element_type=F32))
            att_off = jnp.where(blk_before, jnp.concatenate(rows_att, axis=1), 0.0)
            decay_in_block = ref - b

            def factored_diag():
                kd = stack(kx * jnp.exp(decay_in_block)).astype(BF16)
                full = lax.dot_general(qh, kd, (((2,), (2,)), ((0,), (0,))), preferred_element_type=F32)
                return jnp.where(same_blk & at_or_before, full, att_off)

            def pairwise_diag():
                att = att_off
                bs, ks, qs = stack(b * LOG2_E), stack(kx), stack(q)
                for j in range(SUB):
                    pj = (qs * within_block(ks, j)
                          * jnp.exp2(jnp.minimum(bs - within_block(bs, j), 0.0))).astype(BF16)
                    tot = jnp.dot(pj.reshape(P * L, N), ones_nl, preferred_element_type=F32).reshape(P, L, L)
                    att = jnp.where((cl == blk_r * SUB + j) & at_or_before, tot, att)
                return att

            return lax.cond(jnp.max(decay_in_block) < HG_SAFE_EXP, factored_diag, pairwise_diag)

        att = lax.cond(jnp.max(-b) < HG_SAFE_EXP, chunk_factored, block_factored)
        s0 = s_ref[...]
        o = lax.dot_general(att.astype(BF16), vb, (((2,), (1,)), ((0,), (0,))), preferred_element_type=F32)
        o = o + lax.dot_general(q_dec, s0.astype(BF16), (((2,), (2,)), ((0,), (0,))), preferred_element_type=F32)
        for p in range(P):
            o_ref[0, rows, p * N:(p + 1) * N] = o[p]
        kend = stack(kx * jnp.exp(bend - b)).astype(BF16)
        s_ref[...] = s0 * stack(jnp.exp(bend)) + lax.dot_general(
            vb, kend, (((1,), (1,)), ((0,), (0,))), preferred_element_type=F32)


def hgrn_scan(z, f_bias, lb, B, S):
    T, D5 = z.shape
    D = D5 // 5
    tb = _tile(S, (HG_BLOCK_T, HG_CHUNK))
    hw = _tile(D, (HG_BLOCK_H * HG_DK, 4 * HG_DK, 2 * HG_DK, HG_DK))
    nt, nh = S // tb, D // hw

    def trow(b, d, c):
        return b * nt + jnp.where(d == 0, c, nt - 1 - c)

    return pl.pallas_call(
        _hgrn_scan_kernel,
        grid=(B, nh, 2, nt),
        in_specs=[pl.BlockSpec((tb, hw), lambda b, g, d, c: (trow(b, d, c), g)),
                  pl.BlockSpec((tb, hw), lambda b, g, d, c: (trow(b, d, c), (1 + d) * nh + g)),
                  pl.BlockSpec((tb, hw), lambda b, g, d, c: (trow(b, d, c), 3 * nh + g)),
                  pl.BlockSpec((1, 1, hw), lambda b, g, d, c: (d, 0, g)),
                  pl.BlockSpec((1, hw), lambda b, g, d, c: (0, g))],
        out_specs=pl.BlockSpec((1, tb, hw), lambda b, g, d, c: (d, trow(b, d, c), g)),
        out_shape=jax.ShapeDtypeStruct((2, T, D), F32),
        scratch_shapes=[pltpu.VMEM((hw // HG_DK, HG_DK, HG_DK), F32)],
        compiler_params=_cparams("parallel", "parallel", "arbitrary", "arbitrary"),
        name="hgrn_scan",
    )(z, z, z, f_bias.reshape(2, 1, D), lb.reshape(1, D))


def _head_out_kernel(y_ref, g_ref, gain_ref, o_ref, *, head, act):
    y = y_ref[0] + y_ref[1]
    g = g_ref[...].astype(F32)
    gain = gain_ref[...]
    for p in range(y.shape[1] // head):
        cols = slice(p * head, (p + 1) * head)
        yp = y[:, cols]
        yn = yp * lax.rsqrt(jnp.mean(yp * yp, axis=-1, keepdims=True) + NORM_EPS) * gain[:, cols]
        o_ref[:, cols] = (yn * act(g[:, cols])).astype(o_ref.dtype)


def head_out(y, z, gate_col, gain, head, act):
    _, T, D = y.shape
    tm = _tile(T, (512, 256, 128, 64, 32, 16, 8))
    w = _tile(D, (1024, 512, 256, 128))
    w = max(w, head)
    assert gate_col % w == 0
    gb = gate_col // w
    return pl.pallas_call(
        functools.partial(_head_out_kernel, head=head, act=act),
        grid=(T // tm, D // w),
        in_specs=[pl.BlockSpec((2, tm, w), lambda i, j: (0, i, j)),
                  pl.BlockSpec((tm, w), lambda i, j: (i, gb + j)),
                  pl.BlockSpec((1, w), lambda i, j: (0, j))],
        out_specs=pl.BlockSpec((tm, w), lambda i, j: (i, j)),
        out_shape=jax.ShapeDtypeStruct((T, D), BF16),
        compiler_params=_cparams("parallel", "parallel"),
        name="head_out",
    )(y, z, gain.reshape(1, D))


def hgrn2_layer(h, hn, w_in, f_bias, lb_logits, layer_idx, head_gain, w_out, B, S, lj=0):
    T, D = h.shape
    z = matmul(hn, w_in, lj, out_dtype=BF16)
    probs = jax.nn.softmax(lb_logits, axis=0)
    lb = (jnp.cumsum(probs, axis=0) - probs[0])[layer_idx]
    o = hgrn_scan(z, f_bias, lb, B, S)
    gated = head_out(o, z, 4 * D, head_gain, HG_DK, jax.nn.silu)
    return matmul(gated, w_out, lj, residual=h)


def kernel(x, p, norm_mix, norm_ffn, norm_ple_gate, norm_ple_post, norm_final, ml_w_in, ml_gate_bias, ml_conv, ml_head_gain, ml_w_out, rw_mu, rw_w_rkv, rw_w0, rw_w1, rw_w2, rw_a0, rw_a1, rw_a2, rw_g1, rw_g2, rw_k_k, rw_k_a, rw_r_k, rw_ln_gain, rw_ln_bias, rw_w_o, hg_w_in, hg_f_bias, hg_lb, hg_head_gain, hg_w_out, moe_router, moe_w_gate, moe_w_up, moe_w_down, ple_w, ple_gate_down, ple_gate_up):
    B, S, D = x.shape
    depth = p.shape[0]
    T = B * S
    h = x.reshape(T, D)
    hn = rms_norm_bf16(h, norm_mix[0])
    for i in range(depth):
        kind, j = i % 3, i // 3
        if kind == 0:
            h = mlstm_layer(h, hn, ml_w_in, ml_gate_bias[j], ml_conv[j], ml_head_gain[j], ml_w_out, B, S, j)
        elif kind == 1:
            h = rwkv_layer(h, hn, rw_mu[j], rw_w_rkv, rw_w0[j], rw_w1[j], rw_w2[j], rw_a0[j], rw_a1[j],
                           rw_a2[j], rw_g1[j], rw_g2[j], rw_k_k[j], rw_k_a[j], rw_r_k[j], rw_ln_gain[j],
                           rw_ln_bias[j], rw_w_o, B, S, j)
        else:
            h = hgrn2_layer(h, hn, hg_w_in, hg_f_bias[j], hg_lb, i, hg_head_gain[j], hg_w_out, B, S, j)
        h = moe_layer(h, norm_ffn[i], moe_router, moe_w_gate, moe_w_up, moe_w_down, i, B, S)
        last = i + 1 == depth
        h, hn = ple_layer(h, p[i].reshape(T, -1), ple_w[i], ple_gate_down[i], ple_gate_up[i], norm_ple_gate[i],
                          norm_ple_post[i], norm_final if last else norm_mix[i + 1], F32 if last else BF16)
    return hn.reshape(B, S, D)
```

```python
import functools

import jax
import jax.numpy as jnp
from jax import lax
from jax.experimental import pallas as pl
from jax.experimental.pallas import tpu as pltpu

F32 = jnp.float32
BF16 = jnp.bfloat16

NORM_EPS = 1e-6
GATE_CAP = 15.0
ML_CHUNK = 128
RW_HEAD = 64
RW_GN_EPS = 64e-5
HG_DK = 128
HG_CHUNK = 64
EC_CAPACITY = 2

LANES = 128
SUBLANES = 8
VMEM_LIMIT = 52 * 1024 * 1024


def _cparams(*sem):
    return pltpu.CompilerParams(dimension_semantics=sem, vmem_limit_bytes=VMEM_LIMIT)


def _tile(n, prefs):
    for t in prefs:
        if n % t == 0:
            return t
    return n


def _rms(x, gain):
    return x * lax.rsqrt(jnp.mean(x * x, axis=-1, keepdims=True) + NORM_EPS) * gain


def _norm_kernel(h_ref, g_ref, o_ref):
    o_ref[...] = _rms(h_ref[...], g_ref[...]).astype(o_ref.dtype)


def rms_norm_bf16(h, gain):
    T, D = h.shape
    tm = _tile(T, (256, 128, 64, 32, 16, 8))
    return pl.pallas_call(
        _norm_kernel,
        grid=(T // tm,),
        in_specs=[pl.BlockSpec((tm, D), lambda i: (i, 0)), pl.BlockSpec((1, D), lambda i: (0, 0))],
        out_specs=pl.BlockSpec((tm, D), lambda i: (i, 0)),
        out_shape=jax.ShapeDtypeStruct((T, D), BF16),
        compiler_params=_cparams("parallel"),
        name="rms_norm",
    )(h, gain.reshape(1, D))


def _router_kernel(h_ref, g_ref, r_ref, l_ref):
    yh, yl = _split(_rms(h_ref[...], g_ref[...]))
    rh, rl = _split(r_ref[0])
    l_ref[...] = _bdot(yh, rh) + (_bdot(yh, rl) + _bdot(yl, rh))


def router_logits(h, gain, router, layer):
    T, D = h.shape
    E = router.shape[2]
    rp = jnp.pad(router, ((0, 0), (0, 0), (0, LANES - E)))
    tm = _tile(T, (256, 128, 64, 32, 16, 8))
    logits = pl.pallas_call(
        _router_kernel,
        grid=(T // tm,),
        in_specs=[pl.BlockSpec((tm, D), lambda i: (i, 0)), pl.BlockSpec((1, D), lambda i: (0, 0)),
                  pl.BlockSpec((1, D, LANES), lambda i: (layer, 0, 0))],
        out_specs=pl.BlockSpec((tm, LANES), lambda i: (i, 0)),
        out_shape=jax.ShapeDtypeStruct((T, LANES), F32),
        compiler_params=_cparams("parallel"),
        name="router_logits",
    )(h, gain.reshape(1, D), rp)
    return logits[:, :E]


_CAST_ROWS = 256


def _cast_weight(w_ref, wb_ref):
    k = w_ref.shape[0]
    rows = _tile(k, (_CAST_ROWS, 128, 64, 32, 16))

    def body(c, _):
        r = pl.multiple_of(c * rows, rows)
        wb_ref[pl.ds(r, rows), :] = w_ref[pl.ds(r, rows), :].astype(BF16)
        return 0

    lax.fori_loop(0, k // rows, body, 0)


def _mm_kernel(x_ref, w_ref, *rest, has_res, has_bias, act, w_is_nk):
    rest = list(rest)
    b_ref = rest.pop(0) if has_bias else None
    r_ref = rest.pop(0) if has_res else None
    o_ref, wb_ref = rest

    @pl.when(pl.program_id(1) == 0)
    def _():
        _cast_weight(w_ref.at[0], wb_ref)

    contract = (((1,), (1,)), ((), ())) if w_is_nk else (((1,), (0,)), ((), ()))
    acc = lax.dot_general(x_ref[...], wb_ref[...], contract, preferred_element_type=F32)
    if has_bias:
        acc = acc + b_ref[...]
    if act is not None:
        acc = act(acc)
    if has_res:
        acc = acc + r_ref[...]
    o_ref[...] = acc.astype(o_ref.dtype)


def matmul(x, w, layer=0, *, n=None, col_off=0, bias=None, residual=None, act=None, out_dtype=F32, tm=None, tn=None,
           w_is_nk=False):
    M, K = x.shape
    if w.ndim == 2:
        w = w[None]
    if n is None:
        n = w.shape[1 if w_is_nk else 2] - col_off
    tm = tm or _tile(M, (1024, 512, 256, 128, 64, 32, 16))
    tn = tn or _tile(n, (512, 256, 128))
    assert n % tn == 0 and col_off % tn == 0 and M % tm == 0
    cb = col_off // tn
    if w_is_nk:
        w_spec = pl.BlockSpec((1, tn, K), lambda j, i: (layer, j + cb, 0))
    else:
        w_spec = pl.BlockSpec((1, K, tn), lambda j, i: (layer, 0, j + cb))
    in_specs = [pl.BlockSpec((tm, K), lambda j, i: (i, 0)), w_spec]
    args = [x, w]
    if bias is not None:
        in_specs.append(pl.BlockSpec((1, tn), lambda j, i: (0, j)))
        args.append(bias.reshape(1, n))
    if residual is not None:
        in_specs.append(pl.BlockSpec((tm, tn), lambda j, i: (i, j)))
        args.append(residual)
    return pl.pallas_call(
        functools.partial(_mm_kernel, has_res=residual is not None, has_bias=bias is not None, act=act,
                          w_is_nk=w_is_nk),
        grid=(n // tn, M // tm),
        in_specs=in_specs,
        out_specs=pl.BlockSpec((tm, tn), lambda j, i: (i, j)),
        out_shape=jax.ShapeDtypeStruct((M, n), out_dtype),
        scratch_shapes=[pltpu.VMEM((tn, K) if w_is_nk else (K, tn), BF16)],
        compiler_params=_cparams("arbitrary", "arbitrary"),
        name="matmul",
    )(*args)


def _lora_kernel(x_ref, a_ref, b_ref, bias_ref, o_ref, ab_ref, bb_ref, *, act1, act2):
    @pl.when(pl.program_id(0) == 0)
    def _():
        _cast_weight(a_ref, ab_ref)
        _cast_weight(b_ref, bb_ref)

    t = jnp.dot(x_ref[...], ab_ref[...], preferred_element_type=F32)
    if act1 is not None:
        t = act1(t)
    y = jnp.dot(t.astype(BF16), bb_ref[...], preferred_element_type=F32) + bias_ref[...]
    if act2 is not None:
        y = act2(y)
    o_ref[...] = y.astype(o_ref.dtype)


def lora(x, a, b, bias=None, act1=None, act2=None, out_dtype=F32, tm=256):
    M, K = x.shape
    R, N = b.shape
    rp = -R % LANES
    if rp:
        a = jnp.pad(a, ((0, 0), (0, rp)))
        b = jnp.pad(b, ((0, rp), (0, 0)))
        R += rp
    if bias is None:
        bias = jnp.zeros((N,), F32)
    tm = _tile(M, (tm, 128, 64, 32, 16))
    return pl.pallas_call(
        functools.partial(_lora_kernel, act1=act1, act2=act2),
        grid=(M // tm,),
        in_specs=[pl.BlockSpec((tm, K), lambda i: (i, 0)), pl.BlockSpec((K, R), lambda i: (0, 0)),
                  pl.BlockSpec((R, N), lambda i: (0, 0)), pl.BlockSpec((1, N), lambda i: (0, 0))],
        out_specs=pl.BlockSpec((tm, N), lambda i: (i, 0)),
        out_shape=jax.ShapeDtypeStruct((M, N), out_dtype),
        scratch_shapes=[pltpu.VMEM((K, R), BF16), pltpu.VMEM((R, N), BF16)],
        compiler_params=_cparams("arbitrary"),
        name="lora",
    )(x, a, b, bias.reshape(1, N))


def _ple_kernel(h_ref, p_ref, wp_ref, gd_ref, gu_ref, gg_ref, pg_ref, ng_ref, ho_ref, no_ref,
                wpb_ref, gdb_ref, gub_ref):
    @pl.when(pl.program_id(0) == 0)
    def _():
        _cast_weight(wp_ref, wpb_ref)
        _cast_weight(gd_ref, gdb_ref)
        _cast_weight(gu_ref, gub_ref)

    h = h_ref[...]
    e = jnp.dot(p_ref[...].astype(BF16), wpb_ref[...], preferred_element_type=F32)
    e = _rms(e, pg_ref[...])
    t = jnp.dot(_rms(h, gg_ref[...]).astype(BF16), gdb_ref[...], preferred_element_type=F32)
    g = jnp.dot(t.astype(BF16), gub_ref[...], preferred_element_type=F32)
    hn = h + e * jax.nn.sigmoid(g)
    ho_ref[...] = hn
    no_ref[...] = _rms(hn, ng_ref[...]).astype(no_ref.dtype)


def ple_layer(h, p, w_ple, g_down, g_up, gate_gain, post_gain, next_gain, next_dtype):
    T, D = h.shape
    P = p.shape[1]
    tm = _tile(T, (256, 128, 64, 32, 16, 8))
    row = lambda i: (i, 0)
    fix = lambda i: (0, 0)
    return pl.pallas_call(
        _ple_kernel,
        grid=(T // tm,),
        in_specs=[pl.BlockSpec((tm, D), row), pl.BlockSpec((tm, P), row), pl.BlockSpec((P, D), fix),
                  pl.BlockSpec((D, P), fix), pl.BlockSpec((P, D), fix), pl.BlockSpec((1, D), fix),
                  pl.BlockSpec((1, D), fix), pl.BlockSpec((1, D), fix)],
        out_specs=[pl.BlockSpec((tm, D), row), pl.BlockSpec((tm, D), row)],
        out_shape=[jax.ShapeDtypeStruct((T, D), F32), jax.ShapeDtypeStruct((T, D), next_dtype)],
        scratch_shapes=[pltpu.VMEM((P, D), BF16), pltpu.VMEM((D, P), BF16), pltpu.VMEM((P, D), BF16)],
        compiler_params=_cparams("arbitrary"),
        name="ple",
    )(h, p, w_ple, g_down, g_up, gate_gain.reshape(1, D), post_gain.reshape(1, D), next_gain.reshape(1, D))


MOE_NORM_ROWS = 256
DMA_ISSUE_UNROLL = 8


def _moe_up_kernel(tok_ref, nxt_ref, h_ref, gain_ref, wg_ref, wu_ref, o_ref, xg_ref, xb_ref, wgb_ref, wub_ref, sem,
                   *, nf):
    rows = xb_ref.shape[0]
    padded = xg_ref.shape[0]
    part = padded // nf
    e = pl.program_id(0)
    f = pl.program_id(1)

    def copy_row(idx_ref, i):
        return pltpu.make_async_copy(h_ref.at[pl.ds(idx_ref[0, 0, i], 1), :], xg_ref.at[pl.ds(i, 1), :], sem)

    def wait_all():
        pltpu.make_async_copy(h_ref.at[pl.ds(0, padded), :], xg_ref, sem).wait()

    @pl.when(f == 0)
    def _():
        @pl.when(e == 0)
        def _():
            def issue(i, _):
                copy_row(tok_ref, i).start()
                return 0

            lax.fori_loop(0, padded, issue, 0, unroll=DMA_ISSUE_UNROLL)

        wait_all()
        step = _tile(rows, (MOE_NORM_ROWS, 128, 64, 32, 16))

        def norm(c, _):
            r = pl.multiple_of(c * step, step)
            xb_ref[pl.ds(r, step), :] = _rms(xg_ref[pl.ds(r, step), :], gain_ref[...]).astype(BF16)
            return 0

        lax.fori_loop(0, rows // step, norm, 0)

    _cast_weight(wg_ref.at[0, 0], wgb_ref)
    _cast_weight(wu_ref.at[0, 0], wub_ref)
    for j in range(part):
        copy_row(nxt_ref, f * part + j).start()
    x = xb_ref[...]
    g = jnp.dot(x, wgb_ref[...], preferred_element_type=F32)
    u = jnp.dot(x, wub_ref[...], preferred_element_type=F32)
    o_ref[0] = (g * jax.nn.sigmoid(g) * u).astype(o_ref.dtype)

    @pl.when((e == pl.num_programs(0) - 1) & (f == nf - 1))
    def _():
        wait_all()


def _moe_down_kernel(h_ref, wd_ref, s_ref, o_ref, wdb_ref):
    _cast_weight(wd_ref.at[0, 0], wdb_ref)
    o_ref[0] = jnp.dot(h_ref[0], wdb_ref[...], preferred_element_type=F32) * s_ref[0]


def moe_ffn(h, gain, tok, gsel, w_gate, w_up, w_down, layer):
    T, D = h.shape
    E, R = tok.shape
    FF = w_gate.shape[3]
    tf = _tile(FF, (256, 128))
    nf = FF // tf
    part = -(-R // (nf * SUBLANES)) * SUBLANES
    padded = nf * part
    assert padded <= T
    tokp = jnp.pad(tok, ((0, 0), (0, padded - R))).reshape(E, 1, padded)
    hid = pl.pallas_call(
        functools.partial(_moe_up_kernel, nf=nf),
        grid=(E, nf),
        in_specs=[pl.BlockSpec((1, 1, padded), lambda e, f: (e, 0, 0), memory_space=pltpu.SMEM),
                  pl.BlockSpec((1, 1, padded), lambda e, f: (jnp.minimum(e + 1, E - 1), 0, 0),
                               memory_space=pltpu.SMEM),
                  pl.BlockSpec(memory_space=pl.ANY),
                  pl.BlockSpec((1, D), lambda e, f: (0, 0)),
                  pl.BlockSpec((1, 1, D, tf), lambda e, f: (layer, e, 0, f)),
                  pl.BlockSpec((1, 1, D, tf), lambda e, f: (layer, e, 0, f))],
        out_specs=pl.BlockSpec((1, R, tf), lambda e, f: (e, 0, f)),
        out_shape=jax.ShapeDtypeStruct((E, R, FF), BF16),
        scratch_shapes=[pltpu.VMEM((padded, D), F32), pltpu.VMEM((R, D), BF16),
                        pltpu.VMEM((D, tf), BF16), pltpu.VMEM((D, tf), BF16), pltpu.SemaphoreType.DMA(())],
        compiler_params=_cparams("arbitrary", "arbitrary"),
        name="moe_up",
    )(tokp, tokp, h, gain.reshape(1, D), w_gate, w_up)
    tn = _tile(D, (2048, 1024, 512, 256, 128))
    return pl.pallas_call(
        _moe_down_kernel,
        grid=(E, D // tn),
        in_specs=[pl.BlockSpec((1, R, FF), lambda e, j: (e, 0, 0)),
                  pl.BlockSpec((1, 1, FF, tn), lambda e, j: (layer, e, 0, j)),
                  pl.BlockSpec((1, R, 1), lambda e, j: (e, 0, 0))],
        out_specs=pl.BlockSpec((1, R, tn), lambda e, j: (e, 0, j)),
        out_shape=jax.ShapeDtypeStruct((E, R, D), F32),
        scratch_shapes=[pltpu.VMEM((FF, tn), BF16)],
        compiler_params=_cparams("parallel", "parallel"),
        name="moe_down",
    )(hid, w_down, gsel.reshape(E, R, 1))


MOE_TILE = 256
MOE_CHUNK = 256


def _moe_combine_kernel(tile_ref, chunk_ref, next_ref, flag_ref, h_ref, cur_ref, nxt_ref, tok_ref, y_ref, o_ref,
                        buf_ref, sem):
    w = pl.program_id(0)
    nw = pl.num_programs(0)
    ch = buf_ref.shape[1]
    tm = h_ref.shape[0]
    slot = w % 2

    def copy_row(idx_ref, s, j):
        return pltpu.make_async_copy(y_ref.at[pl.ds(idx_ref[0, 0, j], 1), :], buf_ref.at[s, pl.ds(j, 1), :],
                                     sem.at[s])

    def wait_buffer(s):
        pltpu.make_async_copy(y_ref.at[pl.ds(0, ch), :], buf_ref.at[s], sem.at[s]).wait()

    @pl.when(w == 0)
    def _():
        def issue(j, _):
            copy_row(cur_ref, 0, j).start()
            return 0

        lax.fori_loop(0, ch, issue, 0, unroll=DMA_ISSUE_UNROLL)

    wait_buffer(slot)
    flags = flag_ref[w]

    @pl.when(flags >= 2)
    def _():
        o_ref[...] = h_ref[...]

    for j in range(ch):
        copy_row(nxt_ref, 1 - slot, j).start()
    local = jnp.where(flags % 2 == 1, tok_ref[0] - tile_ref[w] * tm, -1)
    rows = lax.broadcasted_iota(jnp.int32, (tm, ch), 0)
    onehot = jnp.where(rows == local, 1.0, 0.0).astype(BF16)
    o_ref[...] += _dot_exact_lhs(onehot, buf_ref[slot])

    @pl.when(w == nw - 1)
    def _():
        wait_buffer(1 - slot)


def moe_combine(h, y, tok):
    T, D = h.shape
    n = tok.shape[0]
    tm = _tile(T, (MOE_TILE, 128, 64, 32, 16, 8))
    ch = _tile(n, (MOE_CHUNK, 128))
    nt, nch = T // tm, n // ch
    order = jnp.argsort(tok).astype(jnp.int32)
    stok = tok[order]
    edges = jnp.arange(nt + 1, dtype=jnp.int32) * tm
    starts = jnp.sum((tok[None, :] < edges[:, None]).astype(jnp.int32), axis=1)
    g_lo = jnp.minimum(starts[:-1] // ch, nch - 1)
    g_hi = jnp.clip((starts[1:] - 1) // ch, g_lo, nch - 1)
    cnt = g_hi - g_lo + 1
    off = jnp.cumsum(cnt)
    nw = nch + nt
    wi = jnp.arange(nw, dtype=jnp.int32)
    tile = jnp.minimum(jnp.sum((off[None, :] <= wi[:, None]).astype(jnp.int32), axis=1), nt - 1)
    first_w = (off - cnt)[tile]
    valid = wi < off[-1]
    chunk = jnp.where(valid, g_lo[tile] + wi - first_w, nch - 1).astype(jnp.int32)
    flags = (valid.astype(jnp.int32) + 2 * (valid & (wi == first_w)).astype(jnp.int32))
    nxt = jnp.concatenate([chunk[1:], chunk[-1:]])
    idx3 = order.reshape(nch, 1, ch)
    grid_spec = pltpu.PrefetchScalarGridSpec(
        num_scalar_prefetch=4,
        grid=(nw,),
        in_specs=[pl.BlockSpec((tm, D), lambda w, t, c, x, f: (t[w], 0)),
                  pl.BlockSpec((1, 1, ch), lambda w, t, c, x, f: (c[w], 0, 0), memory_space=pltpu.SMEM),
                  pl.BlockSpec((1, 1, ch), lambda w, t, c, x, f: (x[w], 0, 0), memory_space=pltpu.SMEM),
                  pl.BlockSpec((1, 1, ch), lambda w, t, c, x, f: (c[w], 0, 0)),
                  pl.BlockSpec(memory_space=pl.ANY)],
        out_specs=pl.BlockSpec((tm, D), lambda w, t, c, x, f: (t[w], 0)),
        scratch_shapes=[pltpu.VMEM((2, ch, D), F32), pltpu.SemaphoreType.DMA((2,))],
    )

    return pl.pallas_call(
        _moe_combine_kernel,
        grid_spec=grid_spec,
        out_shape=jax.ShapeDtypeStruct((T, D), F32),
        compiler_params=_cparams("arbitrary"),
        name="moe_combine",
    )(tile, chunk, nxt, flags, h, idx3, idx3, stok.reshape(nch, 1, ch), y)


def moe_layer(h, gain, router, w_gate, w_up, w_down, layer, B, S):
    T, D = h.shape
    E = router.shape[2]
    cap = EC_CAPACITY * S // E
    logits = router_logits(h, gain, router, layer)
    aff = jax.nn.softmax(logits, axis=-1).reshape(B, S, E)
    gsel, idx = lax.top_k(jnp.swapaxes(aff, 1, 2), cap)
    tok = idx + (jnp.arange(B, dtype=idx.dtype) * S)[:, None, None]
    tok = jnp.swapaxes(tok, 0, 1).reshape(E, B * cap)
    gsel = jnp.swapaxes(gsel, 0, 1).reshape(E, B * cap)
    out = moe_ffn(h, gain, tok, gsel, w_gate, w_up, w_down, layer)
    return moe_combine(h, out.reshape(-1, D), tok.reshape(-1))


def _softplus(x):
    return jnp.maximum(x, 0.0) + jnp.log(1.0 + jnp.exp(-jnp.abs(x)))


def _dwconv3_kernel(x_ref, w_ref, o_ref):
    x = x_ref[0].astype(F32)
    s = x.shape[0]
    t = lax.broadcasted_iota(jnp.int32, x.shape, 0)
    prev = jnp.where(t == 0, 0.0, pltpu.roll(x, 1, 0))
    nxt = jnp.where(t == s - 1, 0.0, pltpu.roll(x, s - 1, 0))
    o_ref[0] = (w_ref[0:1, :] * prev + w_ref[1:2, :] * x + w_ref[2:3, :] * nxt).astype(o_ref.dtype)


def dwconv3(x, w):
    B, S, _ = x.shape
    assert w.shape[0] == 3
    C = w.shape[1]
    blk = pl.BlockSpec((1, S, LANES), lambda b, j: (b, 0, j))
    return pl.pallas_call(
        _dwconv3_kernel,
        grid=(B, C // LANES),
        in_specs=[blk, pl.BlockSpec((3, LANES), lambda b, j: (0, j))],
        out_specs=blk,
        out_shape=jax.ShapeDtypeStruct((B, S, C), BF16),
        compiler_params=_cparams("parallel", "parallel"),
        name="dwconv3",
    )(x, w)


ML_EXT = LANES
ML_BLOCK_H = 8


def _mlstm_scan_kernel(q_ref, k_ref, v_ref, g_ref, gb_ref, o_ref, c_ref, m_ref, *, heads):
    hg = pl.program_id(1)
    d = pl.program_id(2)
    L = q_ref.shape[0]
    hb = c_ref.shape[0]
    dk = q_ref.shape[1] // hb
    dv = v_ref.shape[1] // hb

    @pl.when(pl.program_id(3) == 0)
    def _():
        c_ref[...] = jnp.zeros_like(c_ref)
        m_ref[...] = jnp.full_like(m_ref, -1e30)

    sgn = jnp.where(d == 0, 1, -1)
    rl = lax.broadcasted_iota(jnp.int32, (L, L), 0)
    cl = lax.broadcasted_iota(jnp.int32, (L, L), 1)
    incl = (cl - rl) * sgn <= 0
    incl_t = (rl - cl) * sgn <= 0
    eye = rl == cl

    g = g_ref[...] + gb_ref[...]
    g = GATE_CAP * jnp.tanh(g / GATE_CAP)
    lane = lax.broadcasted_iota(jnp.int32, g.shape, 1)
    ones_col = jnp.where(lax.broadcasted_iota(jnp.int32, (L, ML_EXT), 1) == 0, 1.0, 0.0).astype(BF16)

    for hh in range(hb):
        i_idx = d * 2 * heads + hg * hb + hh
        i_col = jnp.sum(jnp.where(lane == i_idx, g, 0.0), axis=1, keepdims=True)
        f_col = -_softplus(-jnp.sum(jnp.where(lane == i_idx + heads, g, 0.0), axis=1, keepdims=True))
        i_row = jnp.sum(jnp.where(eye, i_col, 0.0), axis=0, keepdims=True)
        f_row = jnp.sum(jnp.where(eye, f_col, 0.0), axis=0, keepdims=True)
        b_col = jnp.sum(jnp.where(incl, f_row, 0.0), axis=1, keepdims=True)
        b_row = jnp.sum(jnp.where(incl_t, f_col, 0.0), axis=0, keepdims=True)
        gtot = jnp.sum(f_col, axis=0, keepdims=True)
        m = m_ref[hh, 0:1, 0:1]

        dm = b_col - b_row + i_row
        inter_log = b_col + m
        m_t = jnp.maximum(inter_log, jnp.max(jnp.where(incl, dm, -jnp.inf), axis=1, keepdims=True))
        pmat = jnp.where(incl, jnp.exp(jnp.where(incl, dm - m_t, 0.0)), 0.0)
        w_inter = jnp.exp(inter_log - m_t)

        q = q_ref[:, hh * dk:(hh + 1) * dk]
        k = k_ref[:, hh * dk:(hh + 1) * dk]
        s = lax.dot_general(q, k, (((1,), (1,)), ((), ())), preferred_element_type=F32) * pmat
        vext = jnp.concatenate([v_ref[:, hh * dv:(hh + 1) * dv], ones_col], axis=1)
        c = c_ref[hh]
        tot = (jnp.dot(s.astype(BF16), vext, preferred_element_type=F32)
               + w_inter * jnp.dot(q, c.astype(BF16), preferred_element_type=F32))
        den = tot[:, dv:dv + 1]
        o_ref[0, :, hh * dv:(hh + 1) * dv] = tot[:, :dv] / jnp.maximum(jnp.abs(den), jnp.exp(-m_t))

        src = gtot - b_col + i_col
        m_new = jnp.maximum(gtot + m, jnp.max(src, axis=0, keepdims=True))
        kw = (k.astype(F32) * jnp.exp(src - m_new)).astype(BF16)
        c_ref[hh] = jnp.exp(gtot + m - m_new) * c + lax.dot_general(
            kw, vext, (((0,), (0,)), ((), ())), preferred_element_type=F32)
        m_ref[hh] = jnp.broadcast_to(m_new, (1, LANES))


def mlstm_scan(qk, z, gates, gate_bias, heads, B, S):
    T, QK2 = qk.shape
    L = ML_CHUNK
    dk = QK2 // 2 // heads
    V = (z.shape[1] - QK2) // 2
    dv = V // heads
    nc = S // L
    hb = _tile(heads, (ML_BLOCK_H, 1))
    ng = heads // hb

    def trow(b, d, c):
        return b * nc + jnp.where(d == 0, c, nc - 1 - c)

    return pl.pallas_call(
        functools.partial(_mlstm_scan_kernel, heads=heads),
        grid=(B, ng, 2, nc),
        in_specs=[pl.BlockSpec((L, hb * dk), lambda b, h, d, c: (trow(b, d, c), h)),
                  pl.BlockSpec((L, hb * dk), lambda b, h, d, c: (trow(b, d, c), ng + h)),
                  pl.BlockSpec((L, hb * dv), lambda b, h, d, c: (trow(b, d, c), QK2 // (hb * dv) + h)),
                  pl.BlockSpec((L, LANES), lambda b, h, d, c: (trow(b, d, c), 0)),
                  pl.BlockSpec((1, LANES), lambda b, h, d, c: (0, 0))],
        out_specs=pl.BlockSpec((1, L, hb * dv), lambda b, h, d, c: (d, trow(b, d, c), h)),
        out_shape=jax.ShapeDtypeStruct((2, T, V), F32),
        scratch_shapes=[pltpu.VMEM((hb, dk, dv + ML_EXT), F32), pltpu.VMEM((hb, 1, LANES), F32)],
        compiler_params=_cparams("parallel", "parallel", "arbitrary", "arbitrary"),
        name="mlstm_scan",
    )(qk, qk, z, gates, gate_bias.reshape(1, LANES))


def mlstm_layer(h, hn, w_in, gate_bias, conv_w, head_gain, w_out, B, S, lj=0):
    T, D = h.shape
    H = gate_bias.shape[0] // 4
    QK = conv_w.shape[1] // 2
    V = w_out.shape[-2]
    DK, DV = QK // H, V // H
    nmain = 2 * QK + 2 * V
    w_t = jnp.swapaxes(w_in if w_in.ndim == 3 else w_in[None], 1, 2)
    z = matmul(hn, w_t, lj, n=nmain, out_dtype=BF16, w_is_nk=True)
    pad = LANES - 4 * H
    w_gates = lax.slice(w_t, (lj, nmain, 0), (lj + 1, nmain + 4 * H, D))[0]
    gates = matmul(hn, jnp.pad(w_gates, ((0, pad), (0, 0))), w_is_nk=True)
    taps = conv_w * jnp.concatenate([jnp.full((QK,), DK ** -0.5, F32), jnp.ones((QK,), F32)])
    qk = dwconv3(z.reshape(B, S, nmain), taps).reshape(T, 2 * QK)
    hs = mlstm_scan(qk, z, gates, jnp.pad(gate_bias, (0, pad)), H, B, S)
    gated = head_out(hs, z, 2 * QK + V, head_gain, DV, jax.nn.sigmoid)
    return matmul(gated, w_out, lj, residual=h)


RW_CHUNK = 64
RW_BLOCK_T = 128
RW_BLOCK_H = 32
assert RW_CHUNK == RW_HEAD


def _bdot(a, b):
    return jnp.dot(a, b, preferred_element_type=F32)


def _split(x):
    hi = x.astype(BF16)
    return hi, (x - hi.astype(F32)).astype(BF16)


def _dot_exact_lhs(m, x):
    xh, xl = _split(x)
    return _bdot(m, xh) + _bdot(m, xl)


def _dot_exact_rhs(x, m):
    xh, xl = _split(x)
    return _bdot(xh, m) + _bdot(xl, m)


def _tri_inverse(a, row, col, eye, mm, bd):
    size = a.shape[1]
    t = eye + jnp.where((row // 2) == (col // 2), a, 0.0)
    s = 2
    while s < size:
        joins = ((row // (2 * s)) == (col // (2 * s))) & ((row // s) != (col // s))
        t = t + mm(mm(t, bd(jnp.where(joins, a, 0.0))), bd(t))
        s *= 2
    return t


def _rwkv_scan_kernel(r_ref, k_ref, v_ref, wl_ref, a_ref, kk_ref, ka_ref, y_ref, s_ref):
    d = pl.program_id(2)
    L, N = RW_CHUNK, RW_HEAD
    W = 2 * N
    tb, hw = r_ref.shape
    nch, P = tb // L, hw // W

    @pl.when(pl.program_id(3) == 0)
    def _():
        s_ref[...] = jnp.zeros_like(s_ref)

    fwd = d == 0
    sgn = jnp.where(fwd, 1, -1)
    row = lax.broadcasted_iota(jnp.int32, (L, W), 0)
    lane = lax.broadcasted_iota(jnp.int32, (L, W), 1)
    cs = lane % L
    ahead = (cs - row) * sgn
    strict = ahead < 0
    incl = ahead <= 0
    eye = jnp.where(cs == row, 1.0, 0.0)
    low = lane < N
    rl = lax.broadcasted_iota(jnp.int32, (L, L), 0)
    cl = lax.broadcasted_iota(jnp.int32, (L, L), 1)
    incl_bf = jnp.where((cl - rl) * sgn <= 0, 1.0, 0.0).astype(BF16)
    br = lax.broadcasted_iota(jnp.int32, (W, W), 0) // N
    bc = lax.broadcasted_iota(jnp.int32, (W, W), 1) // N
    same = br == bc
    seg = jnp.where(same, 1.0, 0.0).astype(BF16)

    def stack(x):
        return jnp.stack([x[:, p * W:(p + 1) * W] for p in range(P)], axis=0)

    def bd(x):
        return jnp.where(same, jnp.concatenate([x, x], axis=1), 0.0).astype(BF16)

    def mm(a, b):
        return lax.dot_general(a.astype(BF16), b, (((2,), (1,)), ((0,), (0,))), preferred_element_type=F32)

    def mm_nt(a, b):
        return lax.dot_general(a.astype(BF16), b, (((2,), (2,)), ((0,), (0,))), preferred_element_type=F32)

    for ci in range(nch):
        t0 = pl.multiple_of(jnp.where(fwd, ci * L, (nch - 1 - ci) * L), L)
        rows = pl.ds(t0, L)
        r = r_ref[rows, :]
        k = k_ref[rows, :]
        v = v_ref[rows, :]
        rate = a_ref[rows, :]
        logw = -jnp.exp(-_softplus(-wl_ref[rows, :]) - 0.5)
        kk = k * kk_ref[...]
        ssq = kk * kk
        nrm2 = jnp.concatenate([_bdot(ssq[:, p * W:(p + 1) * W].astype(BF16), seg) for p in range(P)], axis=1)
        kk = kk / jnp.maximum(jnp.sqrt(nrm2), 1e-12)
        kd = k * (1.0 + (rate - 1.0) * ka_ref[...])
        bv = kk * rate
        c = _dot_exact_lhs(incl_bf, logw)
        cend = jnp.sum(logw, axis=0, keepdims=True)
        einv = jnp.exp(-c)
        eend = jnp.exp(cend - c)
        at = stack(-kk * jnp.exp(c - logw))
        rt = stack(r * jnp.exp(c))
        vp = stack(v)
        ar = jnp.concatenate([at, rt], axis=1)
        g_b = mm_nt(ar, bd(stack(bv * einv)))
        g_k = mm_nt(ar, bd(stack(kd * einv)))
        aab = jnp.where(strict, g_b[:, :L], 0.0)
        rb = jnp.where(incl, g_b[:, L:], 0.0)
        aak = jnp.where(strict, g_k[:, :L], 0.0)
        rk = jnp.where(incl, g_k[:, L:], 0.0)
        tinv = _tri_inverse(aab, row, cs, eye, mm, bd)
        s0 = s_ref[...]
        from_state = mm_nt(ar, bd(s0))
        from_v = mm(jnp.concatenate([aak, rk], axis=1), bd(vp))
        u = mm(tinv, bd(from_state[:, :L] + from_v[:, :L]))
        y = from_state[:, L:] + mm(rb, bd(u)) + from_v[:, L:]
        for p in range(P):
            y_ref[0, rows, p * W:(p + 1) * W] = y[p]
        uv = jnp.concatenate([u, vp], axis=1).astype(BF16)
        bkh = jnp.concatenate([stack(bv * eend), stack(kd * eend)], axis=1).astype(BF16)
        g = lax.dot_general(uv, bkh, (((1,), (1,)), ((0,), (0,))), preferred_element_type=F32)
        s_ref[...] = s0 * stack(jnp.exp(cend)) + jnp.where(low, g[:, :N], g[:, N:])


def rwkv_scan(r, k, v, wl2, a2, k_k, k_a, B, S):
    T, D = r.shape
    tb = _tile(S, (RW_BLOCK_T, RW_CHUNK))
    hw = _tile(D, (RW_BLOCK_H * RW_HEAD, 4 * RW_HEAD, 2 * RW_HEAD))
    nt, nh = S // tb, D // hw

    def trow(b, d, c):
        return b * nt + jnp.where(d == 0, c, nt - 1 - c)

    shared = pl.BlockSpec((tb, hw), lambda b, g, d, c: (trow(b, d, c), g))
    perdir = pl.BlockSpec((tb, hw), lambda b, g, d, c: (trow(b, d, c), d * nh + g))
    par = pl.BlockSpec((1, hw), lambda b, g, d, c: (0, g))
    return pl.pallas_call(
        _rwkv_scan_kernel,
        grid=(B, nh, 2, nt),
        in_specs=[shared, shared, shared, perdir, perdir, par, par],
        out_specs=pl.BlockSpec((1, tb, hw), lambda b, g, d, c: (d, trow(b, d, c), g)),
        out_shape=jax.ShapeDtypeStruct((2, T, D), F32),
        scratch_shapes=[pltpu.VMEM((hw // (2 * RW_HEAD), RW_HEAD, 2 * RW_HEAD), F32)],
        compiler_params=_cparams("parallel", "parallel", "arbitrary", "arbitrary"),
        name="rwkv_scan",
    )(r, k, v, wl2, a2, k_k.reshape(1, D), k_a.reshape(1, D))


def _seg_sum(x, seg_bf):
    return _dot_exact_rhs(x, seg_bf)


def _rwkv_out_kernel(y_ref, r_ref, k_ref, v_ref, af_ref, ab_ref, g_ref, rk_ref, ka_ref, lg_ref, lb_ref, o_ref):
    li = lax.broadcasted_iota(jnp.int32, (LANES, LANES), 0) // RW_HEAD
    lj = lax.broadcasted_iota(jnp.int32, (LANES, LANES), 1) // RW_HEAD
    seg = jnp.where(li == lj, 1.0, 0.0).astype(BF16)
    inv_n = 1.0 / RW_HEAD
    for c in range(y_ref.shape[2] // LANES):
        cols = slice(c * LANES, (c + 1) * LANES)
        y = y_ref[0, :, cols] + y_ref[1, :, cols]
        mean = _seg_sum(y, seg) * inv_n
        yc = y - mean
        var = _seg_sum(yc * yc, seg) * inv_n
        yn = yc * lax.rsqrt(var + RW_GN_EPS) * lg_ref[:, cols] + lb_ref[:, cols]
        a_mean = 0.5 * (af_ref[:, cols] + ab_ref[:, cols])
        k_bonus = k_ref[:, cols] * (1.0 + (a_mean - 1.0) * ka_ref[:, cols])
        bonus = _seg_sum(r_ref[:, cols] * k_bonus * rk_ref[:, cols], seg) * v_ref[:, cols]
        o_ref[:, cols] = ((yn + bonus) * g_ref[:, cols]).astype(o_ref.dtype)


def rwkv_out(y, r, k, v, a2, gate, r_k, k_a, ln_gain, ln_bias):
    _, T, D = y.shape
    tm = _tile(T, (256, 128, 64, 32, 16, 8))
    w = _tile(D, (512, 256, 128))
    nw = D // w
    tile = pl.BlockSpec((tm, w), lambda i, j: (i, j))
    par = pl.BlockSpec((1, w), lambda i, j: (0, j))
    return pl.pallas_call(
        _rwkv_out_kernel,
        grid=(T // tm, nw),
        in_specs=[pl.BlockSpec((2, tm, w), lambda i, j: (0, i, j)), tile, tile, tile,
                  tile, pl.BlockSpec((tm, w), lambda i, j: (i, nw + j)), tile, par, par, par, par],
        out_specs=tile,
        out_shape=jax.ShapeDtypeStruct((T, D), BF16),
        compiler_params=_cparams("parallel", "parallel"),
        name="rwkv_out",
    )(y, r, k, v, a2, a2, gate, r_k.reshape(1, D), k_a.reshape(1, D), ln_gain.reshape(1, D), ln_bias.reshape(1, D))


def _shift_mix_kernel(x_ref, mu_ref, *o_refs):
    x = x_ref[0].astype(F32)
    s = x.shape[0]
    t = lax.broadcasted_iota(jnp.int32, x.shape, 0)
    prev = jnp.where(t == 0, 0.0, pltpu.roll(x, 1, 0))
    nxt = jnp.where(t == s - 1, 0.0, pltpu.roll(x, s - 1, 0))
    dx = 0.5 * (prev + nxt) - x
    for j, o_ref in enumerate(o_refs):
        o_ref[0] = (x + dx * mu_ref[j:j + 1, :]).astype(o_ref.dtype)


def shift_mix(xn, mu):
    B, S, D = xn.shape
    J = mu.shape[0]
    w = LANES
    blk = pl.BlockSpec((1, S, w), lambda b, j: (b, 0, j))
    return pl.pallas_call(
        _shift_mix_kernel,
        grid=(B, D // w),
        in_specs=[blk, pl.BlockSpec((J, w), lambda b, j: (0, j))],
        out_specs=[blk] * J,
        out_shape=[jax.ShapeDtypeStruct((B, S, D), BF16)] * J,
        compiler_params=_cparams("parallel", "parallel"),
        name="shift_mix",
    )(xn, mu)


def rwkv_layer(h, hn, mu, w_rkv, w0, w1, w2, a0, a1, a2, g1, g2, k_k, k_a, r_k, ln_gain, ln_bias, w_o, B, S, lj=0):
    T, D = h.shape
    N = RW_HEAD
    H = D // N
    xm = [t.reshape(T, D) for t in shift_mix(hn.reshape(B, S, D), mu)]
    w_rkv = w_rkv.reshape(-1, D, D)
    r = matmul(xm[0], w_rkv, 3 * lj)
    k = matmul(xm[1], w_rkv, 3 * lj + 1)
    v = matmul(xm[2], w_rkv, 3 * lj + 2)

    def both_dirs(lo, hi):
        z = jnp.zeros_like(hi[0])
        return (jnp.concatenate([lo[0], lo[1]], axis=1),
                jnp.concatenate([jnp.concatenate([hi[0], z], axis=1), jnp.concatenate([z, hi[1]], axis=1)], axis=0))

    wa, wb = both_dirs(w1, w2)
    wl2 = lora(xm[3], wa, wb, bias=w0.reshape(-1), act1=jnp.tanh, tm=128)
    aa, ab = both_dirs(a1, a2)
    rate2 = lora(xm[4], aa, ab, bias=a0.reshape(-1), act2=jax.nn.sigmoid, tm=128)
    gate = lora(xm[5], g1, g2, act1=jax.nn.sigmoid)
    y = rwkv_scan(r, k, v, wl2, rate2, k_k, k_a, B, S)
    out = rwkv_out(y, r, k, v, rate2, gate, r_k, k_a, ln_gain, ln_bias)
    return matmul(out, w_o, lj, residual=h)


LOG2_E = 1.4426950408889634
HG_SAFE_EXP = 60.0
HG_SUB = 16
HG_BLOCK_T = 128
HG_BLOCK_H = 16


def _hgrn_scan_kernel(q_ref, zf_ref, v_ref, fb_ref, lb_ref, o_ref, s_ref):
    d = pl.program_id(2)
    L, N, SUB = HG_CHUNK, HG_DK, HG_SUB
    tb, hw = q_ref.shape
    nch, P = tb // L, hw // N

    @pl.when(pl.program_id(3) == 0)
    def _():
        s_ref[...] = jnp.zeros_like(s_ref)

    fwd = d == 0
    sgn = jnp.where(fwd, 1, -1)
    rl = lax.broadcasted_iota(jnp.int32, (L, L), 0)
    cl = lax.broadcasted_iota(jnp.int32, (L, L), 1)
    incl_bf = jnp.where((cl - rl) * sgn <= 0, 1.0, 0.0).astype(BF16)
    blk_r, blk_c = rl // SUB, cl // SUB
    first_r = blk_r * SUB + jnp.where(fwd, 0, SUB - 1)
    sel_first = jnp.where(cl == first_r, 1.0, 0.0).astype(BF16)
    blk_before = (blk_c - blk_r) * sgn < 0
    same_blk = blk_c == blk_r
    at_or_before = (cl - rl) * sgn <= 0
    ones_nl = jnp.ones((N, L), BF16)

    def stack(x):
        return jnp.stack([x[:, p * N:(p + 1) * N] for p in range(P)], axis=0)

    def within_block(x, j):
        return jnp.concatenate(
            [jnp.broadcast_to(x[:, i * SUB + j:i * SUB + j + 1, :], (P, SUB, N)) for i in range(L // SUB)], axis=1)

    lb = lb_ref[...]

    for ci in range(nch):
        t0 = pl.multiple_of(jnp.where(fwd, ci * L, (nch - 1 - ci) * L), L)
        rows = pl.ds(t0, L)
        qv = q_ref[rows, :].astype(F32)
        q = qv * jax.nn.sigmoid(qv) * (N ** -0.5)
        sig = jax.nn.sigmoid(zf_ref[rows, :].astype(F32) + fb_ref[0])
        lf = jnp.log(lb + (1.0 - lb) * sig)
        kx = (1.0 - lb) * (1.0 - sig)
        vb = stack(v_ref[rows, :])
        b = _dot_exact_lhs(incl_bf, lf)
        bend = jnp.sum(lf, axis=0, keepdims=True)
        q_dec = stack(q * jnp.exp(b)).astype(BF16)

        def chunk_factored():
            full = lax.dot_general(q_dec, stack(kx * jnp.exp(-b)).astype(BF16),
                                   (((2,), (2,)), ((0,), (0,))), preferred_element_type=F32)
            return jnp.where(at_or_before, full, 0.0)

        def block_factored():
            ref = _dot_exact_lhs(sel_first, b - lf)
            qh = stack(q * jnp.exp(b - ref)).astype(BF16)
            rows_att = []
            for i in range(L // SUB):
                ki = kx * jnp.exp(jnp.minimum(ref[i * SUB:i * SUB + 1, :] - b, 0.0))
                rows_att.append(lax.dot_general(qh[:, i * SUB:(i + 1) * SUB], stack(ki).astype(BF16),
                                                (((2,), (2,)), ((0,), (0,))), preferred_element_type=F32))
            att_off = jnp.where(blk_before, jnp.concatenate(rows_att, axis=1), 0.0)
            decay_in_block = ref - b

            def factored_diag():
                kd = stack(kx * jnp.exp(decay_in_block)).astype(BF16)
                full = lax.dot_general(qh, kd, (((2,), (2,)), ((0,), (0,))), preferred_element_type=F32)
                return jnp.where(same_blk & at_or_before, full, att_off)

            def pairwise_diag():
                att = att_off
                bs, ks, qs = stack(b * LOG2_E), stack(kx), stack(q)
                for j in range(SUB):
                    pj = (qs * within_block(ks, j)
                          * jnp.exp2(jnp.minimum(bs - within_block(bs, j), 0.0))).astype(BF16)
                    tot = jnp.dot(pj.reshape(P * L, N), ones_nl, preferred_element_type=F32).reshape(P, L, L)
                    att = jnp.where((cl == blk_r * SUB + j) & at_or_before, tot, att)
                return att

            return lax.cond(jnp.max(decay_in_block) < HG_SAFE_EXP, factored_diag, pairwise_diag)

        att = lax.cond(jnp.max(-b) < HG_SAFE_EXP, chunk_factored, block_factored)
        s0 = s_ref[...]
        o = lax.dot_general(att.astype(BF16), vb, (((2,), (1,)), ((0,), (0,))), preferred_element_type=F32)
        o = o + lax.dot_general(q_dec, s0.astype(BF16), (((2,), (2,)), ((0,), (0,))), preferred_element_type=F32)
        for p in range(P):
            o_ref[0, rows, p * N:(p + 1) * N] = o[p]
        kend = stack(kx * jnp.exp(bend - b)).astype(BF16)
        s_ref[...] = s0 * stack(jnp.exp(bend)) + lax.dot_general(
            vb, kend, (((1,), (1,)), ((0,), (0,))), preferred_element_type=F32)


def hgrn_scan(z, f_bias, lb, B, S):
    T, D5 = z.shape
    D = D5 // 5
    tb = _tile(S, (HG_BLOCK_T, HG_CHUNK))
    hw = _tile(D, (HG_BLOCK_H * HG_DK, 4 * HG_DK, 2 * HG_DK, HG_DK))
    nt, nh = S // tb, D // hw

    def trow(b, d, c):
        return b * nt + jnp.where(d == 0, c, nt - 1 - c)

    return pl.pallas_call(
        _hgrn_scan_kernel,
        grid=(B, nh, 2, nt),
        in_specs=[pl.BlockSpec((tb, hw), lambda b, g, d, c: (trow(b, d, c), g)),
                  pl.BlockSpec((tb, hw), lambda b, g, d, c: (trow(b, d, c), (1 + d) * nh + g)),
                  pl.BlockSpec((tb, hw), lambda b, g, d, c: (trow(b, d, c), 3 * nh + g)),
                  pl.BlockSpec((1, 1, hw), lambda b, g, d, c: (d, 0, g)),
                  pl.BlockSpec((1, hw), lambda b, g, d, c: (0, g))],
        out_specs=pl.BlockSpec((1, tb, hw), lambda b, g, d, c: (d, trow(b, d, c), g)),
        out_shape=jax.ShapeDtypeStruct((2, T, D), F32),
        scratch_shapes=[pltpu.VMEM((hw // HG_DK, HG_DK, HG_DK), F32)],
        compiler_params=_cparams("parallel", "parallel", "arbitrary", "arbitrary"),
        name="hgrn_scan",
    )(z, z, z, f_bias.reshape(2, 1, D), lb.reshape(1, D))


def _head_out_kernel(y_ref, g_ref, gain_ref, o_ref, *, head, act):
    y = y_ref[0] + y_ref[1]
    g = g_ref[...].astype(F32)
    gain = gain_ref[...]
    for p in range(y.shape[1] // head):
        cols = slice(p * head, (p + 1) * head)
        yp = y[:, cols]
        yn = yp * lax.rsqrt(jnp.mean(yp * yp, axis=-1, keepdims=True) + NORM_EPS) * gain[:, cols]
        o_ref[:, cols] = (yn * act(g[:, cols])).astype(o_ref.dtype)


def head_out(y, z, gate_col, gain, head, act):
    _, T, D = y.shape
    tm = _tile(T, (512, 256, 128, 64, 32, 16, 8))
    w = _tile(D, (1024, 512, 256, 128))
    w = max(w, head)
    assert gate_col % w == 0
    gb = gate_col // w
    return pl.pallas_call(
        functools.partial(_head_out_kernel, head=head, act=act),
        grid=(T // tm, D // w),
        in_specs=[pl.BlockSpec((2, tm, w), lambda i, j: (0, i, j)),
                  pl.BlockSpec((tm, w), lambda i, j: (i, gb + j)),
                  pl.BlockSpec((1, w), lambda i, j: (0, j))],
        out_specs=pl.BlockSpec((tm, w), lambda i, j: (i, j)),
        out_shape=jax.ShapeDtypeStruct((T, D), BF16),
        compiler_params=_cparams("parallel", "parallel"),
        name="head_out",
    )(y, z, gain.reshape(1, D))


def hgrn2_layer(h, hn, w_in, f_bias, lb_logits, layer_idx, head_gain, w_out, B, S, lj=0):
    T, D = h.shape
    z = matmul(hn, w_in, lj, out_dtype=BF16)
    probs = jax.nn.softmax(lb_logits, axis=0)
    lb = (jnp.cumsum(probs, axis=0) - probs[0])[layer_idx]
    o = hgrn_scan(z, f_bias, lb, B, S)
    gated = head_out(o, z, 4 * D, head_gain, HG_DK, jax.nn.silu)
    return matmul(gated, w_out, lj, residual=h)


def kernel(x, p, norm_mix, norm_ffn, norm_ple_gate, norm_ple_post, norm_final, ml_w_in, ml_gate_bias, ml_conv, ml_head_gain, ml_w_out, rw_mu, rw_w_rkv, rw_w0, rw_w1, rw_w2, rw_a0, rw_a1, rw_a2, rw_g1, rw_g2, rw_k_k, rw_k_a, rw_r_k, rw_ln_gain, rw_ln_bias, rw_w_o, hg_w_in, hg_f_bias, hg_lb, hg_head_gain, hg_w_out, moe_router, moe_w_gate, moe_w_up, moe_w_down, ple_w, ple_gate_down, ple_gate_up):
    B, S, D = x.shape
    depth = p.shape[0]
    T = B * S
    h = x.reshape(T, D)
    hn = rms_norm_bf16(h, norm_mix[0])
    for i in range(depth):
        kind, j = i % 3, i // 3
        if kind == 0:
            h = mlstm_layer(h, hn, ml_w_in, ml_gate_bias[j], ml_conv[j], ml_head_gain[j], ml_w_out, B, S, j)
        elif kind == 1:
            h = rwkv_layer(h, hn, rw_mu[j], rw_w_rkv, rw_w0[j], rw_w1[j], rw_w2[j], rw_a0[j], rw_a1[j],
                           rw_a2[j], rw_g1[j], rw_g2[j], rw_k_k[j], rw_k_a[j], rw_r_k[j], rw_ln_gain[j],
                           rw_ln_bias[j], rw_w_o, B, S, j)
        else:
            h = hgrn2_layer(h, hn, hg_w_in, hg_f_bias[j], hg_lb, i, hg_head_gain[j], hg_w_out, B, S, j)
        h = moe_layer(h, norm_ffn[i], moe_router, moe_w_gate, moe_w_up, moe_w_down, i, B, S)
        last = i + 1 == depth
        h, hn = ple_layer(h, p[i].reshape(T, -1), ple_w[i], ple_gate_down[i], ple_gate_up[i], norm_ple_gate[i],
                          norm_ple_post[i], norm_final if last else norm_mix[i + 1], F32 if last else BF16)
    return hn.reshape(B, S, D)
```
